```python
import math
import jax, jax.numpy as jnp
from jax import lax
import numpy as np

D_MODEL = 2048
BATCH = 32
SEQ = 256
DEPTH = 4
DEC_BATCH = 8
DEC_SEQ = 1024
PAST_LEN = 512

GRID_W = 64
HEAD_DIM = 128
Q_BLOCK = 128
WINDOW = 128
ROPE_THETA = 10000.0
EPS = 1e-6
NEG_INF = -1e30
N_EVEN = (DEPTH + 1) // 2
N_ODD = DEPTH // 2
MIX_W = D_MODEL
HALF_W = MIX_W // 2

A_HEADS = HALF_W // HEAD_DIM
A_KV = A_HEADS // 4
A_GROUP = A_HEADS // A_KV
A_W = A_HEADS * HEAD_DIM
A_KV_W = A_KV * HEAD_DIM
B_W = HALF_W
C_HEADS = HALF_W // (2 * HEAD_DIM)
C_V_DIM = 2 * HEAD_DIM
C_QK_W = 2 * C_HEADS * HEAD_DIM
C_W = C_HEADS * C_V_DIM
D_HEADS = HALF_W // HEAD_DIM
D_KV = D_HEADS // 4
D_GROUP = D_HEADS // D_KV
D_W = D_HEADS * HEAD_DIM
D_KV_W = D_KV * HEAD_DIM
EVEN_SPLIT = (A_W, A_KV_W, A_KV_W, A_W, B_W, B_W, B_W, B_W)
ODD_SPLIT = (C_QK_W, C_QK_W, C_W, C_W, D_W, D_KV_W, D_KV_W, D_W)
EVEN_IN = sum(EVEN_SPLIT)
ODD_IN = sum(ODD_SPLIT)

kernel_name = 'hybrid_diffusion_prefix_trunk_step'


def rms_norm(x, g):
    xf = x.astype(jnp.float32)
    y = xf * lax.rsqrt(jnp.mean(xf * xf, axis=-1, keepdims=True) + EPS)
    return y.astype(x.dtype) * g


def modulation(cond, w, b):
    m = jax.nn.silu(cond) @ w + b
    if m.ndim == 2:
        m = m[:, None, :]
    return jnp.split(m, 3, axis=-1)


def split_cols(p, sizes):
    idx = [int(s) for s in np.cumsum(sizes)[:-1]]
    return jnp.split(p, idx, axis=-1)


def axial_rope_tables(n_tokens):
    rows = n_tokens // GRID_W
    row = jnp.repeat(jnp.arange(rows), GRID_W)
    col = jnp.tile(jnp.arange(GRID_W), rows)
    n_freq = HEAD_DIM // 4
    inv = ROPE_THETA ** (-jnp.arange(n_freq, dtype=jnp.float32) / n_freq)
    ang = jnp.stack([row[:, None] * inv, col[:, None] * inv], axis=1)
    return jnp.cos(ang), jnp.sin(ang)


def apply_rope(x, cos, sin):
    shp = x.shape
    xr = x.reshape(shp[:-1] + (2, 2, HEAD_DIM // 4))
    x1, x2 = xr[..., 0, :], xr[..., 1, :]
    c = cos.astype(x.dtype)
    s = sin.astype(x.dtype)
    out = jnp.stack([x1 * c - x2 * s, x1 * s + x2 * c], axis=-2)
    return out.reshape(shp)


def q_heads(x, n_kv, group):
    b, t, _ = x.shape
    return x.reshape(b, t, n_kv, group, HEAD_DIM).transpose(0, 2, 3, 1, 4)


def kv_heads(x, n_kv):
    b, t, _ = x.shape
    return x.reshape(b, t, n_kv, HEAD_DIM).transpose(0, 2, 1, 3)


def merge_heads(o):
    b, k, g, t, d = o.shape
    return o.transpose(0, 3, 1, 2, 4).reshape(b, t, k * g * d)


def softmax_with_sink(s, sink):
    sk = jnp.broadcast_to(sink, s.shape[:-1] + (1,))
    return jax.nn.softmax(jnp.concatenate([sk, s], axis=-1), axis=-1)[..., 1:]


def attend(q, k, v, sink=None):
    b, nk, g, t, d = q.shape
    nb = t // Q_BLOCK
    qb = jnp.moveaxis(q.reshape(b, nk, g, nb, Q_BLOCK, d), 3, 0)
    scale = d ** -0.5

    def one_block(qi):
        s = jnp.einsum('bkgqd,bksd->bkgqs', qi, k).astype(jnp.float32) * scale
        if sink is None:
            p = jax.nn.softmax(s, axis=-1)
        else:
            p = softmax_with_sink(s, sink.astype(jnp.float32)[None, :, :, None, None])
        return jnp.einsum('bkgqs,bksv->bkgqv', p.astype(v.dtype), v)

    o = lax.map(one_block, qb)
    return jnp.moveaxis(o, 0, 3).reshape(b, nk, g, t, v.shape[-1])


def band_blocks(x, nb):
    b, nk, t, d = x.shape
    xp = jnp.pad(x, ((0, 0), (0, 0), (WINDOW, WINDOW), (0, 0))).reshape(b, nk, nb + 2, WINDOW, d)
    return jnp.concatenate([xp[:, :, :-2], xp[:, :, 1:-1], xp[:, :, 2:]], axis=3)


def banded_attend(q, k, v, k_ctx, v_ctx, sink):
    b, nk, g, t, d = q.shape
    nb = t // WINDOW
    qb = q.reshape(b, nk, g, nb, WINDOW, d)
    kb = band_blocks(k, nb)
    vb = band_blocks(v, nb)
    scale = d ** -0.5
    s_ctx = jnp.einsum('bkgnqd,bksd->bkgnqs', qb, k_ctx).astype(jnp.float32) * scale
    s_band = jnp.einsum('bkgnqd,bknsd->bkgnqs', qb, kb).astype(jnp.float32) * scale
    blk = jnp.arange(nb)[:, None, None]
    qpos = blk * WINDOW + jnp.arange(WINDOW)[None, :, None]
    kpos = (blk - 1) * WINDOW + jnp.arange(3 * WINDOW)[None, None, :]
    valid = (jnp.abs(kpos - qpos) <= WINDOW) & (kpos >= 0) & (kpos < t)
    s_band = jnp.where(valid, s_band, NEG_INF)
    n_ctx = k_ctx.shape[2]
    p = softmax_with_sink(jnp.concatenate([s_ctx, s_band], axis=-1),
                          sink.astype(jnp.float32)[None, :, :, None, None, None]).astype(v.dtype)
    o = (jnp.einsum('bkgnqs,bksv->bkgnqv', p[..., :n_ctx], v_ctx)
         + jnp.einsum('bkgnqs,bknsv->bkgnqv', p[..., n_ctx:], vb))
    return o.reshape(b, nk, g, t, d)


def gated_short_conv(u, gate_b, gate_c, w):
    z = gate_c * u
    zp = jnp.pad(z, ((0, 0), (1, 1), (0, 0)))
    conv = zp[:, :-2] * w[0] + zp[:, 1:-1] * w[1] + zp[:, 2:] * w[2]
    return gate_b * conv


def even_mixer(h, w_in, w_out, qn, kn, sink, conv_w, ctx=None, rope=None):
    qa, ka, va, ga, ub, bb, cb, gb = split_cols(h @ w_in, EVEN_SPLIT)
    q = rms_norm(q_heads(qa, A_KV, A_GROUP), qn)
    k = rms_norm(kv_heads(ka, A_KV), kn)
    v = kv_heads(va, A_KV)
    sink_g = sink.reshape(A_KV, A_GROUP)
    if ctx is None:
        o = attend(q, k, v, sink_g)
        new = (k, v)
    else:
        cos, sin = rope
        o = banded_attend(apply_rope(q, cos, sin), apply_rope(k, cos, sin), v, ctx[0], ctx[1], sink_g)
        new = None
    y_a = merge_heads(o) * jax.nn.silu(ga)
    y_b = gated_short_conv(ub, bb, cb, conv_w) * jax.nn.silu(gb)
    return jnp.concatenate([y_a, y_b], axis=-1) @ w_out, new


def odd_mixer(h, w_in, w_out, cqn, ckn, c_lam, c_on, dqn, dkn, lam_init, ctx=None, rope=None):
    cq, ck, cv, cg, dq, dk, dv, dg = split_cols(h @ w_in, ODD_SPLIT)
    b, t, _ = h.shape
    cq = rms_norm(cq.reshape(b, t, 2, C_HEADS, HEAD_DIM).transpose(0, 2, 3, 1, 4), cqn)
    ck = rms_norm(ck.reshape(b, t, 2, C_HEADS, HEAD_DIM).transpose(0, 2, 3, 1, 4), ckn)
    cv = cv.reshape(b, t, C_HEADS, C_V_DIM).transpose(0, 2, 1, 3)
    q_d = rms_norm(q_heads(dq, D_KV, D_GROUP), dqn)
    k_d = rms_norm(kv_heads(dk, D_KV), dkn)
    v_d = kv_heads(dv, D_KV)
    if ctx is None:
        ck_all, cv_all, dk_all, dv_all = ck, cv, k_d, v_d
        new = (ck, cv, k_d, v_d)
    else:
        cos, sin = rope
        cq = apply_rope(cq, cos, sin)
        q_d = apply_rope(q_d, cos, sin)
        ck_all = jnp.concatenate([ctx[0], apply_rope(ck, cos, sin)], axis=3)
        cv_all = jnp.concatenate([ctx[1], cv], axis=2)
        dk_all = jnp.concatenate([ctx[2], apply_rope(k_d, cos, sin)], axis=2)
        dv_all = jnp.concatenate([ctx[3], v_d], axis=2)
        new = None
    lf = c_lam.astype(jnp.float32)
    lam = jnp.exp(jnp.sum(lf[0] * lf[1])) - jnp.exp(jnp.sum(lf[2] * lf[3])) + lam_init
    o1 = attend(cq[:, 0][:, :, None], ck_all[:, 0], cv_all)[:, :, 0]
    o2 = attend(cq[:, 1][:, :, None], ck_all[:, 1], cv_all)[:, :, 0]
    o_c = rms_norm(o1 - lam.astype(o1.dtype) * o2, c_on) * (1.0 - lam_init)
    y_c = o_c.transpose(0, 2, 1, 3).reshape(b, t, C_W) * jax.nn.silu(cg)
    y_d = merge_heads(attend(q_d, dk_all, dv_all)) * jax.nn.silu(dg)
    return jnp.concatenate([y_c, y_d], axis=-1) @ w_out, new


def setup_inputs(seed: int = 0) -> dict:
    key = jax.random.key(seed)
    ks = jax.random.split(key, 32)

    def nrm(k, shape, scale):
        return jax.random.normal(k, shape, jnp.float32) * scale

    def gain(k, shape):
        return 1.0 + nrm(k, shape, 0.05)

    return {
        'x_prompt': nrm(ks[0], (BATCH, SEQ, D_MODEL), 1.0),
        'x_sample': nrm(ks[1], (DEC_BATCH, DEC_SEQ, D_MODEL), 1.0),
        'cache_a_k': nrm(ks[2], (DEC_BATCH, N_EVEN, A_KV, PAST_LEN, HEAD_DIM), 1.0),
        'cache_a_v': nrm(ks[3], (DEC_BATCH, N_EVEN, A_KV, PAST_LEN, HEAD_DIM), 1.0),
        'cache_c_k': nrm(ks[4], (DEC_BATCH, N_ODD, 2, C_HEADS, PAST_LEN, HEAD_DIM), 1.0),
        'cache_c_v': nrm(ks[5], (DEC_BATCH, N_ODD, C_HEADS, PAST_LEN, C_V_DIM), 1.0),
        'cache_d_k': nrm(ks[6], (DEC_BATCH, N_ODD, D_KV, PAST_LEN, HEAD_DIM), 1.0),
        'cache_d_v': nrm(ks[7], (DEC_BATCH, N_ODD, D_KV, PAST_LEN, HEAD_DIM), 1.0),
        'c': nrm(ks[8], (DEC_BATCH, D_MODEL), 1.0),
        'c_ctx': nrm(ks[9], (D_MODEL,), 1.0),
        'norm_g': gain(ks[10], (DEPTH, D_MODEL)),
        'ada_w': nrm(ks[11], (DEPTH, D_MODEL, 3 * D_MODEL), 0.5 * D_MODEL ** -0.5),
        'ada_b': nrm(ks[12], (DEPTH, 3 * D_MODEL), 0.01),
        'ev_w_in': nrm(ks[13], (N_EVEN, D_MODEL, EVEN_IN), D_MODEL ** -0.5),
        'ev_w_out': nrm(ks[14], (N_EVEN, MIX_W, D_MODEL), MIX_W ** -0.5),
        'a_q_norm': gain(ks[15], (N_EVEN, HEAD_DIM)),
        'a_k_norm': gain(ks[16], (N_EVEN, HEAD_DIM)),
        'a_sink': nrm(ks[17], (N_EVEN, A_HEADS), 0.5),
        'b_conv': nrm(ks[18], (N_EVEN, 3, B_W), 0.5),
        'od_w_in': nrm(ks[19], (N_ODD, D_MODEL, ODD_IN), D_MODEL ** -0.5),
        'od_w_out': nrm(ks[20], (N_ODD, MIX_W, D_MODEL), MIX_W ** -0.5),
        'c_q_norm': gain(ks[21], (N_ODD, HEAD_DIM)),
        'c_k_norm': gain(ks[22], (N_ODD, HEAD_DIM)),
        'c_lambda': nrm(ks[23], (N_ODD, 4, HEAD_DIM), 0.1),
        'c_out_norm': gain(ks[24], (N_ODD, C_V_DIM)),
        'd_q_norm': gain(ks[25], (N_ODD, HEAD_DIM)),
        'd_k_norm': gain(ks[26], (N_ODD, HEAD_DIM)),
    }


def reference(x_prompt, x_sample, cache_a_k, cache_a_v, cache_c_k, cache_c_v, cache_d_k, cache_d_v,
              c, c_ctx, norm_g, ada_w, ada_b, ev_w_in, ev_w_out, a_q_norm, a_k_norm, a_sink, b_conv,
              od_w_in, od_w_out, c_q_norm, c_k_norm, c_lambda, c_out_norm, d_q_norm, d_k_norm):
    rope = axial_rope_tables(x_sample.shape[1])
    xp, xs = x_prompt, x_sample
    a_k, a_v, c_k, c_v, d_k, d_v = [], [], [], [], [], []
    for layer in range(DEPTH):
        sh_p, sc_p, gt_p = modulation(c_ctx, ada_w[layer], ada_b[layer])
        sh_s, sc_s, gt_s = modulation(c, ada_w[layer], ada_b[layer])
        hp = rms_norm(xp, norm_g[layer]) * (1.0 + sc_p) + sh_p
        hs = rms_norm(xs, norm_g[layer]) * (1.0 + sc_s) + sh_s
        i = layer // 2
        if layer % 2 == 0:
            w = (ev_w_in[i], ev_w_out[i], a_q_norm[i], a_k_norm[i], a_sink[i], b_conv[i])
            out_p, (kc, vc) = even_mixer(hp, *w)
            out_s, _ = even_mixer(hs, *w, ctx=(cache_a_k[:, i], cache_a_v[:, i]), rope=rope)
            a_k.append(kc)
            a_v.append(vc)
        else:
            lam_init = 0.8 - 0.6 * math.exp(-0.3 * layer)
            w = (od_w_in[i], od_w_out[i], c_q_norm[i], c_k_norm[i], c_lambda[i], c_out_norm[i],
                 d_q_norm[i], d_k_norm[i], lam_init)
            out_p, (ckc, cvc, dkc, dvc) = odd_mixer(hp, *w)
            out_s, _ = odd_mixer(hs, *w, ctx=(cache_c_k[:, i], cache_c_v[:, i], cache_d_k[:, i], cache_d_v[:, i]),
                                 rope=rope)
            c_k.append(ckc)
            c_v.append(cvc)
            d_k.append(dkc)
            d_v.append(dvc)
        xp = xp + gt_p * out_p
        xs = xs + gt_s * out_s
    return (xp, xs, jnp.stack(a_k, axis=1), jnp.stack(a_v, axis=1), jnp.stack(c_k, axis=1),
            jnp.stack(c_v, axis=1), jnp.stack(d_k, axis=1), jnp.stack(d_v, axis=1))
```

```python
import functools
import math

import jax
import jax.numpy as jnp
from jax import lax
from jax.experimental import pallas as pl
from jax.experimental.pallas import tpu as pltpu

D_MODEL = 2048
DEPTH = 4
GRID_W = 64
HEAD_DIM = 128
WINDOW = 128
ROPE_THETA = 10000.0
EPS = 1e-6
NEG_INF = -1e30
HALF_W = D_MODEL // 2
N_KV = 2
GROUP = 4
C_HEADS = 4
C_V_DIM = 2 * HEAD_DIM
COND_ROWS = 16
CTX_ROW = 8

BF16 = jnp.bfloat16
F32 = jnp.float32

VMEM_LIMIT_BYTES = 52 * 1024 * 1024
PROJ_ROWS = 512
PROJ_CHUNK = 512
SAMPLE_QB = 128
PROMPT_SEQS_PER_STEP = 4


def _params(n_axes=1):
    return pltpu.CompilerParams(
        dimension_semantics=("arbitrary",) * n_axes,
        vmem_limit_bytes=VMEM_LIMIT_BYTES,
    )


def _dot(a, b):
    return jnp.dot(a, b, preferred_element_type=F32)


def _dot_nt(a, b):
    return lax.dot_general(a, b, (((1,), (1,)), ((), ())), preferred_element_type=F32)


def _silu(x):
    return x / (1.0 + jnp.exp(-x))


def _mod_kernel(cond_ref, w_ref, b_ref, o_ref):
    s = _silu(cond_ref[...]).astype(BF16)
    o_ref[...] = _dot(s, w_ref[...].astype(BF16)) + b_ref[...]


def _modulation(cond, ada_w, ada_b):
    bn = 1024
    n_out = 3 * D_MODEL
    return pl.pallas_call(
        _mod_kernel,
        grid=(DEPTH, n_out // bn),
        in_specs=[
            pl.BlockSpec((COND_ROWS, D_MODEL), lambda l, n: (0, 0)),
            pl.BlockSpec((None, D_MODEL, bn), lambda l, n: (l, 0, n)),
            pl.BlockSpec((None, 1, bn), lambda l, n: (l, 0, n)),
        ],
        out_specs=pl.BlockSpec((None, COND_ROWS, bn), lambda l, n: (l, 0, n)),
        out_shape=jax.ShapeDtypeStruct((DEPTH, COND_ROWS, n_out), F32),
        compiler_params=_params(2),
        name="modulation",
    )(cond, ada_w, ada_b.reshape(DEPTH, 1, n_out))


def _mod_spec(layer, part, row_fn):
    return pl.BlockSpec((None, None, 1, D_MODEL), lambda m: (layer, row_fn(m), 0, part))


def _norm_mod(x, r, g, sc, sh):
    return ((x * r) * g) * (1.0 + sc) + sh


def _prenorm_kernel(x_ref, g_ref, sh_ref, sc_ref, h_ref):
    x = x_ref[...]
    r = lax.rsqrt(jnp.mean(x * x, axis=-1, keepdims=True) + EPS)
    h_ref[...] = _norm_mod(x, r, g_ref[...], sc_ref[...], sh_ref[...]).astype(BF16)


def _prenorm(x, norm_g3, mod4, layer, row_fn):
    m_rows = x.shape[0]
    bm = PROJ_ROWS
    return pl.pallas_call(
        _prenorm_kernel,
        grid=(m_rows // bm,),
        in_specs=[
            pl.BlockSpec((bm, D_MODEL), lambda m: (m, 0)),
            pl.BlockSpec((None, 1, D_MODEL), lambda m: (layer, 0, 0)),
            _mod_spec(layer, 0, row_fn),
            _mod_spec(layer, 1, row_fn),
        ],
        out_specs=pl.BlockSpec((bm, D_MODEL), lambda m: (m, 0)),
        out_shape=jax.ShapeDtypeStruct((m_rows, D_MODEL), BF16),
        compiler_params=_params(),
        name="prenorm",
    )(x, norm_g3, mod4, mod4)


class _Seg:
    def __init__(self, kind, width, cache=None):
        self.kind = kind
        self.width = width
        self.cache = cache


def _rope(y, cos, sin_signed, first_half):
    swapped = jnp.where(first_half, pltpu.roll(y, 96, axis=1), pltpu.roll(y, 32, axis=1))
    return y * cos + swapped * sin_signed


def _inproj_kernel(*refs, segs, rope, seq_len, bm):
    it = iter(refs)
    h_ref = next(it)
    w_ref = next(it)
    if rope:
        cos_ref = next(it)
        sin_ref = next(it)
    gain_refs = [next(it) if s.kind == "norm" else None for s in segs]
    out_refs = [next(it) for _ in segs]
    cache_refs = [next(it) if s.cache else None for s in segs]

    if rope:
        cos = cos_ref[...]
        sin_signed = sin_ref[...]
        lane = lax.broadcasted_iota(jnp.int32, (bm, HEAD_DIM), 1)
        first_half = (lane % 64) < 32

    col = 0
    for seg, gain_ref, out_ref, cache_ref in zip(segs, gain_refs, out_refs, cache_refs):
        for c0 in range(0, seg.width, PROJ_CHUNK):
            cw = min(PROJ_CHUNK, seg.width - c0)
            acc = _dot(h_ref[...], w_ref[:, col + c0:col + c0 + cw])
            if seg.kind == "norm":
                heads = []
                for hh in range(cw // HEAD_DIM):
                    a = acc[:, hh * HEAD_DIM:(hh + 1) * HEAD_DIM]
                    r = lax.rsqrt(jnp.mean(a * a, axis=-1, keepdims=True) + EPS)
                    y = (a * r) * gain_ref[:, c0 + hh * HEAD_DIM:c0 + (hh + 1) * HEAD_DIM]
                    if rope:
                        y = _rope(y, cos, sin_signed, first_half)
                    heads.append(y)
                val = jnp.concatenate(heads, axis=1) if len(heads) > 1 else heads[0]
            elif seg.kind == "silu":
                val = _silu(acc)
            else:
                val = acc
            out_ref[:, c0:c0 + cw] = val.astype(BF16)
            if seg.cache:
                _, hw = seg.cache
                for j in range(bm // seq_len):
                    for hh in range(cw // hw):
                        cache_ref[j, (c0 // hw) + hh] = val[j * seq_len:(j + 1) * seq_len,
                                                            hh * hw:(hh + 1) * hw]
        col += seg.width


def _inproj(h, w_slab, segs, gains, rope_tabs, seq_len):
    m_rows = h.shape[0]
    bm = PROJ_ROWS
    n_cols = w_slab.shape[1]
    rope = rope_tabs is not None
    in_specs = [
        pl.BlockSpec((bm, D_MODEL), lambda m: (m, 0)),
        pl.BlockSpec((D_MODEL, n_cols), lambda m: (0, 0)),
    ]
    args = [h, w_slab]
    if rope:
        blocks_per_seq = seq_len // bm
        tab_spec = pl.BlockSpec((bm, HEAD_DIM), lambda m: (m % blocks_per_seq, 0))
        in_specs += [tab_spec, tab_spec]
        args += list(rope_tabs)
    for seg, g in zip(segs, gains):
        if seg.kind == "norm":
            in_specs.append(pl.BlockSpec((1, seg.width), lambda m: (0, 0)))
            args.append(g)
    out_specs = [pl.BlockSpec((bm, s.width), lambda m: (m, 0)) for s in segs]
    out_shape = [jax.ShapeDtypeStruct((m_rows, s.width), BF16) for s in segs]
    seqs_per_block = bm // seq_len if seq_len <= bm else 0
    for s in segs:
        if s.cache:
            nh, hw = s.cache
            out_specs.append(pl.BlockSpec((seqs_per_block, nh, seq_len, hw), lambda m: (m, 0, 0, 0)))
            out_shape.append(jax.ShapeDtypeStruct((m_rows // seq_len, nh, seq_len, hw), F32))
    outs = pl.pallas_call(
        functools.partial(_inproj_kernel, segs=segs, rope=rope, seq_len=seq_len, bm=bm),
        grid=(m_rows // bm,),
        in_specs=in_specs,
        out_specs=out_specs,
        out_shape=out_shape,
        compiler_params=_params(),
        name="inproj",
    )(*args)
    n = len(segs)
    return list(outs[:n]), list(outs[n:])


def _softmax_pv(scores, values, extra=None):
    m = jnp.max(scores[0], axis=-1, keepdims=True)
    for s in scores[1:]:
        m = jnp.maximum(m, jnp.max(s, axis=-1, keepdims=True))
    if extra is not None:
        m = jnp.maximum(m, extra)
    denom = None
    out = None
    for s, v in zip(scores, values):
        e = jnp.exp(s - m)
        part = jnp.sum(e, axis=-1, keepdims=True)
        pv = _dot(e.astype(BF16), v)
        denom = part if denom is None else denom + part
        out = pv if out is None else out + pv
    if extra is not None:
        denom = denom + jnp.exp(extra - m)
    return out * (1.0 / denom)


def _stack_heads(ref, r0, rows, first_head):
    return jnp.concatenate(
        [ref[pl.ds(r0, rows), (first_head + g) * HEAD_DIM:(first_head + g + 1) * HEAD_DIM]
         for g in range(GROUP)], axis=0)


def _sink_column(sink_ref, first_head, rows):
    return jnp.concatenate(
        [jnp.full((rows, 1), sink_ref[first_head + g], F32) for g in range(GROUP)], axis=0)


def _store_gated(o_ref, g_ref, o, r0, rows, first_head):
    for g in range(GROUP):
        cs = slice((first_head + g) * HEAD_DIM, (first_head + g + 1) * HEAD_DIM)
        gate = g_ref[pl.ds(r0, rows), cs].astype(F32)
        o_ref[pl.ds(r0, rows), cs] = (o[g * rows:(g + 1) * rows] * gate).astype(BF16)


def _gqa_kernel(*refs, seq_len, qb, n_blocks, ctx, sink):
    it = iter(refs)
    sink_ref = next(it) if sink else None
    q_ref, k_ref, v_ref, g_ref = next(it), next(it), next(it), next(it)
    if ctx:
        ck_ref, cv_ref = next(it), next(it)
    o_ref = next(it)
    if ctx:
        kc_s, vc_s = next(it), next(it)
        for kv in range(N_KV):
            kc_s[kv] = ck_ref[kv].astype(BF16)
            vc_s[kv] = cv_ref[kv].astype(BF16)
    blocks_per_seq = seq_len // qb

    def body(n, carry):
        r0 = pl.multiple_of(n * qb, qb)
        k0 = pl.multiple_of((n // blocks_per_seq) * seq_len, seq_len)
        for kv in range(N_KV):
            hs = slice(kv * HEAD_DIM, (kv + 1) * HEAD_DIM)
            q = _stack_heads(q_ref, r0, qb, kv * GROUP)
            k = k_ref[pl.ds(k0, seq_len), hs]
            v = v_ref[pl.ds(k0, seq_len), hs]
            scores, values = [], []
            if ctx:
                scores.append(_dot_nt(q, kc_s[kv]))
                values.append(vc_s[kv])
            scores.append(_dot_nt(q, k))
            values.append(v)
            extra = _sink_column(sink_ref, kv * GROUP, qb) if sink else None
            o = _softmax_pv(scores, values, extra)
            _store_gated(o_ref, g_ref, o, r0, qb, kv * GROUP)
        return carry

    lax.fori_loop(0, n_blocks, body, 0)


def _gqa(q, k, v, gate, seq_len, qb, rows_per_step, sink=None, ctx=None, layer_idx=0):
    m_rows = q.shape[0]
    kvw = N_KV * HEAD_DIM
    in_specs, args = [], []
    if sink is not None:
        in_specs.append(pl.BlockSpec(memory_space=pltpu.SMEM))
        args.append(sink)
    in_specs += [
        pl.BlockSpec((rows_per_step, HALF_W), lambda b: (b, 0)),
        pl.BlockSpec((rows_per_step, kvw), lambda b: (b, 0)),
        pl.BlockSpec((rows_per_step, kvw), lambda b: (b, 0)),
        pl.BlockSpec((rows_per_step, HALF_W), lambda b: (b, 0)),
    ]
    args += [q, k, v, gate]
    scratch = []
    if ctx is not None:
        past = ctx[0].shape[-2]
        cspec = pl.BlockSpec((None, None, N_KV, past, HEAD_DIM), lambda b: (b, layer_idx, 0, 0, 0))
        in_specs += [cspec, cspec]
        args += list(ctx)
        scratch = [pltpu.VMEM((N_KV, past, HEAD_DIM), BF16)] * 2
    return pl.pallas_call(
        functools.partial(_gqa_kernel, seq_len=seq_len, qb=qb, n_blocks=rows_per_step // qb,
                          ctx=ctx is not None, sink=sink is not None),
        grid=(m_rows // rows_per_step,),
        in_specs=in_specs,
        out_specs=pl.BlockSpec((rows_per_step, HALF_W), lambda b: (b, 0)),
        out_shape=jax.ShapeDtypeStruct((m_rows, HALF_W), BF16),
        scratch_shapes=scratch,
        compiler_params=_params(),
        name="gqa",
    )(*args)


def _banded_kernel(sink_ref, q_ref, k_ref, v_ref, g_ref, ck_ref, cv_ref, o_ref,
                   kpad, vpad, kc_s, vc_s, *, seq_len):
    n_blocks = seq_len // WINDOW
    zeros = jnp.zeros((WINDOW, HEAD_DIM), BF16)
    for kv in range(N_KV):
        hs = slice(kv * HEAD_DIM, (kv + 1) * HEAD_DIM)
        kpad[kv, 0:WINDOW] = zeros
        vpad[kv, 0:WINDOW] = zeros
        kpad[kv, WINDOW + seq_len:2 * WINDOW + seq_len] = zeros
        vpad[kv, WINDOW + seq_len:2 * WINDOW + seq_len] = zeros
        kpad[kv, WINDOW:WINDOW + seq_len] = k_ref[:, hs]
        vpad[kv, WINDOW:WINDOW + seq_len] = v_ref[:, hs]
        kc_s[kv] = ck_ref[kv].astype(BF16)
        vc_s[kv] = cv_ref[kv].astype(BF16)

    rows = GROUP * WINDOW
    qi = lax.broadcasted_iota(jnp.int32, (rows, 3 * WINDOW), 0) % WINDOW
    col = lax.broadcasted_iota(jnp.int32, (rows, 3 * WINDOW), 1)
    kj = col % WINDOW
    blk = col // WINDOW
    never = 4 * WINDOW

    def body(n, carry):
        r0 = pl.multiple_of(n * WINDOW, WINDOW)
        lo = jnp.where(n > 0, 0, never)
        hi = jnp.where(n < n_blocks - 1, 0, never)
        slack = jnp.where(blk == 0, kj - qi - lo, jnp.where(blk == 2, qi - kj - hi, 0))
        valid = slack >= 0
        for kv in range(N_KV):
            q = _stack_heads(q_ref, r0, WINDOW, kv * GROUP)
            kb = kpad[kv, pl.ds(r0, 3 * WINDOW), :]
            vb = vpad[kv, pl.ds(r0, 3 * WINDOW), :]
            s_ctx = _dot_nt(q, kc_s[kv])
            s_band = jnp.where(valid, _dot_nt(q, kb), NEG_INF)
            o = _softmax_pv([s_ctx, s_band], [vc_s[kv], vb],
                            _sink_column(sink_ref, kv * GROUP, WINDOW))
            _store_gated(o_ref, g_ref, o, r0, WINDOW, kv * GROUP)
        return carry

    lax.fori_loop(0, n_blocks, body, 0)


def _banded(q, k, v, gate, sink, cache_k, cache_v, layer_idx, seq_len):
    m_rows = q.shape[0]
    kvw = N_KV * HEAD_DIM
    past = cache_k.shape[-2]
    cspec = pl.BlockSpec((None, None, N_KV, past, HEAD_DIM), lambda b: (b, layer_idx, 0, 0, 0))
    pad_rows = seq_len + 2 * WINDOW
    return pl.pallas_call(
        functools.partial(_banded_kernel, seq_len=seq_len),
        grid=(m_rows // seq_len,),
        in_specs=[
            pl.BlockSpec(memory_space=pltpu.SMEM),
            pl.BlockSpec((seq_len, HALF_W), lambda b: (b, 0)),
            pl.BlockSpec((seq_len, kvw), lambda b: (b, 0)),
            pl.BlockSpec((seq_len, kvw), lambda b: (b, 0)),
            pl.BlockSpec((seq_len, HALF_W), lambda b: (b, 0)),
            cspec, cspec,
        ],
        out_specs=pl.BlockSpec((seq_len, HALF_W), lambda b: (b, 0)),
        out_shape=jax.ShapeDtypeStruct((m_rows, HALF_W), BF16),
        scratch_shapes=[
            pltpu.VMEM((N_KV, pad_rows, HEAD_DIM), BF16),
            pltpu.VMEM((N_KV, pad_rows, HEAD_DIM), BF16),
            pltpu.VMEM((N_KV, past, HEAD_DIM), BF16),
            pltpu.VMEM((N_KV, past, HEAD_DIM), BF16),
        ],
        compiler_params=_params(),
        name="banded",
    )(sink, q, k, v, gate, cache_k, cache_v)


def _diff_kernel(*refs, seq_len, qb, n_blocks, ctx, lam_init):
    it = iter(refs)
    lam_ref, on_ref = next(it), next(it)
    q_ref, k_ref, v_ref, g_ref = next(it), next(it), next(it), next(it)
    if ctx:
        ck_ref, cv_ref = next(it), next(it)
    o_ref = next(it)
    if ctx:
        kc_s, vc_s = next(it), next(it)
        for slot in range(2 * C_HEADS):
            kc_s[slot] = ck_ref[slot].astype(BF16)
        for hd in range(C_HEADS):
            vc_s[hd] = cv_ref[hd].astype(BF16)
    lf = lam_ref[...]
    lam = (jnp.exp(jnp.sum(lf[0:1] * lf[1:2], axis=-1, keepdims=True))
           - jnp.exp(jnp.sum(lf[2:3] * lf[3:4], axis=-1, keepdims=True)) + lam_init)
    out_gain = on_ref[...]
    blocks_per_seq = seq_len // qb

    def body(n, carry):
        r0 = pl.multiple_of(n * qb, qb)
        k0 = pl.multiple_of((n // blocks_per_seq) * seq_len, seq_len)
        for hd in range(C_HEADS):
            vs = slice(hd * C_V_DIM, (hd + 1) * C_V_DIM)
            v = v_ref[pl.ds(k0, seq_len), vs]
            maps = []
            for mp in range(2):
                slot = mp * C_HEADS + hd
                hs = slice(slot * HEAD_DIM, (slot + 1) * HEAD_DIM)
                q = q_ref[pl.ds(r0, qb), hs]
                k = k_ref[pl.ds(k0, seq_len), hs]
                scores, values = [], []
                if ctx:
                    scores.append(_dot_nt(q, kc_s[slot]))
                    values.append(vc_s[hd])
                scores.append(_dot_nt(q, k))
                values.append(v)
                maps.append(_softmax_pv(scores, values))
            d = maps[0] - lam * maps[1]
            r = lax.rsqrt(jnp.mean(d * d, axis=-1, keepdims=True) + EPS)
            y = ((d * r) * out_gain) * (1.0 - lam_init)
            gate = g_ref[pl.ds(r0, qb), vs].astype(F32)
            o_ref[pl.ds(r0, qb), vs] = (y * gate).astype(BF16)
        return carry

    lax.fori_loop(0, n_blocks, body, 0)


def _diff(q, k, v, gate, c_lam, c_on, lam_init, seq_len, qb, rows_per_step, ctx=None, layer_idx=0):
    m_rows = q.shape[0]
    in_specs = [
        pl.BlockSpec((4, HEAD_DIM), lambda b: (0, 0)),
        pl.BlockSpec((1, C_V_DIM), lambda b: (0, 0)),
    ] + [pl.BlockSpec((rows_per_step, HALF_W), lambda b: (b, 0))] * 4
    args = [c_lam, c_on, q, k, v, gate]
    scratch = []
    if ctx is not None:
        past = ctx[0].shape[-2]
        in_specs += [
            pl.BlockSpec((None, None, 2 * C_HEADS, past, HEAD_DIM), lambda b: (b, layer_idx, 0, 0, 0)),
            pl.BlockSpec((None, None, C_HEADS, past, C_V_DIM), lambda b: (b, layer_idx, 0, 0, 0)),
        ]
        args += list(ctx)
        scratch = [pltpu.VMEM((2 * C_HEADS, past, HEAD_DIM), BF16),
                   pltpu.VMEM((C_HEADS, past, C_V_DIM), BF16)]
    return pl.pallas_call(
        functools.partial(_diff_kernel, seq_len=seq_len, qb=qb, n_blocks=rows_per_step // qb,
                          ctx=ctx is not None, lam_init=lam_init),
        grid=(m_rows // rows_per_step,),
        in_specs=in_specs,
        out_specs=pl.BlockSpec((rows_per_step, HALF_W), lambda b: (b, 0)),
        out_shape=jax.ShapeDtypeStruct((m_rows, HALF_W), BF16),
        scratch_shapes=scratch,
        compiler_params=_params(),
        name="diff_attn",
    )(*args)


def _conv_kernel(u_ref, b_ref, c_ref, g_ref, w_ref, o_ref, *, seq_len, rows):
    z = c_ref[...].astype(F32) * u_ref[...].astype(F32)
    t = lax.broadcasted_iota(jnp.int32, (rows, 1), 0) % seq_len
    z_prev = jnp.where(t == 0, 0.0, pltpu.roll(z, 1, axis=0))
    z_next = jnp.where(t == seq_len - 1, 0.0, pltpu.roll(z, rows - 1, axis=0))
    w = w_ref[...]
    conv = z_prev * w[0:1] + z * w[1:2] + z_next * w[2:3]
    o_ref[...] = ((b_ref[...].astype(F32) * conv) * g_ref[...].astype(F32)).astype(BF16)


def _short_conv(u, bgate, cgate, gate, conv_w, seq_len):
    m_rows = u.shape[0]
    rows = max(seq_len, 512)
    spec = pl.BlockSpec((rows, HALF_W), lambda m: (m, 0))
    return pl.pallas_call(
        functools.partial(_conv_kernel, seq_len=seq_len, rows=rows),
        grid=(m_rows // rows,),
        in_specs=[spec, spec, spec, spec, pl.BlockSpec((3, HALF_W), lambda m: (0, 0))],
        out_specs=spec,
        out_shape=jax.ShapeDtypeStruct((m_rows, HALF_W), BF16),
        compiler_params=_params(),
        name="short_conv",
    )(u, bgate, cgate, gate, conv_w)


def _outproj_kernel(*refs, bm, emit_h):
    it = iter(refs)
    y1_ref, y2_ref, x_ref, w_ref, gt_ref = next(it), next(it), next(it), next(it), next(it)
    if emit_h:
        gn_ref, sh_ref, sc_ref = next(it), next(it), next(it)
    xo_ref = next(it)
    if emit_h:
        ho_ref = next(it)
    ssq = jnp.zeros((bm, 1), F32)
    chunks = [slice(c, c + PROJ_CHUNK) for c in range(0, D_MODEL, PROJ_CHUNK)]
    for cs in chunks:
        acc = _dot(y1_ref[...], w_ref[0:HALF_W, cs]) + _dot(y2_ref[...], w_ref[HALF_W:D_MODEL, cs])
        xn = x_ref[:, cs] + gt_ref[:, cs] * acc
        xo_ref[:, cs] = xn
        if emit_h:
            ssq = ssq + jnp.sum(xn * xn, axis=-1, keepdims=True)
    if emit_h:
        r = lax.rsqrt(ssq * (1.0 / D_MODEL) + EPS)
        for cs in chunks:
            ho_ref[:, cs] = _norm_mod(xo_ref[:, cs], r, gn_ref[:, cs], sc_ref[:, cs],
                                      sh_ref[:, cs]).astype(BF16)


def _outproj(y1, y2, x, w_out, norm_g3, mod4, layer, row_fn, emit_h):
    m_rows = x.shape[0]
    bm = PROJ_ROWS
    in_specs = [
        pl.BlockSpec((bm, HALF_W), lambda m: (m, 0)),
        pl.BlockSpec((bm, HALF_W), lambda m: (m, 0)),
        pl.BlockSpec((bm, D_MODEL), lambda m: (m, 0)),
        pl.BlockSpec((D_MODEL, D_MODEL), lambda m: (0, 0)),
        _mod_spec(layer, 2, row_fn),
    ]
    args = [y1, y2, x, w_out, mod4]
    out_specs = [pl.BlockSpec((bm, D_MODEL), lambda m: (m, 0))]
    out_shape = [jax.ShapeDtypeStruct((m_rows, D_MODEL), F32)]
    if emit_h:
        in_specs += [
            pl.BlockSpec((None, 1, D_MODEL), lambda m: (layer + 1, 0, 0)),
            _mod_spec(layer + 1, 0, row_fn),
            _mod_spec(layer + 1, 1, row_fn),
        ]
        args += [norm_g3, mod4, mod4]
        out_specs.append(pl.BlockSpec((bm, D_MODEL), lambda m: (m, 0)))
        out_shape.append(jax.ShapeDtypeStruct((m_rows, D_MODEL), BF16))
    outs = pl.pallas_call(
        functools.partial(_outproj_kernel, bm=bm, emit_h=emit_h),
        grid=(m_rows // bm,),
        in_specs=in_specs,
        out_specs=out_specs,
        out_shape=out_shape,
        compiler_params=_params(),
        name="outproj",
    )(*args)
    return (outs[0], outs[1]) if emit_h else (outs[0], None)


def _rope_tables(n_tokens):
    t = jnp.arange(n_tokens)
    row = t // GRID_W
    col = t % GRID_W
    n_freq = HEAD_DIM // 4
    inv = ROPE_THETA ** (-jnp.arange(n_freq, dtype=F32) / n_freq)
    ang_r = row[:, None] * inv
    ang_c = col[:, None] * inv
    cos = jnp.concatenate([jnp.cos(ang_r)] * 2 + [jnp.cos(ang_c)] * 2, axis=1)
    sin = jnp.concatenate([-jnp.sin(ang_r), jnp.sin(ang_r), -jnp.sin(ang_c), jnp.sin(ang_c)], axis=1)
    return cos, sin


def _tile_gain(g, n_heads, scale=1.0):
    return jnp.tile(g * scale, n_heads).reshape(1, n_heads * HEAD_DIM)


def kernel(x_prompt, x_sample, cache_a_k, cache_a_v, cache_c_k, cache_c_v, cache_d_k, cache_d_v,
           c, c_ctx, norm_g, ada_w, ada_b, ev_w_in, ev_w_out, a_q_norm, a_k_norm, a_sink, b_conv,
           od_w_in, od_w_out, c_q_norm, c_k_norm, c_lambda, c_out_norm, d_q_norm, d_k_norm):
    n_prompt, seq_p, _ = x_prompt.shape
    n_sample, seq_s, _ = x_sample.shape
    past = cache_a_k.shape[-2]
    scale = HEAD_DIM ** -0.5

    cond = jnp.concatenate(
        [c, c_ctx[None, :], jnp.zeros((COND_ROWS - n_sample - 1, D_MODEL), F32)], axis=0)
    mod4 = _modulation(cond, ada_w, ada_b).reshape(DEPTH, COND_ROWS, 1, 3 * D_MODEL)
    norm_g3 = norm_g.reshape(DEPTH, 1, D_MODEL)
    rope_tabs = _rope_tables(seq_s)

    prompt_row = lambda m: CTX_ROW
    sample_row = lambda m: (m * PROJ_ROWS) // seq_s

    xp = x_prompt.reshape(n_prompt * seq_p, D_MODEL)
    xs = x_sample.reshape(n_sample * seq_s, D_MODEL)
    hp = _prenorm(xp, norm_g3, mod4, 0, prompt_row)
    hs = _prenorm(xs, norm_g3, mod4, 0, sample_row)

    cache_c_k8 = cache_c_k.reshape(n_sample, DEPTH // 2, 2 * C_HEADS, past, HEAD_DIM)
    prompt_rows = PROMPT_SEQS_PER_STEP * seq_p

    a_k, a_v, c_k, c_v, d_k, d_v = [], [], [], [], [], []
    for layer in range(DEPTH):
        i = layer // 2
        last = layer == DEPTH - 1
        if layer % 2 == 0:
            w_in = ev_w_in[i].astype(BF16)
            w_out = ev_w_out[i].astype(BF16)
            slabs = [w_in[:, 0:2560], w_in[:, 2560:4608], w_in[:, 4608:6656]]
            gq = _tile_gain(a_q_norm[i], N_KV * GROUP, scale)
            gk = _tile_gain(a_k_norm[i], N_KV)
            kv_cache = (N_KV, HEAD_DIM)

            def project(h, seq_len, tabs, cache):
                segs1 = [_Seg("norm", HALF_W), _Seg("norm", N_KV * HEAD_DIM, cache),
                         _Seg("plain", N_KV * HEAD_DIM, cache), _Seg("silu", HALF_W)]
                (q, k, v, ga), caches = _inproj(h, slabs[0], segs1, [gq, gk, None, None], tabs, seq_len)
                (u, bg), _ = _inproj(h, slabs[1], [_Seg("plain", HALF_W)] * 2, [None] * 2, None, seq_len)
                (cg, gb), _ = _inproj(h, slabs[2], [_Seg("plain", HALF_W), _Seg("silu", HALF_W)],
                                      [None] * 2, None, seq_len)
                return q, k, v, ga, u, bg, cg, gb, caches

            q, k, v, ga, u, bg, cg, gb, caches = project(hp, seq_p, None, kv_cache)
            a_k.append(caches[0])
            a_v.append(caches[1])
            y1p = _gqa(q, k, v, ga, seq_p, seq_p, prompt_rows, sink=a_sink[i])
            y2p = _short_conv(u, bg, cg, gb, b_conv[i], seq_p)

            q, k, v, ga, u, bg, cg, gb, _ = project(hs, seq_s, rope_tabs, None)
            y1s = _banded(q, k, v, ga, a_sink[i], cache_a_k, cache_a_v, i, seq_s)
            y2s = _short_conv(u, bg, cg, gb, b_conv[i], seq_s)
        else:
            lam_init = 0.8 - 0.6 * math.exp(-0.3 * layer)
            w_in = od_w_in[i].astype(BF16)
            w_out = od_w_out[i].astype(BF16)
            slabs = [w_in[:, 0:2048], w_in[:, 2048:4096], w_in[:, 4096:6656]]
            gcq = _tile_gain(c_q_norm[i], 2 * C_HEADS, scale)
            gck = _tile_gain(c_k_norm[i], 2 * C_HEADS)
            gdq = _tile_gain(d_q_norm[i], N_KV * GROUP, scale)
            gdk = _tile_gain(d_k_norm[i], N_KV)
            c_on = c_out_norm[i].reshape(1, C_V_DIM)

            def project(h, seq_len, tabs, with_cache):
                ck_cache = (2 * C_HEADS, HEAD_DIM) if with_cache else None
                cv_cache = (C_HEADS, C_V_DIM) if with_cache else None
                kv_cache = (N_KV, HEAD_DIM) if with_cache else None
                (cq, ck), cc1 = _inproj(h, slabs[0], [_Seg("norm", HALF_W), _Seg("norm", HALF_W, ck_cache)],
                                        [gcq, gck], tabs, seq_len)
                (cv, cgate), cc2 = _inproj(h, slabs[1], [_Seg("plain", HALF_W, cv_cache), _Seg("silu", HALF_W)],
                                           [None] * 2, None, seq_len)
                segs3 = [_Seg("norm", HALF_W), _Seg("norm", N_KV * HEAD_DIM, kv_cache),
                         _Seg("plain", N_KV * HEAD_DIM, kv_cache), _Seg("silu", HALF_W)]
                (dq, dk, dv, dgate), cc3 = _inproj(h, slabs[2], segs3, [gdq, gdk, None, None], tabs, seq_len)
                return cq, ck, cv, cgate, dq, dk, dv, dgate, cc1 + cc2 + cc3

            cq, ck, cv, cgate, dq, dk, dv, dgate, caches = project(hp, seq_p, None, True)
            c_k.append(caches[0].reshape(n_prompt, 2, C_HEADS, seq_p, HEAD_DIM))
            c_v.append(caches[1])
            d_k.append(caches[2])
            d_v.append(caches[3])
            y1p = _diff(cq, ck, cv, cgate, c_lambda[i], c_on, lam_init, seq_p, seq_p, prompt_rows)
            y2p = _gqa(dq, dk, dv, dgate, seq_p, seq_p, prompt_rows)

            cq, ck, cv, cgate, dq, dk, dv, dgate, _ = project(hs, seq_s, rope_tabs, False)
            y1s = _diff(cq, ck, cv, cgate, c_lambda[i], c_on, lam_init, seq_s, 2 * SAMPLE_QB, seq_s,
                        ctx=(cache_c_k8, cache_c_v), layer_idx=i)
            y2s = _gqa(dq, dk, dv, dgate, seq_s, SAMPLE_QB, seq_s,
                       ctx=(cache_d_k, cache_d_v), layer_idx=i)

        xp, hp = _outproj(y1p, y2p, xp, w_out, norm_g3, mod4, layer, prompt_row, not last)
        xs, hs = _outproj(y1s, y2s, xs, w_out, norm_g3, mod4, layer, sample_row, not last)

    return (xp.reshape(n_prompt, seq_p, D_MODEL), xs.reshape(n_sample, seq_s, D_MODEL),
            jnp.stack(a_k, axis=1), jnp.stack(a_v, axis=1), jnp.stack(c_k, axis=1),
            jnp.stack(c_v, axis=1), jnp.stack(d_k, axis=1), jnp.stack(d_v, axis=1))
```

```python
import functools
import math

import jax
import jax.numpy as jnp
from jax import lax
from jax.experimental import pallas as pl
from jax.experimental.pallas import tpu as pltpu

D_MODEL = 2048
DEPTH = 4
GRID_W = 64
HEAD_DIM = 128
WINDOW = 128
ROPE_THETA = 10000.0
EPS = 1e-6
NEG_INF = -1e30
HALF_W = D_MODEL // 2
N_KV = 2
GROUP = 4
C_HEADS = 4
C_V_DIM = 2 * HEAD_DIM
CACHE_SLOTS = DEPTH // 2
COND_ROWS = 16
CTX_ROW = 8

BF16 = jnp.bfloat16
F32 = jnp.float32

VMEM_LIMIT_BYTES = 52 * 1024 * 1024
PROJ_ROWS = 512
PROJ_CHUNK = 512
SAMPLE_QB = 128
PROMPT_SEQS_PER_STEP = 4


def _params(n_axes=1):
    return pltpu.CompilerParams(
        dimension_semantics=("arbitrary",) * n_axes,
        vmem_limit_bytes=VMEM_LIMIT_BYTES,
    )


def _dot(a, b):
    return jnp.dot(a, b, preferred_element_type=F32)


def _dot_nt(a, b):
    return lax.dot_general(a, b, (((1,), (1,)), ((), ())), preferred_element_type=F32)


def _silu(x):
    return x / (1.0 + jnp.exp(-x))


def _mod_kernel(cond_ref, w_ref, b_ref, o_ref):
    s = _silu(cond_ref[...]).astype(BF16)
    o_ref[...] = _dot(s, w_ref[...].astype(BF16)) + b_ref[...]


def _modulation(cond, ada_w, ada_b):
    bn = 1024
    n_out = 3 * D_MODEL
    return pl.pallas_call(
        _mod_kernel,
        grid=(DEPTH, n_out // bn),
        in_specs=[
            pl.BlockSpec((COND_ROWS, D_MODEL), lambda l, n: (0, 0)),
            pl.BlockSpec((None, D_MODEL, bn), lambda l, n: (l, 0, n)),
            pl.BlockSpec((None, 1, bn), lambda l, n: (l, 0, n)),
        ],
        out_specs=pl.BlockSpec((None, COND_ROWS, bn), lambda l, n: (l, 0, n)),
        out_shape=jax.ShapeDtypeStruct((DEPTH, COND_ROWS, n_out), F32),
        compiler_params=_params(2),
        name="modulation",
    )(cond, ada_w, ada_b.reshape(DEPTH, 1, n_out))


def _mod_spec(layer, part, row_fn):
    return pl.BlockSpec((None, None, 1, D_MODEL), lambda m: (layer, row_fn(m), 0, part))


def _norm_mod(x, r, g, sc, sh):
    return ((x * r) * g) * (1.0 + sc) + sh


def _prenorm_kernel(x_ref, g_ref, sh_ref, sc_ref, h_ref):
    x = x_ref[...]
    r = lax.rsqrt(jnp.mean(x * x, axis=-1, keepdims=True) + EPS)
    h_ref[...] = _norm_mod(x, r, g_ref[...], sc_ref[...], sh_ref[...]).astype(BF16)


def _prenorm(x, norm_g3, mod4, layer, row_fn):
    m_rows = x.shape[0]
    bm = PROJ_ROWS
    return pl.pallas_call(
        _prenorm_kernel,
        grid=(m_rows // bm,),
        in_specs=[
            pl.BlockSpec((bm, D_MODEL), lambda m: (m, 0)),
            pl.BlockSpec((None, 1, D_MODEL), lambda m: (layer, 0, 0)),
            _mod_spec(layer, 0, row_fn),
            _mod_spec(layer, 1, row_fn),
        ],
        out_specs=pl.BlockSpec((bm, D_MODEL), lambda m: (m, 0)),
        out_shape=jax.ShapeDtypeStruct((m_rows, D_MODEL), BF16),
        compiler_params=_params(),
        name="prenorm",
    )(x, norm_g3, mod4, mod4)


class _Seg:
    def __init__(self, kind, width, cache=None):
        self.kind = kind
        self.width = width
        self.cache = cache


def _rope(y, cos, sin_signed, first_half):
    swapped = jnp.where(first_half, pltpu.roll(y, 96, axis=1), pltpu.roll(y, 32, axis=1))
    return y * cos + swapped * sin_signed


def _inproj_kernel(*refs, segs, rope, seq_len, bm, n_prev):
    it = iter(refs)
    h_ref = next(it)
    w_ref = next(it)
    if rope:
        cos_ref = next(it)
        sin_ref = next(it)
    gain_refs = [next(it) if s.kind == "norm" else None for s in segs]
    for _ in range(n_prev):
        next(it)
    out_refs = [next(it) for _ in segs]
    cache_refs = [next(it) if s.cache else None for s in segs]

    if rope:
        cos = cos_ref[...]
        sin_signed = sin_ref[...]
        lane = lax.broadcasted_iota(jnp.int32, (bm, HEAD_DIM), 1)
        first_half = (lane % 64) < 32

    col = 0
    for seg, gain_ref, out_ref, cache_ref in zip(segs, gain_refs, out_refs, cache_refs):
        for c0 in range(0, seg.width, PROJ_CHUNK):
            cw = min(PROJ_CHUNK, seg.width - c0)
            acc = _dot(h_ref[...], w_ref[:, col + c0:col + c0 + cw])
            if seg.kind == "norm":
                heads = []
                for hh in range(cw // HEAD_DIM):
                    a = acc[:, hh * HEAD_DIM:(hh + 1) * HEAD_DIM]
                    r = lax.rsqrt(jnp.mean(a * a, axis=-1, keepdims=True) + EPS)
                    y = (a * r) * gain_ref[:, c0 + hh * HEAD_DIM:c0 + (hh + 1) * HEAD_DIM]
                    if rope:
                        y = _rope(y, cos, sin_signed, first_half)
                    heads.append(y)
                val = jnp.concatenate(heads, axis=1) if len(heads) > 1 else heads[0]
            elif seg.kind == "silu":
                val = _silu(acc)
            else:
                val = acc
            out_ref[:, c0:c0 + cw] = val.astype(BF16)
            if seg.cache:
                _, hw = seg.cache
                for j in range(bm // seq_len):
                    for hh in range(cw // hw):
                        cache_ref[j, (c0 // hw) + hh] = val[j * seq_len:(j + 1) * seq_len,
                                                            hh * hw:(hh + 1) * hw]
        col += seg.width


def _inproj(h, w_slab, segs, gains, rope_tabs, seq_len, cache_slot=0, cache_prev=None):
    m_rows = h.shape[0]
    bm = PROJ_ROWS
    n_cols = w_slab.shape[1]
    rope = rope_tabs is not None
    in_specs = [
        pl.BlockSpec((bm, D_MODEL), lambda m: (m, 0)),
        pl.BlockSpec((D_MODEL, n_cols), lambda m: (0, 0)),
    ]
    args = [h, w_slab]
    if rope:
        blocks_per_seq = seq_len // bm
        tab_spec = pl.BlockSpec((bm, HEAD_DIM), lambda m: (m % blocks_per_seq, 0))
        in_specs += [tab_spec, tab_spec]
        args += list(rope_tabs)
    for seg, g in zip(segs, gains):
        if seg.kind == "norm":
            in_specs.append(pl.BlockSpec((1, seg.width), lambda m: (0, 0)))
            args.append(g)
    n_prev = 0 if cache_prev is None else len(cache_prev)
    aliases = {}
    for k in range(n_prev):
        aliases[len(args)] = len(segs) + k
        in_specs.append(pl.BlockSpec(memory_space=pl.ANY))
        args.append(cache_prev[k])
    out_specs = [pl.BlockSpec((bm, s.width), lambda m: (m, 0)) for s in segs]
    out_shape = [jax.ShapeDtypeStruct((m_rows, s.width), BF16) for s in segs]
    for s in segs:
        if s.cache:
            nh, hw = s.cache
            out_specs.append(pl.BlockSpec((bm // seq_len, None, nh, seq_len, hw),
                                          lambda m: (m, cache_slot, 0, 0, 0)))
            out_shape.append(jax.ShapeDtypeStruct((m_rows // seq_len, CACHE_SLOTS, nh, seq_len, hw), F32))
    outs = pl.pallas_call(
        functools.partial(_inproj_kernel, segs=segs, rope=rope, seq_len=seq_len, bm=bm, n_prev=n_prev),
        grid=(m_rows // bm,),
        in_specs=in_specs,
        out_specs=out_specs,
        out_shape=out_shape,
        input_output_aliases=aliases,
        compiler_params=_params(),
        name="inproj",
    )(*args)
    n = len(segs)
    return list(outs[:n]), list(outs[n:])


def _softmax_pv(scores, values, sinks=None):
    groups = 1 if sinks is None else len(sinks)
    rows = scores[0].shape[0] // groups
    probs = [[] for _ in scores]
    inv = []
    for g in range(groups):
        rs = slice(g * rows, (g + 1) * rows)
        m = jnp.max(scores[0][rs], axis=-1, keepdims=True)
        for s in scores[1:]:
            m = jnp.maximum(m, jnp.max(s[rs], axis=-1, keepdims=True))
        if sinks is None:
            denom = None
        else:
            m = jnp.maximum(m, sinks[g])
            denom = jnp.exp(sinks[g] - m)
        for i, s in enumerate(scores):
            e = jnp.exp(s[rs] - m)
            part = jnp.sum(e, axis=-1, keepdims=True)
            denom = part if denom is None else denom + part
            probs[i].append(e.astype(BF16))
        inv.append(1.0 / denom)
    out = None
    for p, v in zip(probs, values):
        pv = _dot(p[0] if groups == 1 else jnp.concatenate(p, axis=0), v)
        out = pv if out is None else out + pv
    return [out[g * rows:(g + 1) * rows] * inv[g] for g in range(groups)]


def _stack_heads(ref, r0, rows, first_head):
    return jnp.concatenate(
        [ref[pl.ds(r0, rows), (first_head + g) * HEAD_DIM:(first_head + g + 1) * HEAD_DIM]
         for g in range(GROUP)], axis=0)


def _head_sinks(sink_ref, first_head):
    return [sink_ref[first_head + g] for g in range(GROUP)]


def _store_gated(o_ref, g_ref, o, r0, rows, first_head):
    if len(o) == 1:
        o = [o[0][g * rows:(g + 1) * rows] for g in range(GROUP)]
    for g in range(GROUP):
        cs = slice((first_head + g) * HEAD_DIM, (first_head + g + 1) * HEAD_DIM)
        gate = g_ref[pl.ds(r0, rows), cs].astype(F32)
        o_ref[pl.ds(r0, rows), cs] = (o[g] * gate).astype(BF16)


def _gqa_kernel(*refs, seq_len, qb, n_blocks, ctx, sink):
    it = iter(refs)
    sink_ref = next(it) if sink else None
    q_ref, k_ref, v_ref, g_ref = next(it), next(it), next(it), next(it)
    if ctx:
        ck_ref, cv_ref = next(it), next(it)
    o_ref = next(it)
    if ctx:
        kc_s, vc_s = next(it), next(it)
        for kv in range(N_KV):
            kc_s[kv] = ck_ref[kv].astype(BF16)
            vc_s[kv] = cv_ref[kv].astype(BF16)
    blocks_per_seq = seq_len // qb

    def body(n, carry):
        r0 = pl.multiple_of(n * qb, qb)
        k0 = pl.multiple_of((n // blocks_per_seq) * seq_len, seq_len)
        for kv in range(N_KV):
            hs = slice(kv * HEAD_DIM, (kv + 1) * HEAD_DIM)
            q = _stack_heads(q_ref, r0, qb, kv * GROUP)
            k = k_ref[pl.ds(k0, seq_len), hs]
            v = v_ref[pl.ds(k0, seq_len), hs]
            scores, values = [], []
            if ctx:
                scores.append(_dot_nt(q, kc_s[kv]))
                values.append(vc_s[kv])
            scores.append(_dot_nt(q, k))
            values.append(v)
            sinks = _head_sinks(sink_ref, kv * GROUP) if sink else None
            o = _softmax_pv(scores, values, sinks)
            _store_gated(o_ref, g_ref, o, r0, qb, kv * GROUP)
        return carry

    lax.fori_loop(0, n_blocks, body, 0)


def _gqa(q, k, v, gate, seq_len, qb, rows_per_step, sink=None, ctx=None, layer_idx=0):
    m_rows = q.shape[0]
    kvw = N_KV * HEAD_DIM
    in_specs, args = [], []
    if sink is not None:
        in_specs.append(pl.BlockSpec(memory_space=pltpu.SMEM))
        args.append(sink)
    in_specs += [
        pl.BlockSpec((rows_per_step, HALF_W), lambda b: (b, 0)),
        pl.BlockSpec((rows_per_step, kvw), lambda b: (b, 0)),
        pl.BlockSpec((rows_per_step, kvw), lambda b: (b, 0)),
        pl.BlockSpec((rows_per_step, HALF_W), lambda b: (b, 0)),
    ]
    args += [q, k, v, gate]
    scratch = []
    if ctx is not None:
        past = ctx[0].shape[-2]
        cspec = pl.BlockSpec((None, None, N_KV, past, HEAD_DIM), lambda b: (b, layer_idx, 0, 0, 0))
        in_specs += [cspec, cspec]
        args += list(ctx)
        scratch = [pltpu.VMEM((N_KV, past, HEAD_DIM), BF16)] * 2
    return pl.pallas_call(
        functools.partial(_gqa_kernel, seq_len=seq_len, qb=qb, n_blocks=rows_per_step // qb,
                          ctx=ctx is not None, sink=sink is not None),
        grid=(m_rows // rows_per_step,),
        in_specs=in_specs,
        out_specs=pl.BlockSpec((rows_per_step, HALF_W), lambda b: (b, 0)),
        out_shape=jax.ShapeDtypeStruct((m_rows, HALF_W), BF16),
        scratch_shapes=scratch,
        compiler_params=_params(),
        name="gqa",
    )(*args)


def _banded_kernel(sink_ref, q_ref, k_ref, v_ref, g_ref, ck_ref, cv_ref, o_ref,
                   kpad, vpad, kc_s, vc_s, *, seq_len):
    n_blocks = seq_len // WINDOW
    zeros = jnp.zeros((WINDOW, HEAD_DIM), BF16)
    for kv in range(N_KV):
        hs = slice(kv * HEAD_DIM, (kv + 1) * HEAD_DIM)
        kpad[kv, 0:WINDOW] = zeros
        vpad[kv, 0:WINDOW] = zeros
        kpad[kv, WINDOW + seq_len:2 * WINDOW + seq_len] = zeros
        vpad[kv, WINDOW + seq_len:2 * WINDOW + seq_len] = zeros
        kpad[kv, WINDOW:WINDOW + seq_len] = k_ref[:, hs]
        vpad[kv, WINDOW:WINDOW + seq_len] = v_ref[:, hs]
        kc_s[kv] = ck_ref[kv].astype(BF16)
        vc_s[kv] = cv_ref[kv].astype(BF16)

    rows = GROUP * WINDOW
    qi = lax.broadcasted_iota(jnp.int32, (rows, 3 * WINDOW), 0) % WINDOW
    col = lax.broadcasted_iota(jnp.int32, (rows, 3 * WINDOW), 1)
    kj = col % WINDOW
    blk = col // WINDOW
    never = 4 * WINDOW

    def body(n, carry):
        r0 = pl.multiple_of(n * WINDOW, WINDOW)
        lo = jnp.where(n > 0, 0, never)
        hi = jnp.where(n < n_blocks - 1, 0, never)
        slack = jnp.where(blk == 0, kj - qi - lo, jnp.where(blk == 2, qi - kj - hi, 0))
        valid = slack >= 0
        for kv in range(N_KV):
            q = _stack_heads(q_ref, r0, WINDOW, kv * GROUP)
            kb = kpad[kv, pl.ds(r0, 3 * WINDOW), :]
            vb = vpad[kv, pl.ds(r0, 3 * WINDOW), :]
            s_ctx = _dot_nt(q, kc_s[kv])
            s_band = jnp.where(valid, _dot_nt(q, kb), NEG_INF)
            o = _softmax_pv([s_ctx, s_band], [vc_s[kv], vb], _head_sinks(sink_ref, kv * GROUP))
            _store_gated(o_ref, g_ref, o, r0, WINDOW, kv * GROUP)
        return carry

    lax.fori_loop(0, n_blocks, body, 0)


def _banded(q, k, v, gate, sink, cache_k, cache_v, layer_idx, seq_len):
    m_rows = q.shape[0]
    kvw = N_KV * HEAD_DIM
    past = cache_k.shape[-2]
    cspec = pl.BlockSpec((None, None, N_KV, past, HEAD_DIM), lambda b: (b, layer_idx, 0, 0, 0))
    pad_rows = seq_len + 2 * WINDOW
    return pl.pallas_call(
        functools.partial(_banded_kernel, seq_len=seq_len),
        grid=(m_rows // seq_len,),
        in_specs=[
            pl.BlockSpec(memory_space=pltpu.SMEM),
            pl.BlockSpec((seq_len, HALF_W), lambda b: (b, 0)),
            pl.BlockSpec((seq_len, kvw), lambda b: (b, 0)),
            pl.BlockSpec((seq_len, kvw), lambda b: (b, 0)),
            pl.BlockSpec((seq_len, HALF_W), lambda b: (b, 0)),
            cspec, cspec,
        ],
        out_specs=pl.BlockSpec((seq_len, HALF_W), lambda b: (b, 0)),
        out_shape=jax.ShapeDtypeStruct((m_rows, HALF_W), BF16),
        scratch_shapes=[
            pltpu.VMEM((N_KV, pad_rows, HEAD_DIM), BF16),
            pltpu.VMEM((N_KV, pad_rows, HEAD_DIM), BF16),
            pltpu.VMEM((N_KV, past, HEAD_DIM), BF16),
            pltpu.VMEM((N_KV, past, HEAD_DIM), BF16),
        ],
        compiler_params=_params(),
        name="banded",
    )(sink, q, k, v, gate, cache_k, cache_v)


def _diff_kernel(*refs, seq_len, qb, n_blocks, ctx, lam_init):
    it = iter(refs)
    lam_ref, on_ref = next(it), next(it)
    q_ref, k_ref, v_ref, g_ref = next(it), next(it), next(it), next(it)
    if ctx:
        ck_ref, cv_ref = next(it), next(it)
    o_ref = next(it)
    if ctx:
        kc_s, vc_s = next(it), next(it)
        for slot in range(2 * C_HEADS):
            kc_s[slot] = ck_ref[slot].astype(BF16)
        for hd in range(C_HEADS):
            vc_s[hd] = cv_ref[hd].astype(BF16)
    lf = lam_ref[...]
    lam = (jnp.exp(jnp.sum(lf[0:1] * lf[1:2], axis=-1, keepdims=True))
           - jnp.exp(jnp.sum(lf[2:3] * lf[3:4], axis=-1, keepdims=True)) + lam_init)
    out_gain = on_ref[...]
    blocks_per_seq = seq_len // qb

    def body(n, carry):
        r0 = pl.multiple_of(n * qb, qb)
        k0 = pl.multiple_of((n // blocks_per_seq) * seq_len, seq_len)
        for hd in range(C_HEADS):
            vs = slice(hd * C_V_DIM, (hd + 1) * C_V_DIM)
            v = v_ref[pl.ds(k0, seq_len), vs]
            maps = []
            for mp in range(2):
                slot = mp * C_HEADS + hd
                hs = slice(slot * HEAD_DIM, (slot + 1) * HEAD_DIM)
                q = q_ref[pl.ds(r0, qb), hs]
                k = k_ref[pl.ds(k0, seq_len), hs]
                scores, values = [], []
                if ctx:
                    scores.append(_dot_nt(q, kc_s[slot]))
                    values.append(vc_s[hd])
                scores.append(_dot_nt(q, k))
                values.append(v)
                maps.append(_softmax_pv(scores, values)[0])
            d = maps[0] - lam * maps[1]
            r = lax.rsqrt(jnp.mean(d * d, axis=-1, keepdims=True) + EPS)
            y = ((d * r) * out_gain) * (1.0 - lam_init)
            gate = g_ref[pl.ds(r0, qb), vs].astype(F32)
            o_ref[pl.ds(r0, qb), vs] = (y * gate).astype(BF16)
        return carry

    lax.fori_loop(0, n_blocks, body, 0)


def _diff(q, k, v, gate, c_lam, c_on, lam_init, seq_len, qb, rows_per_step, ctx=None, layer_idx=0):
    m_rows = q.shape[0]
    in_specs = [
        pl.BlockSpec((4, HEAD_DIM), lambda b: (0, 0)),
        pl.BlockSpec((1, C_V_DIM), lambda b: (0, 0)),
    ] + [pl.BlockSpec((rows_per_step, HALF_W), lambda b: (b, 0))] * 4
    args = [c_lam, c_on, q, k, v, gate]
    scratch = []
    if ctx is not None:
        past = ctx[0].shape[-2]
        in_specs += [
            pl.BlockSpec((None, None, 2 * C_HEADS, past, HEAD_DIM), lambda b: (b, layer_idx, 0, 0, 0)),
            pl.BlockSpec((None, None, C_HEADS, past, C_V_DIM), lambda b: (b, layer_idx, 0, 0, 0)),
        ]
        args += list(ctx)
        scratch = [pltpu.VMEM((2 * C_HEADS, past, HEAD_DIM), BF16),
                   pltpu.VMEM((C_HEADS, past, C_V_DIM), BF16)]
    return pl.pallas_call(
        functools.partial(_diff_kernel, seq_len=seq_len, qb=qb, n_blocks=rows_per_step // qb,
                          ctx=ctx is not None, lam_init=lam_init),
        grid=(m_rows // rows_per_step,),
        in_specs=in_specs,
        out_specs=pl.BlockSpec((rows_per_step, HALF_W), lambda b: (b, 0)),
        out_shape=jax.ShapeDtypeStruct((m_rows, HALF_W), BF16),
        scratch_shapes=scratch,
        compiler_params=_params(),
        name="diff_attn",
    )(*args)


def _conv_kernel(u_ref, b_ref, c_ref, g_ref, w_ref, o_ref, *, seq_len, rows):
    z = c_ref[...].astype(F32) * u_ref[...].astype(F32)
    t = lax.broadcasted_iota(jnp.int32, (rows, 1), 0) % seq_len
    z_prev = jnp.where(t == 0, 0.0, pltpu.roll(z, 1, axis=0))
    z_next = jnp.where(t == seq_len - 1, 0.0, pltpu.roll(z, rows - 1, axis=0))
    w = w_ref[...]
    conv = z_prev * w[0:1] + z * w[1:2] + z_next * w[2:3]
    o_ref[...] = ((b_ref[...].astype(F32) * conv) * g_ref[...].astype(F32)).astype(BF16)


def _short_conv(u, bgate, cgate, gate, conv_w, seq_len):
    m_rows = u.shape[0]
    rows = max(seq_len, 512)
    spec = pl.BlockSpec((rows, HALF_W), lambda m: (m, 0))
    return pl.pallas_call(
        functools.partial(_conv_kernel, seq_len=seq_len, rows=rows),
        grid=(m_rows // rows,),
        in_specs=[spec, spec, spec, spec, pl.BlockSpec((3, HALF_W), lambda m: (0, 0))],
        out_specs=spec,
        out_shape=jax.ShapeDtypeStruct((m_rows, HALF_W), BF16),
        compiler_params=_params(),
        name="short_conv",
    )(u, bgate, cgate, gate, conv_w)


def _outproj_kernel(*refs, bm, emit_h):
    it = iter(refs)
    y1_ref, y2_ref, x_ref, w_ref, gt_ref = next(it), next(it), next(it), next(it), next(it)
    if emit_h:
        gn_ref, sh_ref, sc_ref = next(it), next(it), next(it)
    xo_ref = next(it)
    if emit_h:
        ho_ref = next(it)
    ssq = jnp.zeros((bm, 1), F32)
    chunks = [slice(c, c + PROJ_CHUNK) for c in range(0, D_MODEL, PROJ_CHUNK)]
    for cs in chunks:
        acc = _dot(y1_ref[...], w_ref[0:HALF_W, cs]) + _dot(y2_ref[...], w_ref[HALF_W:D_MODEL, cs])
        xn = x_ref[:, cs] + gt_ref[:, cs] * acc
        xo_ref[:, cs] = xn
        if emit_h:
            ssq = ssq + jnp.sum(xn * xn, axis=-1, keepdims=True)
    if emit_h:
        r = lax.rsqrt(ssq * (1.0 / D_MODEL) + EPS)
        for cs in chunks:
            ho_ref[:, cs] = _norm_mod(xo_ref[:, cs], r, gn_ref[:, cs], sc_ref[:, cs],
                                      sh_ref[:, cs]).astype(BF16)


def _outproj(y1, y2, x, w_out, norm_g3, mod4, layer, row_fn, emit_h):
    m_rows = x.shape[0]
    bm = PROJ_ROWS
    in_specs = [
        pl.BlockSpec((bm, HALF_W), lambda m: (m, 0)),
        pl.BlockSpec((bm, HALF_W), lambda m: (m, 0)),
        pl.BlockSpec((bm, D_MODEL), lambda m: (m, 0)),
        pl.BlockSpec((D_MODEL, D_MODEL), lambda m: (0, 0)),
        _mod_spec(layer, 2, row_fn),
    ]
    args = [y1, y2, x, w_out, mod4]
    out_specs = [pl.BlockSpec((bm, D_MODEL), lambda m: (m, 0))]
    out_shape = [jax.ShapeDtypeStruct((m_rows, D_MODEL), F32)]
    if emit_h:
        in_specs += [
            pl.BlockSpec((None, 1, D_MODEL), lambda m: (layer + 1, 0, 0)),
            _mod_spec(layer + 1, 0, row_fn),
            _mod_spec(layer + 1, 1, row_fn),
        ]
        args += [norm_g3, mod4, mod4]
        out_specs.append(pl.BlockSpec((bm, D_MODEL), lambda m: (m, 0)))
        out_shape.append(jax.ShapeDtypeStruct((m_rows, D_MODEL), BF16))
    outs = pl.pallas_call(
        functools.partial(_outproj_kernel, bm=bm, emit_h=emit_h),
        grid=(m_rows // bm,),
        in_specs=in_specs,
        out_specs=out_specs,
        out_shape=out_shape,
        compiler_params=_params(),
        name="outproj",
    )(*args)
    return (outs[0], outs[1]) if emit_h else (outs[0], None)


def _rope_tables(n_tokens):
    t = jnp.arange(n_tokens)
    row = t // GRID_W
    col = t % GRID_W
    n_freq = HEAD_DIM // 4
    inv = ROPE_THETA ** (-jnp.arange(n_freq, dtype=F32) / n_freq)
    ang_r = row[:, None] * inv
    ang_c = col[:, None] * inv
    cos = jnp.concatenate([jnp.cos(ang_r)] * 2 + [jnp.cos(ang_c)] * 2, axis=1)
    sin = jnp.concatenate([-jnp.sin(ang_r), jnp.sin(ang_r), -jnp.sin(ang_c), jnp.sin(ang_c)], axis=1)
    return cos, sin


def _tile_gain(g, n_heads, scale=1.0):
    return jnp.tile(g * scale, n_heads).reshape(1, n_heads * HEAD_DIM)


def kernel(x_prompt, x_sample, cache_a_k, cache_a_v, cache_c_k, cache_c_v, cache_d_k, cache_d_v,
           c, c_ctx, norm_g, ada_w, ada_b, ev_w_in, ev_w_out, a_q_norm, a_k_norm, a_sink, b_conv,
           od_w_in, od_w_out, c_q_norm, c_k_norm, c_lambda, c_out_norm, d_q_norm, d_k_norm):
    n_prompt, seq_p, _ = x_prompt.shape
    n_sample, seq_s, _ = x_sample.shape
    past = cache_a_k.shape[-2]
    scale = HEAD_DIM ** -0.5

    cond = jnp.concatenate(
        [c, c_ctx[None, :], jnp.zeros((COND_ROWS - n_sample - 1, D_MODEL), F32)], axis=0)
    mod4 = _modulation(cond, ada_w, ada_b).reshape(DEPTH, COND_ROWS, 1, 3 * D_MODEL)
    norm_g3 = norm_g.reshape(DEPTH, 1, D_MODEL)
    rope_tabs = _rope_tables(seq_s)

    prompt_row = lambda m: CTX_ROW
    sample_row = lambda m: (m * PROJ_ROWS) // seq_s

    xp = x_prompt.reshape(n_prompt * seq_p, D_MODEL)
    xs = x_sample.reshape(n_sample * seq_s, D_MODEL)
    hp = _prenorm(xp, norm_g3, mod4, 0, prompt_row)
    hs = _prenorm(xs, norm_g3, mod4, 0, sample_row)

    cache_c_k8 = cache_c_k.reshape(n_sample, DEPTH // 2, 2 * C_HEADS, past, HEAD_DIM)
    prompt_rows = PROMPT_SEQS_PER_STEP * seq_p

    new_a = None
    new_c = (None, None, None)
    for layer in range(DEPTH):
        i = layer // 2
        last = layer == DEPTH - 1
        if layer % 2 == 0:
            w_out = ev_w_out[i].astype(BF16)
            slabs = [ev_w_in[i][:, a:b].astype(BF16) for a, b in ((0, 2560), (2560, 4608), (4608, 6656))]
            gq = _tile_gain(a_q_norm[i], N_KV * GROUP, scale)
            gk = _tile_gain(a_k_norm[i], N_KV)
            kv_cache = (N_KV, HEAD_DIM)

            def project(h, seq_len, tabs, cache, prev=None):
                segs1 = [_Seg("norm", HALF_W), _Seg("norm", N_KV * HEAD_DIM, cache),
                         _Seg("plain", N_KV * HEAD_DIM, cache), _Seg("silu", HALF_W)]
                (q, k, v, ga), caches = _inproj(h, slabs[0], segs1, [gq, gk, None, None], tabs, seq_len,
                                                cache_slot=i, cache_prev=prev)
                (u, bg), _ = _inproj(h, slabs[1], [_Seg("plain", HALF_W)] * 2, [None] * 2, None, seq_len)
                (cg, gb), _ = _inproj(h, slabs[2], [_Seg("plain", HALF_W), _Seg("silu", HALF_W)],
                                      [None] * 2, None, seq_len)
                return q, k, v, ga, u, bg, cg, gb, caches

            q, k, v, ga, u, bg, cg, gb, new_a = project(hp, seq_p, None, kv_cache, new_a)
            y1p = _gqa(q, k, v, ga, seq_p, seq_p, prompt_rows, sink=a_sink[i])
            y2p = _short_conv(u, bg, cg, gb, b_conv[i], seq_p)

            q, k, v, ga, u, bg, cg, gb, _ = project(hs, seq_s, rope_tabs, None)
            y1s = _banded(q, k, v, ga, a_sink[i], cache_a_k, cache_a_v, i, seq_s)
            y2s = _short_conv(u, bg, cg, gb, b_conv[i], seq_s)
        else:
            lam_init = 0.8 - 0.6 * math.exp(-0.3 * layer)
            w_out = od_w_out[i].astype(BF16)
            slabs = [od_w_in[i][:, a:b].astype(BF16) for a, b in ((0, 2048), (2048, 4096), (4096, 6656))]
            gcq = _tile_gain(c_q_norm[i], 2 * C_HEADS, scale)
            gck = _tile_gain(c_k_norm[i], 2 * C_HEADS)
            gdq = _tile_gain(d_q_norm[i], N_KV * GROUP, scale)
            gdk = _tile_gain(d_k_norm[i], N_KV)
            c_on = c_out_norm[i].reshape(1, C_V_DIM)

            def project(h, seq_len, tabs, with_cache, prev=(None, None, None)):
                ck_cache = (2 * C_HEADS, HEAD_DIM) if with_cache else None
                cv_cache = (C_HEADS, C_V_DIM) if with_cache else None
                kv_cache = (N_KV, HEAD_DIM) if with_cache else None
                (cq, ck), cc1 = _inproj(h, slabs[0], [_Seg("norm", HALF_W), _Seg("norm", HALF_W, ck_cache)],
                                        [gcq, gck], tabs, seq_len, cache_slot=i, cache_prev=prev[0])
                (cv, cgate), cc2 = _inproj(h, slabs[1], [_Seg("plain", HALF_W, cv_cache), _Seg("silu", HALF_W)],
                                           [None] * 2, None, seq_len, cache_slot=i, cache_prev=prev[1])
                segs3 = [_Seg("norm", HALF_W), _Seg("norm", N_KV * HEAD_DIM, kv_cache),
                         _Seg("plain", N_KV * HEAD_DIM, kv_cache), _Seg("silu", HALF_W)]
                (dq, dk, dv, dgate), cc3 = _inproj(h, slabs[2], segs3, [gdq, gdk, None, None], tabs, seq_len,
                                                   cache_slot=i, cache_prev=prev[2])
                return cq, ck, cv, cgate, dq, dk, dv, dgate, (cc1, cc2, cc3)

            cq, ck, cv, cgate, dq, dk, dv, dgate, new_c = project(hp, seq_p, None, True, new_c)
            y1p = _diff(cq, ck, cv, cgate, c_lambda[i], c_on, lam_init, seq_p, seq_p, prompt_rows)
            y2p = _gqa(dq, dk, dv, dgate, seq_p, seq_p, prompt_rows)

            cq, ck, cv, cgate, dq, dk, dv, dgate, _ = project(hs, seq_s, rope_tabs, False)
            y1s = _diff(cq, ck, cv, cgate, c_lambda[i], c_on, lam_init, seq_s, 2 * SAMPLE_QB, seq_s,
                        ctx=(cache_c_k8, cache_c_v), layer_idx=i)
            y2s = _gqa(dq, dk, dv, dgate, seq_s, SAMPLE_QB, seq_s,
                       ctx=(cache_d_k, cache_d_v), layer_idx=i)

        xp, hp = _outproj(y1p, y2p, xp, w_out, norm_g3, mod4, layer, prompt_row, not last)
        xs, hs = _outproj(y1s, y2s, xs, w_out, norm_g3, mod4, layer, sample_row, not last)

    return (xp.reshape(n_prompt, seq_p, D_MODEL), xs.reshape(n_sample, seq_s, D_MODEL),
            new_a[0], new_a[1],
            new_c[0][0].reshape(n_prompt, CACHE_SLOTS, 2, C_HEADS, seq_p, HEAD_DIM), new_c[1][0],
            new_c[2][0], new_c[2][1])
```

```python
import functools
import math

import jax
import jax.numpy as jnp
from jax import lax
from jax.experimental import pallas as pl
from jax.experimental.pallas import tpu as pltpu

D_MODEL = 2048
DEPTH = 4
GRID_W = 64
HEAD_DIM = 128
WINDOW = 128
ROPE_THETA = 10000.0
EPS = 1e-6
NEG_INF = -1e30
HALF_W = D_MODEL // 2
N_KV = 2
GROUP = 4
C_HEADS = 4
C_V_DIM = 2 * HEAD_DIM
CACHE_SLOTS = DEPTH // 2
COND_ROWS = 16
CTX_ROW = 8

BF16 = jnp.bfloat16
F32 = jnp.float32

VMEM_LIMIT_BYTES = 52 * 1024 * 1024
PROJ_ROWS = 512
INPROJ_ROWS = 1024
PROJ_CHUNK = 512
SAMPLE_QB = 128
KEY_CHUNK = 512
LOG2E = math.log2(math.e)
PROMPT_SEQS_PER_STEP = 4


def _params(n_axes=1, flags=None):
    return pltpu.CompilerParams(
        dimension_semantics=("arbitrary",) * n_axes,
        vmem_limit_bytes=VMEM_LIMIT_BYTES,
        flags=flags,
    )


def _dot(a, b):
    return jnp.dot(a, b, preferred_element_type=F32)


def _dot_nt(a, b):
    return lax.dot_general(a, b, (((1,), (1,)), ((), ())), preferred_element_type=F32)


def _silu(x):
    return x / (1.0 + jnp.exp(-x))


def _mod_kernel(cond_ref, w_ref, b_ref, o_ref):
    s = _silu(cond_ref[...]).astype(BF16)
    o_ref[...] = _dot(s, w_ref[...].astype(BF16)) + b_ref[...]


def _modulation(cond, ada_w, ada_b):
    bn = 1024
    n_out = 3 * D_MODEL
    return pl.pallas_call(
        _mod_kernel,
        grid=(DEPTH, n_out // bn),
        in_specs=[
            pl.BlockSpec((COND_ROWS, D_MODEL), lambda l, n: (0, 0)),
            pl.BlockSpec((None, D_MODEL, bn), lambda l, n: (l, 0, n)),
            pl.BlockSpec((None, 1, bn), lambda l, n: (l, 0, n)),
        ],
        out_specs=pl.BlockSpec((None, COND_ROWS, bn), lambda l, n: (l, 0, n)),
        out_shape=jax.ShapeDtypeStruct((DEPTH, COND_ROWS, n_out), F32),
        compiler_params=_params(2),
        name="modulation",
    )(cond, ada_w, ada_b.reshape(DEPTH, 1, n_out))


def _mod_spec(layer, part, row_fn):
    return pl.BlockSpec((None, None, 1, D_MODEL), lambda m: (layer, row_fn(m), 0, part))


def _norm_mod(x, r, g, sc, sh):
    return ((x * r) * g) * (1.0 + sc) + sh


def _prenorm_kernel(x_ref, g_ref, sh_ref, sc_ref, h_ref):
    x = x_ref[...]
    r = lax.rsqrt(jnp.mean(x * x, axis=-1, keepdims=True) + EPS)
    h_ref[...] = _norm_mod(x, r, g_ref[...], sc_ref[...], sh_ref[...]).astype(BF16)


def _prenorm(x, norm_g3, mod4, layer, row_fn):
    m_rows = x.shape[0]
    bm = PROJ_ROWS
    return pl.pallas_call(
        _prenorm_kernel,
        grid=(m_rows // bm,),
        in_specs=[
            pl.BlockSpec((bm, D_MODEL), lambda m: (m, 0)),
            pl.BlockSpec((None, 1, D_MODEL), lambda m: (layer, 0, 0)),
            _mod_spec(layer, 0, row_fn),
            _mod_spec(layer, 1, row_fn),
        ],
        out_specs=pl.BlockSpec((bm, D_MODEL), lambda m: (m, 0)),
        out_shape=jax.ShapeDtypeStruct((m_rows, D_MODEL), BF16),
        compiler_params=_params(),
        name="prenorm",
    )(x, norm_g3, mod4, mod4)


class _Seg:
    def __init__(self, kind, width, cache=None):
        self.kind = kind
        self.width = width
        self.cache = cache


def _swap_halves(a, first_half):
    return jnp.where(first_half, pltpu.roll(a, 96, axis=1), pltpu.roll(a, 32, axis=1))


def _inproj_kernel(*refs, segs, rope, seq_len, bm, n_prev, first_fill):
    it = iter(refs)
    h_ref = next(it)
    w_ref = next(it)
    gain_refs = [tuple(next(it) for _ in range(2 if rope else 1)) if s.kind == "norm" else None
                 for s in segs]
    for _ in range(n_prev):
        next(it)
    out_refs = [next(it) for _ in segs]
    cache_refs = [next(it) if s.cache else None for s in segs]

    if rope:
        lane = lax.broadcasted_iota(jnp.int32, (bm, HEAD_DIM), 1)
        first_half = (lane % 64) < 32

    chunks = []
    col = 0
    for si, seg in enumerate(segs):
        for c0 in range(0, seg.width, PROJ_CHUNK):
            chunks.append((si, c0, min(PROJ_CHUNK, seg.width - c0), col + c0))
        col += seg.width

    def matmul(chunk):
        _, _, cw, wc = chunk
        return _dot(h_ref[...], w_ref[:, wc:wc + cw])

    def epilogue(chunk, acc):
        si, c0, cw, _ = chunk
        seg, gain_ref, out_ref, cache_ref = segs[si], gain_refs[si], out_refs[si], cache_refs[si]
        if seg.kind == "norm":
            heads = []
            for hh in range(cw // HEAD_DIM):
                a = acc[:, hh * HEAD_DIM:(hh + 1) * HEAD_DIM]
                r = lax.rsqrt(jnp.mean(a * a, axis=-1, keepdims=True) + EPS)
                if rope:
                    y = (a * gain_ref[0][...] + _swap_halves(a, first_half) * gain_ref[1][...]) * r
                else:
                    y = (a * r) * gain_ref[0][...]
                heads.append(y)
            val = jnp.concatenate(heads, axis=1) if len(heads) > 1 else heads[0]
        elif seg.kind == "silu":
            val = _silu(acc)
        else:
            val = acc
        out_ref[:, c0:c0 + cw] = val.astype(BF16)
        if seg.cache:
            _, hw = seg.cache
            for j in range(bm // seq_len):
                for hh in range(cw // hw):
                    head = (c0 // hw) + hh
                    data = val[j * seq_len:(j + 1) * seq_len, hh * hw:(hh + 1) * hw]
                    if first_fill is None:
                        cache_ref[j, head] = data
                    else:
                        for slot in range(CACHE_SLOTS):
                            cache_ref[j, slot, head] = data if slot == first_fill else jnp.zeros_like(data)

    acc = matmul(chunks[0])
    for idx, chunk in enumerate(chunks):
        nxt = matmul(chunks[idx + 1]) if idx + 1 < len(chunks) else None
        epilogue(chunk, acc)
        acc = nxt


def _inproj(h, w_slab, segs, gains, rope_tabs, seq_len, cache_slot=0, cache_prev=None):
    m_rows = h.shape[0]
    bm = INPROJ_ROWS
    n_cols = w_slab.shape[1]
    rope = rope_tabs is not None
    in_specs = [
        pl.BlockSpec((bm, D_MODEL), lambda m: (m, 0)),
        pl.BlockSpec((D_MODEL, n_cols), lambda m: (0, 0), pipeline_mode=pl.Buffered(1)),
    ]
    args = [h, w_slab]
    if rope:
        cos, sin_signed = rope_tabs
        blocks_per_seq = seq_len // bm
        tab_spec = pl.BlockSpec((bm, HEAD_DIM), lambda m: (m % blocks_per_seq, 0))
    for seg, g in zip(segs, gains):
        if seg.kind != "norm":
            continue
        if rope:
            g_swapped = g.reshape(2, 2, HEAD_DIM // 4)[:, ::-1].reshape(HEAD_DIM)
            in_specs += [tab_spec, tab_spec]
            args += [cos * g[None, :], sin_signed * g_swapped[None, :]]
        else:
            in_specs.append(pl.BlockSpec((1, HEAD_DIM), lambda m: (0, 0)))
            args.append(g.reshape(1, HEAD_DIM))
    n_prev = 0 if cache_prev is None else len(cache_prev)
    aliases = {}
    for k in range(n_prev):
        aliases[len(args)] = len(segs) + k
        in_specs.append(pl.BlockSpec(memory_space=pl.ANY))
        args.append(cache_prev[k])
    out_specs = [pl.BlockSpec((bm, s.width), lambda m: (m, 0)) for s in segs]
    out_shape = [jax.ShapeDtypeStruct((m_rows, s.width), BF16) for s in segs]
    for s in segs:
        if s.cache:
            nh, hw = s.cache
            if cache_prev is None:
                out_specs.append(pl.BlockSpec((bm // seq_len, CACHE_SLOTS, nh, seq_len, hw),
                                              lambda m: (m, 0, 0, 0, 0)))
            else:
                out_specs.append(pl.BlockSpec((bm // seq_len, None, nh, seq_len, hw),
                                              lambda m: (m, cache_slot, 0, 0, 0)))
            out_shape.append(jax.ShapeDtypeStruct((m_rows // seq_len, CACHE_SLOTS, nh, seq_len, hw), F32))
    outs = pl.pallas_call(
        functools.partial(_inproj_kernel, segs=segs, rope=rope, seq_len=seq_len, bm=bm, n_prev=n_prev,
                          first_fill=cache_slot if cache_prev is None else None),
        grid=(m_rows // bm,),
        in_specs=in_specs,
        out_specs=out_specs,
        out_shape=out_shape,
        input_output_aliases=aliases,
        compiler_params=_params(),
        name="inproj",
    )(*args)
    n = len(segs)
    return list(outs[:n]), list(outs[n:])


def _softmax_pv(scores, values, sinks=None, joint=False):
    groups = 1 if sinks is None else len(sinks)
    rows = scores[0].shape[0] // groups
    if joint or len(scores) == 1:
        probs = [[] for _ in scores]
        inv = []
        for g in range(groups):
            rs = slice(g * rows, (g + 1) * rows)
            m = jnp.max(scores[0][rs], axis=-1, keepdims=True)
            for s in scores[1:]:
                m = jnp.maximum(m, jnp.max(s[rs], axis=-1, keepdims=True))
            l = None
            if sinks is not None:
                m = jnp.maximum(m, sinks[g])
                l = jnp.exp2(sinks[g] - m)
            for i, s in enumerate(scores):
                e = jnp.exp2(s[rs] - m)
                part = jnp.sum(e, axis=-1, keepdims=True)
                l = part if l is None else l + part
                probs[i].append(e.astype(BF16))
            inv.append(1.0 / l)
        o = None
        for p, v in zip(probs, values):
            pv = _dot(p[0] if groups == 1 else jnp.concatenate(p, axis=0), v)
            o = pv if o is None else o + pv
        return [o[g * rows:(g + 1) * rows] * inv[g] for g in range(groups)]
    parts = []
    for s, v in zip(scores, values):
        m = jnp.max(s, axis=-1, keepdims=True)
        e = jnp.exp2(s - m)
        parts.append((m, jnp.sum(e, axis=-1, keepdims=True), _dot(e.astype(BF16), v)))
    outs = []
    for g in range(groups):
        rs = slice(g * rows, (g + 1) * rows)
        m_all = parts[0][0][rs]
        for m, _, _ in parts[1:]:
            m_all = jnp.maximum(m_all, m[rs])
        den = None
        if sinks is not None:
            m_all = jnp.maximum(m_all, sinks[g])
            den = jnp.exp2(sinks[g] - m_all)
        num = None
        for m, l, o in parts:
            w = jnp.exp2(m[rs] - m_all)
            den = l[rs] * w if den is None else den + l[rs] * w
            num = o[rs] * w if num is None else num + o[rs] * w
        outs.append(num * (1.0 / den))
    return outs


def _stack_heads(ref, r0, rows, first_head):
    return jnp.concatenate(
        [ref[pl.ds(r0, rows), (first_head + g) * HEAD_DIM:(first_head + g + 1) * HEAD_DIM]
         for g in range(GROUP)], axis=0)


def _head_sinks(sink_ref, first_head):
    return [sink_ref[first_head + g] * LOG2E for g in range(GROUP)]


def _key_chunks(scores, values, q, k_ref, v_ref, k0, seq_len, cols_k, cols_v, chunk=KEY_CHUNK):
    step = min(seq_len, chunk)
    for c in range(0, seq_len, step):
        scores.append(_dot_nt(q, k_ref[pl.ds(k0 + c, step), cols_k]))
        values.append(v_ref[pl.ds(k0 + c, step), cols_v])


def _store_gated(o_ref, g_ref, o, r0, rows, first_head):
    if len(o) == 1:
        o = [o[0][g * rows:(g + 1) * rows] for g in range(GROUP)]
    for g in range(GROUP):
        cs = slice((first_head + g) * HEAD_DIM, (first_head + g + 1) * HEAD_DIM)
        gate = g_ref[pl.ds(r0, rows), cs].astype(F32)
        o_ref[pl.ds(r0, rows), cs] = (o[g] * gate).astype(BF16)


def _gqa_kernel(*refs, seq_len, qb, n_blocks, ctx, sink):
    it = iter(refs)
    sink_ref = next(it) if sink else None
    q_ref, k_ref, v_ref, g_ref = next(it), next(it), next(it), next(it)
    if ctx:
        ck_ref, cv_ref = next(it), next(it)
    o_ref = next(it)
    if ctx:
        kc_s, vc_s = next(it), next(it)
        for kv in range(N_KV):
            kc_s[kv] = ck_ref[kv].astype(BF16)
            vc_s[kv] = cv_ref[kv].astype(BF16)
    blocks_per_seq = seq_len // qb

    def body(n, carry):
        r0 = pl.multiple_of(n * qb, qb)
        k0 = pl.multiple_of((n // blocks_per_seq) * seq_len, seq_len)
        for kv in range(N_KV):
            hs = slice(kv * HEAD_DIM, (kv + 1) * HEAD_DIM)
            q = _stack_heads(q_ref, r0, qb, kv * GROUP)
            scores, values = [], []
            if ctx:
                scores.append(_dot_nt(q, kc_s[kv]))
                values.append(vc_s[kv])
            _key_chunks(scores, values, q, k_ref, v_ref, k0, seq_len, hs, hs)
            sinks = _head_sinks(sink_ref, kv * GROUP) if sink else None
            o = _softmax_pv(scores, values, sinks)
            _store_gated(o_ref, g_ref, o, r0, qb, kv * GROUP)
        return carry

    lax.fori_loop(0, n_blocks, body, 0)


def _gqa(q, k, v, gate, seq_len, qb, rows_per_step, sink=None, ctx=None, layer_idx=0):
    m_rows = q.shape[0]
    kvw = N_KV * HEAD_DIM
    in_specs, args = [], []
    if sink is not None:
        in_specs.append(pl.BlockSpec(memory_space=pltpu.SMEM))
        args.append(sink)
    in_specs += [
        pl.BlockSpec((rows_per_step, HALF_W), lambda b: (b, 0)),
        pl.BlockSpec((rows_per_step, kvw), lambda b: (b, 0)),
        pl.BlockSpec((rows_per_step, kvw), lambda b: (b, 0)),
        pl.BlockSpec((rows_per_step, HALF_W), lambda b: (b, 0)),
    ]
    args += [q, k, v, gate]
    scratch = []
    if ctx is not None:
        past = ctx[0].shape[-2]
        cspec = pl.BlockSpec((None, None, N_KV, past, HEAD_DIM), lambda b: (b, layer_idx, 0, 0, 0))
        in_specs += [cspec, cspec]
        args += list(ctx)
        scratch = [pltpu.VMEM((N_KV, past, HEAD_DIM), BF16)] * 2
    return pl.pallas_call(
        functools.partial(_gqa_kernel, seq_len=seq_len, qb=qb, n_blocks=rows_per_step // qb,
                          ctx=ctx is not None, sink=sink is not None),
        grid=(m_rows // rows_per_step,),
        in_specs=in_specs,
        out_specs=pl.BlockSpec((rows_per_step, HALF_W), lambda b: (b, 0)),
        out_shape=jax.ShapeDtypeStruct((m_rows, HALF_W), BF16),
        scratch_shapes=scratch,
        compiler_params=_params(),
        name="gqa",
    )(*args)


def _banded_kernel(sink_ref, q_ref, k_ref, v_ref, g_ref, ck_ref, cv_ref, o_ref,
                   kpad, vpad, kc_s, vc_s, *, seq_len):
    n_blocks = seq_len // WINDOW
    zeros = jnp.zeros((WINDOW, HEAD_DIM), BF16)
    for kv in range(N_KV):
        hs = slice(kv * HEAD_DIM, (kv + 1) * HEAD_DIM)
        kpad[kv, 0:WINDOW] = zeros
        vpad[kv, 0:WINDOW] = zeros
        kpad[kv, WINDOW + seq_len:2 * WINDOW + seq_len] = zeros
        vpad[kv, WINDOW + seq_len:2 * WINDOW + seq_len] = zeros
        kpad[kv, WINDOW:WINDOW + seq_len] = k_ref[:, hs]
        vpad[kv, WINDOW:WINDOW + seq_len] = v_ref[:, hs]
        kc_s[kv] = ck_ref[kv].astype(BF16)
        vc_s[kv] = cv_ref[kv].astype(BF16)

    rows = GROUP * WINDOW
    qi = lax.broadcasted_iota(jnp.int32, (rows, 3 * WINDOW), 0) % WINDOW
    col = lax.broadcasted_iota(jnp.int32, (rows, 3 * WINDOW), 1)
    kj = col % WINDOW
    blk = col // WINDOW
    never = 4 * WINDOW

    def body(n, carry):
        r0 = pl.multiple_of(n * WINDOW, WINDOW)
        lo = jnp.where(n > 0, 0, never)
        hi = jnp.where(n < n_blocks - 1, 0, never)
        slack = jnp.where(blk == 0, kj - qi - lo, jnp.where(blk == 2, qi - kj - hi, 0))
        valid = slack >= 0
        for kv in range(N_KV):
            q = _stack_heads(q_ref, r0, WINDOW, kv * GROUP)
            kb = kpad[kv, pl.ds(r0, 3 * WINDOW), :]
            vb = vpad[kv, pl.ds(r0, 3 * WINDOW), :]
            s_ctx = _dot_nt(q, kc_s[kv])
            s_band = jnp.where(valid, _dot_nt(q, kb), NEG_INF)
            o = _softmax_pv([s_ctx, s_band], [vc_s[kv], vb], _head_sinks(sink_ref, kv * GROUP))
            _store_gated(o_ref, g_ref, o, r0, WINDOW, kv * GROUP)
        return carry

    lax.fori_loop(0, n_blocks, body, 0)


def _banded(q, k, v, gate, sink, cache_k, cache_v, layer_idx, seq_len):
    m_rows = q.shape[0]
    kvw = N_KV * HEAD_DIM
    past = cache_k.shape[-2]
    cspec = pl.BlockSpec((None, None, N_KV, past, HEAD_DIM), lambda b: (b, layer_idx, 0, 0, 0))
    pad_rows = seq_len + 2 * WINDOW
    return pl.pallas_call(
        functools.partial(_banded_kernel, seq_len=seq_len),
        grid=(m_rows // seq_len,),
        in_specs=[
            pl.BlockSpec(memory_space=pltpu.SMEM),
            pl.BlockSpec((seq_len, HALF_W), lambda b: (b, 0)),
            pl.BlockSpec((seq_len, kvw), lambda b: (b, 0)),
            pl.BlockSpec((seq_len, kvw), lambda b: (b, 0)),
            pl.BlockSpec((seq_len, HALF_W), lambda b: (b, 0)),
            cspec, cspec,
        ],
        out_specs=pl.BlockSpec((seq_len, HALF_W), lambda b: (b, 0)),
        out_shape=jax.ShapeDtypeStruct((m_rows, HALF_W), BF16),
        scratch_shapes=[
            pltpu.VMEM((N_KV, pad_rows, HEAD_DIM), BF16),
            pltpu.VMEM((N_KV, pad_rows, HEAD_DIM), BF16),
            pltpu.VMEM((N_KV, past, HEAD_DIM), BF16),
            pltpu.VMEM((N_KV, past, HEAD_DIM), BF16),
        ],
        compiler_params=_params(),
        name="banded",
    )(sink, q, k, v, gate, cache_k, cache_v)


def _diff_kernel(*refs, seq_len, qb, n_blocks, ctx, lam_init):
    it = iter(refs)
    lam_ref, on_ref = next(it), next(it)
    q_ref, k_ref, v_ref, g_ref = next(it), next(it), next(it), next(it)
    if ctx:
        ck_ref, cv_ref = next(it), next(it)
    o_ref = next(it)
    if ctx:
        kc_s, vc_s = next(it), next(it)
        for slot in range(2 * C_HEADS):
            kc_s[slot] = ck_ref[slot].astype(BF16)
        for hd in range(C_HEADS):
            vc_s[hd] = cv_ref[hd].astype(BF16)
    lf = lam_ref[...]
    lam = (jnp.exp(jnp.sum(lf[0:1] * lf[1:2], axis=-1, keepdims=True))
           - jnp.exp(jnp.sum(lf[2:3] * lf[3:4], axis=-1, keepdims=True)) + lam_init)
    out_gain = on_ref[...]
    blocks_per_seq = seq_len // qb

    def body(n, carry):
        r0 = pl.multiple_of(n * qb, qb)
        k0 = pl.multiple_of((n // blocks_per_seq) * seq_len, seq_len)
        for hd in range(C_HEADS):
            vs = slice(hd * C_V_DIM, (hd + 1) * C_V_DIM)
            maps = []
            for mp in range(2):
                slot = mp * C_HEADS + hd
                hs = slice(slot * HEAD_DIM, (slot + 1) * HEAD_DIM)
                q = q_ref[pl.ds(r0, qb), hs]
                scores, values = [], []
                if ctx:
                    scores.append(_dot_nt(q, kc_s[slot]))
                    values.append(vc_s[hd])
                _key_chunks(scores, values, q, k_ref, v_ref, k0, seq_len, hs, vs, chunk=seq_len)
                maps.append(_softmax_pv(scores, values, joint=True)[0])
            d = maps[0] - lam * maps[1]
            r = lax.rsqrt(jnp.mean(d * d, axis=-1, keepdims=True) + EPS)
            y = ((d * r) * out_gain) * (1.0 - lam_init)
            gate = g_ref[pl.ds(r0, qb), vs].astype(F32)
            o_ref[pl.ds(r0, qb), vs] = (y * gate).astype(BF16)
        return carry

    lax.fori_loop(0, n_blocks, body, 0)


def _diff(q, k, v, gate, c_lam, c_on, lam_init, seq_len, qb, rows_per_step, ctx=None, layer_idx=0):
    m_rows = q.shape[0]
    in_specs = [
        pl.BlockSpec((4, HEAD_DIM), lambda b: (0, 0)),
        pl.BlockSpec((1, C_V_DIM), lambda b: (0, 0)),
    ] + [pl.BlockSpec((rows_per_step, HALF_W), lambda b: (b, 0))] * 4
    args = [c_lam, c_on, q, k, v, gate]
    scratch = []
    if ctx is not None:
        past = ctx[0].shape[-2]
        in_specs += [
            pl.BlockSpec((None, None, 2 * C_HEADS, past, HEAD_DIM), lambda b: (b, layer_idx, 0, 0, 0)),
            pl.BlockSpec((None, None, C_HEADS, past, C_V_DIM), lambda b: (b, layer_idx, 0, 0, 0)),
        ]
        args += list(ctx)
        scratch = [pltpu.VMEM((2 * C_HEADS, past, HEAD_DIM), BF16),
                   pltpu.VMEM((C_HEADS, past, C_V_DIM), BF16)]
    return pl.pallas_call(
        functools.partial(_diff_kernel, seq_len=seq_len, qb=qb, n_blocks=rows_per_step // qb,
                          ctx=ctx is not None, lam_init=lam_init),
        grid=(m_rows // rows_per_step,),
        in_specs=in_specs,
        out_specs=pl.BlockSpec((rows_per_step, HALF_W), lambda b: (b, 0)),
        out_shape=jax.ShapeDtypeStruct((m_rows, HALF_W), BF16),
        scratch_shapes=scratch,
        compiler_params=_params(),
        name="diff_attn",
    )(*args)


def _conv_kernel(u_ref, b_ref, c_ref, g_ref, w_ref, o_ref, *, seq_len, rows):
    z = c_ref[...].astype(F32) * u_ref[...].astype(F32)
    t = lax.broadcasted_iota(jnp.int32, (rows, 1), 0) % seq_len
    z_prev = jnp.where(t == 0, 0.0, pltpu.roll(z, 1, axis=0))
    z_next = jnp.where(t == seq_len - 1, 0.0, pltpu.roll(z, rows - 1, axis=0))
    w = w_ref[...]
    conv = z_prev * w[0:1] + z * w[1:2] + z_next * w[2:3]
    o_ref[...] = ((b_ref[...].astype(F32) * conv) * g_ref[...].astype(F32)).astype(BF16)


def _short_conv(u, bgate, cgate, gate, conv_w, seq_len):
    m_rows = u.shape[0]
    rows = max(seq_len, 512)
    spec = pl.BlockSpec((rows, HALF_W), lambda m: (m, 0))
    return pl.pallas_call(
        functools.partial(_conv_kernel, seq_len=seq_len, rows=rows),
        grid=(m_rows // rows,),
        in_specs=[spec, spec, spec, spec, pl.BlockSpec((3, HALF_W), lambda m: (0, 0))],
        out_specs=spec,
        out_shape=jax.ShapeDtypeStruct((m_rows, HALF_W), BF16),
        compiler_params=_params(),
        name="short_conv",
    )(u, bgate, cgate, gate, conv_w)


def _outproj_kernel(*refs, bm, emit_h):
    it = iter(refs)
    y1_ref, y2_ref, x_ref, w_ref, gt_ref = next(it), next(it), next(it), next(it), next(it)
    if emit_h:
        gn_ref, sh_ref, sc_ref = next(it), next(it), next(it)
    xo_ref = next(it)
    if emit_h:
        ho_ref = next(it)
    ssq = jnp.zeros((bm, 1), F32)
    chunks = [slice(c, c + PROJ_CHUNK) for c in range(0, D_MODEL, PROJ_CHUNK)]
    for cs in chunks:
        acc = _dot(y1_ref[...], w_ref[0:HALF_W, cs]) + _dot(y2_ref[...], w_ref[HALF_W:D_MODEL, cs])
        xn = x_ref[:, cs] + gt_ref[:, cs] * acc
        xo_ref[:, cs] = xn
        if emit_h:
            ssq = ssq + jnp.sum(xn * xn, axis=-1, keepdims=True)
    if emit_h:
        r = lax.rsqrt(ssq * (1.0 / D_MODEL) + EPS)
        for cs in chunks:
            ho_ref[:, cs] = _norm_mod(xo_ref[:, cs], r, gn_ref[:, cs], sc_ref[:, cs],
                                      sh_ref[:, cs]).astype(BF16)


def _outproj(y1, y2, x, w_out, norm_g3, mod4, layer, row_fn, emit_h):
    m_rows = x.shape[0]
    bm = PROJ_ROWS
    in_specs = [
        pl.BlockSpec((bm, HALF_W), lambda m: (m, 0)),
        pl.BlockSpec((bm, HALF_W), lambda m: (m, 0)),
        pl.BlockSpec((bm, D_MODEL), lambda m: (m, 0)),
        pl.BlockSpec((D_MODEL, D_MODEL), lambda m: (0, 0), pipeline_mode=pl.Buffered(1)),
        _mod_spec(layer, 2, row_fn),
    ]
    args = [y1, y2, x, w_out, mod4]
    out_specs = [pl.BlockSpec((bm, D_MODEL), lambda m: (m, 0))]
    out_shape = [jax.ShapeDtypeStruct((m_rows, D_MODEL), F32)]
    if emit_h:
        in_specs += [
            pl.BlockSpec((None, 1, D_MODEL), lambda m: (layer + 1, 0, 0)),
            _mod_spec(layer + 1, 0, row_fn),
            _mod_spec(layer + 1, 1, row_fn),
        ]
        args += [norm_g3, mod4, mod4]
        out_specs.append(pl.BlockSpec((bm, D_MODEL), lambda m: (m, 0)))
        out_shape.append(jax.ShapeDtypeStruct((m_rows, D_MODEL), BF16))
    outs = pl.pallas_call(
        functools.partial(_outproj_kernel, bm=bm, emit_h=emit_h),
        grid=(m_rows // bm,),
        in_specs=in_specs,
        out_specs=out_specs,
        out_shape=out_shape,
        compiler_params=_params(),
        name="outproj",
    )(*args)
    return (outs[0], outs[1]) if emit_h else (outs[0], None)


def _rope_tables(n_tokens):
    t = jnp.arange(n_tokens)
    row = t // GRID_W
    col = t % GRID_W
    n_freq = HEAD_DIM // 4
    inv = ROPE_THETA ** (-jnp.arange(n_freq, dtype=F32) / n_freq)
    ang_r = row[:, None] * inv
    ang_c = col[:, None] * inv
    cos = jnp.concatenate([jnp.cos(ang_r)] * 2 + [jnp.cos(ang_c)] * 2, axis=1)
    sin = jnp.concatenate([-jnp.sin(ang_r), jnp.sin(ang_r), -jnp.sin(ang_c), jnp.sin(ang_c)], axis=1)
    return cos, sin


def kernel(x_prompt, x_sample, cache_a_k, cache_a_v, cache_c_k, cache_c_v, cache_d_k, cache_d_v,
           c, c_ctx, norm_g, ada_w, ada_b, ev_w_in, ev_w_out, a_q_norm, a_k_norm, a_sink, b_conv,
           od_w_in, od_w_out, c_q_norm, c_k_norm, c_lambda, c_out_norm, d_q_norm, d_k_norm):
    n_prompt, seq_p, _ = x_prompt.shape
    n_sample, seq_s, _ = x_sample.shape
    past = cache_a_k.shape[-2]
    scale = HEAD_DIM ** -0.5 * LOG2E

    cond = jnp.concatenate(
        [c, c_ctx[None, :], jnp.zeros((COND_ROWS - n_sample - 1, D_MODEL), F32)], axis=0)
    mod4 = _modulation(cond, ada_w, ada_b).reshape(DEPTH, COND_ROWS, 1, 3 * D_MODEL)
    norm_g3 = norm_g.reshape(DEPTH, 1, D_MODEL)
    rope_tabs = _rope_tables(seq_s)

    prompt_row = lambda m: CTX_ROW
    sample_row = lambda m: (m * PROJ_ROWS) // seq_s

    xp = x_prompt.reshape(n_prompt * seq_p, D_MODEL)
    xs = x_sample.reshape(n_sample * seq_s, D_MODEL)
    hp = _prenorm(xp, norm_g3, mod4, 0, prompt_row)
    hs = _prenorm(xs, norm_g3, mod4, 0, sample_row)

    cache_c_k8 = cache_c_k.reshape(n_sample, DEPTH // 2, 2 * C_HEADS, past, HEAD_DIM)
    prompt_rows = PROMPT_SEQS_PER_STEP * seq_p

    new_a = None
    new_c = (None, None, None)
    for layer in range(DEPTH):
        i = layer // 2
        last = layer == DEPTH - 1
        if layer % 2 == 0:
            w_out = ev_w_out[i].astype(BF16)
            slabs = [ev_w_in[i][:, a:b].astype(BF16) for a, b in ((0, 2560), (2560, 4608), (4608, 6656))]
            gq = a_q_norm[i] * scale
            gk = a_k_norm[i]
            kv_cache = (N_KV, HEAD_DIM)

            def project(h, seq_len, tabs, cache, prev=None):
                segs1 = [_Seg("norm", HALF_W), _Seg("norm", N_KV * HEAD_DIM, cache),
                         _Seg("plain", N_KV * HEAD_DIM, cache), _Seg("silu", HALF_W)]
                (q, k, v, ga), caches = _inproj(h, slabs[0], segs1, [gq, gk, None, None], tabs, seq_len,
                                                cache_slot=i, cache_prev=prev)
                (u, bg), _ = _inproj(h, slabs[1], [_Seg("plain", HALF_W)] * 2, [None] * 2, None, seq_len)
                (cg, gb), _ = _inproj(h, slabs[2], [_Seg("plain", HALF_W), _Seg("silu", HALF_W)],
                                      [None] * 2, None, seq_len)
                return q, k, v, ga, u, bg, cg, gb, caches

            q, k, v, ga, u, bg, cg, gb, new_a = project(hp, seq_p, None, kv_cache, new_a)
            y1p = _gqa(q, k, v, ga, seq_p, seq_p, prompt_rows, sink=a_sink[i])
            y2p = _short_conv(u, bg, cg, gb, b_conv[i], seq_p)

            q, k, v, ga, u, bg, cg, gb, _ = project(hs, seq_s, rope_tabs, None)
            y1s = _banded(q, k, v, ga, a_sink[i], cache_a_k, cache_a_v, i, seq_s)
            y2s = _short_conv(u, bg, cg, gb, b_conv[i], seq_s)
        else:
            lam_init = 0.8 - 0.6 * math.exp(-0.3 * layer)
            w_out = od_w_out[i].astype(BF16)
            slabs = [od_w_in[i][:, a:b].astype(BF16) for a, b in ((0, 2048), (2048, 4096), (4096, 6656))]
            gcq = c_q_norm[i] * scale
            gck = c_k_norm[i]
            gdq = d_q_norm[i] * scale
            gdk = d_k_norm[i]
            c_on = c_out_norm[i].reshape(1, C_V_DIM)

            def project(h, seq_len, tabs, with_cache, prev=(None, None, None)):
                ck_cache = (2 * C_HEADS, HEAD_DIM) if with_cache else None
                cv_cache = (C_HEADS, C_V_DIM) if with_cache else None
                kv_cache = (N_KV, HEAD_DIM) if with_cache else None
                (cq, ck), cc1 = _inproj(h, slabs[0], [_Seg("norm", HALF_W), _Seg("norm", HALF_W, ck_cache)],
                                        [gcq, gck], tabs, seq_len, cache_slot=i, cache_prev=prev[0])
                (cv, cgate), cc2 = _inproj(h, slabs[1], [_Seg("plain", HALF_W, cv_cache), _Seg("silu", HALF_W)],
                                           [None] * 2, None, seq_len, cache_slot=i, cache_prev=prev[1])
                segs3 = [_Seg("norm", HALF_W), _Seg("norm", N_KV * HEAD_DIM, kv_cache),
                         _Seg("plain", N_KV * HEAD_DIM, kv_cache), _Seg("silu", HALF_W)]
                (dq, dk, dv, dgate), cc3 = _inproj(h, slabs[2], segs3, [gdq, gdk, None, None], tabs, seq_len,
                                                   cache_slot=i, cache_prev=prev[2])
                return cq, ck, cv, cgate, dq, dk, dv, dgate, (cc1, cc2, cc3)

            cq, ck, cv, cgate, dq, dk, dv, dgate, new_c = project(hp, seq_p, None, True, new_c)
            y1p = _diff(cq, ck, cv, cgate, c_lambda[i], c_on, lam_init, seq_p, seq_p, prompt_rows)
            y2p = _gqa(dq, dk, dv, dgate, seq_p, seq_p, prompt_rows)

            cq, ck, cv, cgate, dq, dk, dv, dgate, _ = project(hs, seq_s, rope_tabs, False)
            y1s = _diff(cq, ck, cv, cgate, c_lambda[i], c_on, lam_init, seq_s, 2 * SAMPLE_QB, seq_s,
                        ctx=(cache_c_k8, cache_c_v), layer_idx=i)
            y2s = _gqa(dq, dk, dv, dgate, seq_s, 2 * SAMPLE_QB, seq_s,
                       ctx=(cache_d_k, cache_d_v), layer_idx=i)

        xp, hp = _outproj(y1p, y2p, xp, w_out, norm_g3, mod4, layer, prompt_row, not last)
        xs, hs = _outproj(y1s, y2s, xs, w_out, norm_g3, mod4, layer, sample_row, not last)

    return (xp.reshape(n_prompt, seq_p, D_MODEL), xs.reshape(n_sample, seq_s, D_MODEL),
            new_a[0], new_a[1],
            new_c[0][0].reshape(n_prompt, CACHE_SLOTS, 2, C_HEADS, seq_p, HEAD_DIM), new_c[1][0],
            new_c[2][0], new_c[2][1])
```

```python
import functools
import math

import jax
import jax.numpy as jnp
from jax import lax
from jax.experimental import pallas as pl
from jax.experimental.pallas import tpu as pltpu

D_MODEL = 2048
DEPTH = 4
GRID_W = 64
HEAD_DIM = 128
WINDOW = 128
ROPE_THETA = 10000.0
EPS = 1e-6
NEG_INF = -1e30
HALF_W = D_MODEL // 2
N_KV = 2
GROUP = 4
C_HEADS = 4
C_V_DIM = 2 * HEAD_DIM
CACHE_SLOTS = DEPTH // 2
COND_ROWS = 16
CTX_ROW = 8

BF16 = jnp.bfloat16
F32 = jnp.float32

VMEM_LIMIT_BYTES = 52 * 1024 * 1024
PROJ_ROWS = 512
INPROJ_ROWS = 1024
PROJ_CHUNK = 512
SAMPLE_QB = 128
KEY_CHUNK = 512
HALO_ROWS = 16
LOG2E = math.log2(math.e)
PROMPT_SEQS_PER_STEP = 4


def _params(n_axes=1, flags=None):
    return pltpu.CompilerParams(
        dimension_semantics=("arbitrary",) * n_axes,
        vmem_limit_bytes=VMEM_LIMIT_BYTES,
        flags=flags,
    )


def _dot(a, b):
    return jnp.dot(a, b, preferred_element_type=F32)


def _dot_nt(a, b):
    return lax.dot_general(a, b, (((1,), (1,)), ((), ())), preferred_element_type=F32)


def _silu(x):
    return x / (1.0 + jnp.exp(-x))


def _mod_kernel(cond_ref, w_ref, b_ref, o_ref):
    s = _silu(cond_ref[...]).astype(BF16)
    o_ref[...] = _dot(s, w_ref[...].astype(BF16)) + b_ref[...]


def _modulation(cond, ada_w, ada_b):
    bn = 1024
    n_out = 3 * D_MODEL
    return pl.pallas_call(
        _mod_kernel,
        grid=(DEPTH, n_out // bn),
        in_specs=[
            pl.BlockSpec((COND_ROWS, D_MODEL), lambda l, n: (0, 0)),
            pl.BlockSpec((None, D_MODEL, bn), lambda l, n: (l, 0, n)),
            pl.BlockSpec((None, 1, bn), lambda l, n: (l, 0, n)),
        ],
        out_specs=pl.BlockSpec((None, COND_ROWS, bn), lambda l, n: (l, 0, n)),
        out_shape=jax.ShapeDtypeStruct((DEPTH, COND_ROWS, n_out), F32),
        compiler_params=_params(2),
        name="modulation",
    )(cond, ada_w, ada_b.reshape(DEPTH, 1, n_out))


def _mod_spec(layer, part, row_fn):
    return pl.BlockSpec((None, None, 1, D_MODEL), lambda m: (layer, row_fn(m), 0, part))


def _norm_mod(x, r, g, sc, sh):
    return ((x * r) * g) * (1.0 + sc) + sh


def _prenorm_kernel(x_ref, g_ref, sh_ref, sc_ref, h_ref):
    x = x_ref[...]
    r = lax.rsqrt(jnp.mean(x * x, axis=-1, keepdims=True) + EPS)
    h_ref[...] = _norm_mod(x, r, g_ref[...], sc_ref[...], sh_ref[...]).astype(BF16)


def _prenorm(x, norm_g3, mod4, layer, row_fn):
    m_rows = x.shape[0]
    bm = PROJ_ROWS
    return pl.pallas_call(
        _prenorm_kernel,
        grid=(m_rows // bm,),
        in_specs=[
            pl.BlockSpec((bm, D_MODEL), lambda m: (m, 0)),
            pl.BlockSpec((None, 1, D_MODEL), lambda m: (layer, 0, 0)),
            _mod_spec(layer, 0, row_fn),
            _mod_spec(layer, 1, row_fn),
        ],
        out_specs=pl.BlockSpec((bm, D_MODEL), lambda m: (m, 0)),
        out_shape=jax.ShapeDtypeStruct((m_rows, D_MODEL), BF16),
        compiler_params=_params(),
        name="prenorm",
    )(x, norm_g3, mod4, mod4)


class _Seg:
    def __init__(self, kind, width, cache=None):
        self.kind = kind
        self.width = width
        self.cache = cache


def _swap_halves(a, first_half):
    return jnp.where(first_half, pltpu.roll(a, 96, axis=1), pltpu.roll(a, 32, axis=1))


def _inproj_kernel(*refs, segs, rope, seq_len, bm, n_prev, first_fill):
    it = iter(refs)
    h_ref = next(it)
    w_ref = next(it)
    gain_refs = [tuple(next(it) for _ in range(2 if rope else 1)) if s.kind == "norm" else None
                 for s in segs]
    for _ in range(n_prev):
        next(it)
    out_refs = [next(it) for _ in segs]
    cache_refs = [next(it) if s.cache else None for s in segs]

    if rope:
        lane = lax.broadcasted_iota(jnp.int32, (bm, HEAD_DIM), 1)
        first_half = (lane % 64) < 32

    chunks = []
    col = 0
    for si, seg in enumerate(segs):
        for c0 in range(0, seg.width, PROJ_CHUNK):
            chunks.append((si, c0, min(PROJ_CHUNK, seg.width - c0), col + c0))
        col += seg.width

    def matmul(chunk):
        _, _, cw, wc = chunk
        return _dot(h_ref[...], w_ref[:, wc:wc + cw])

    def epilogue(chunk, acc):
        si, c0, cw, _ = chunk
        seg, gain_ref, out_ref, cache_ref = segs[si], gain_refs[si], out_refs[si], cache_refs[si]
        if seg.kind == "norm":
            heads = []
            for hh in range(cw // HEAD_DIM):
                a = acc[:, hh * HEAD_DIM:(hh + 1) * HEAD_DIM]
                r = lax.rsqrt(jnp.mean(a * a, axis=-1, keepdims=True) + EPS)
                if rope:
                    y = (a * gain_ref[0][...] + _swap_halves(a, first_half) * gain_ref[1][...]) * r
                else:
                    y = (a * r) * gain_ref[0][...]
                heads.append(y)
            val = jnp.concatenate(heads, axis=1) if len(heads) > 1 else heads[0]
        elif seg.kind == "silu":
            val = _silu(acc)
        else:
            val = acc
        out_ref[:, c0:c0 + cw] = val.astype(BF16)
        if seg.cache:
            _, hw = seg.cache
            for j in range(bm // seq_len):
                for hh in range(cw // hw):
                    head = (c0 // hw) + hh
                    data = val[j * seq_len:(j + 1) * seq_len, hh * hw:(hh + 1) * hw]
                    if first_fill is None:
                        cache_ref[j, head] = data
                    else:
                        for slot in range(CACHE_SLOTS):
                            cache_ref[j, slot, head] = data if slot == first_fill else jnp.zeros_like(data)

    acc = matmul(chunks[0])
    for idx, chunk in enumerate(chunks):
        nxt = matmul(chunks[idx + 1]) if idx + 1 < len(chunks) else None
        epilogue(chunk, acc)
        acc = nxt


def _inproj(h, w_slab, segs, gains, rope_tabs, seq_len, cache_slot=0, cache_prev=None):
    m_rows = h.shape[0]
    bm = INPROJ_ROWS
    n_cols = w_slab.shape[1]
    rope = rope_tabs is not None
    in_specs = [
        pl.BlockSpec((bm, D_MODEL), lambda m: (m, 0)),
        pl.BlockSpec((D_MODEL, n_cols), lambda m: (0, 0), pipeline_mode=pl.Buffered(1)),
    ]
    args = [h, w_slab]
    if rope:
        cos, sin_signed = rope_tabs
        blocks_per_seq = seq_len // bm
        tab_spec = pl.BlockSpec((bm, HEAD_DIM), lambda m: (m % blocks_per_seq, 0))
    for seg, g in zip(segs, gains):
        if seg.kind != "norm":
            continue
        if rope:
            g_swapped = g.reshape(2, 2, HEAD_DIM // 4)[:, ::-1].reshape(HEAD_DIM)
            in_specs += [tab_spec, tab_spec]
            args += [cos * g[None, :], sin_signed * g_swapped[None, :]]
        else:
            in_specs.append(pl.BlockSpec((1, HEAD_DIM), lambda m: (0, 0)))
            args.append(g.reshape(1, HEAD_DIM))
    n_prev = 0 if cache_prev is None else len(cache_prev)
    aliases = {}
    for k in range(n_prev):
        aliases[len(args)] = len(segs) + k
        in_specs.append(pl.BlockSpec(memory_space=pl.ANY))
        args.append(cache_prev[k])
    out_specs = [pl.BlockSpec((bm, s.width), lambda m: (m, 0)) for s in segs]
    out_shape = [jax.ShapeDtypeStruct((m_rows, s.width), BF16) for s in segs]
    for s in segs:
        if s.cache:
            nh, hw = s.cache
            if cache_prev is None:
                out_specs.append(pl.BlockSpec((bm // seq_len, CACHE_SLOTS, nh, seq_len, hw),
                                              lambda m: (m, 0, 0, 0, 0)))
            else:
                out_specs.append(pl.BlockSpec((bm // seq_len, None, nh, seq_len, hw),
                                              lambda m: (m, cache_slot, 0, 0, 0)))
            out_shape.append(jax.ShapeDtypeStruct((m_rows // seq_len, CACHE_SLOTS, nh, seq_len, hw), F32))
    outs = pl.pallas_call(
        functools.partial(_inproj_kernel, segs=segs, rope=rope, seq_len=seq_len, bm=bm, n_prev=n_prev,
                          first_fill=cache_slot if cache_prev is None else None),
        grid=(m_rows // bm,),
        in_specs=in_specs,
        out_specs=out_specs,
        out_shape=out_shape,
        input_output_aliases=aliases,
        compiler_params=_params(),
        name="inproj",
    )(*args)
    n = len(segs)
    return list(outs[:n]), list(outs[n:])


def _softmax_pv(scores, values, sinks=None, joint=False):
    groups = 1 if sinks is None else len(sinks)
    rows = scores[0].shape[0] // groups
    if joint or len(scores) == 1:
        probs = [[] for _ in scores]
        inv = []
        for g in range(groups):
            rs = slice(g * rows, (g + 1) * rows)
            m = jnp.max(scores[0][rs], axis=-1, keepdims=True)
            for s in scores[1:]:
                m = jnp.maximum(m, jnp.max(s[rs], axis=-1, keepdims=True))
            l = None
            if sinks is not None:
                m = jnp.maximum(m, sinks[g])
                l = jnp.exp2(sinks[g] - m)
            for i, s in enumerate(scores):
                e = jnp.exp2(s[rs] - m)
                part = jnp.sum(e, axis=-1, keepdims=True)
                l = part if l is None else l + part
                probs[i].append(e.astype(BF16))
            inv.append(1.0 / l)
        o = None
        for p, v in zip(probs, values):
            pv = _dot(p[0] if groups == 1 else jnp.concatenate(p, axis=0), v)
            o = pv if o is None else o + pv
        return [o[g * rows:(g + 1) * rows] * inv[g] for g in range(groups)]
    parts = []
    for s, v in zip(scores, values):
        m = jnp.max(s, axis=-1, keepdims=True)
        e = jnp.exp2(s - m)
        parts.append((m, jnp.sum(e, axis=-1, keepdims=True), _dot(e.astype(BF16), v)))
    outs = []
    for g in range(groups):
        rs = slice(g * rows, (g + 1) * rows)
        m_all = parts[0][0][rs]
        for m, _, _ in parts[1:]:
            m_all = jnp.maximum(m_all, m[rs])
        den = None
        if sinks is not None:
            m_all = jnp.maximum(m_all, sinks[g])
            den = jnp.exp2(sinks[g] - m_all)
        num = None
        for m, l, o in parts:
            w = jnp.exp2(m[rs] - m_all)
            den = l[rs] * w if den is None else den + l[rs] * w
            num = o[rs] * w if num is None else num + o[rs] * w
        outs.append(num * (1.0 / den))
    return outs


def _stack_heads(ref, r0, rows, first_head):
    return jnp.concatenate(
        [ref[pl.ds(r0, rows), (first_head + g) * HEAD_DIM:(first_head + g + 1) * HEAD_DIM]
         for g in range(GROUP)], axis=0)


def _head_sinks(sink_ref, first_head):
    return [sink_ref[first_head + g] * LOG2E for g in range(GROUP)]


def _key_chunks(scores, values, q, k_ref, v_ref, k0, seq_len, cols_k, cols_v, chunk=KEY_CHUNK):
    step = min(seq_len, chunk)
    for c in range(0, seq_len, step):
        scores.append(_dot_nt(q, k_ref[pl.ds(k0 + c, step), cols_k]))
        values.append(v_ref[pl.ds(k0 + c, step), cols_v])


def _store_gated(o_ref, g_ref, o, r0, rows, first_head):
    if len(o) == 1:
        o = [o[0][g * rows:(g + 1) * rows] for g in range(GROUP)]
    for g in range(GROUP):
        cs = slice((first_head + g) * HEAD_DIM, (first_head + g + 1) * HEAD_DIM)
        gate = g_ref[pl.ds(r0, rows), cs].astype(F32)
        o_ref[pl.ds(r0, rows), cs] = (o[g] * gate).astype(BF16)


def _gqa_kernel(*refs, seq_len, qb, n_blocks, ctx, sink):
    it = iter(refs)
    sink_ref = next(it) if sink else None
    q_ref, k_ref, v_ref, g_ref = next(it), next(it), next(it), next(it)
    if ctx:
        ck_ref, cv_ref = next(it), next(it)
    o_ref = next(it)
    if ctx:
        kc_s, vc_s = next(it), next(it)
        for kv in range(N_KV):
            kc_s[kv] = ck_ref[kv].astype(BF16)
            vc_s[kv] = cv_ref[kv].astype(BF16)
    blocks_per_seq = seq_len // qb

    def body(n, carry):
        r0 = pl.multiple_of(n * qb, qb)
        k0 = pl.multiple_of((n // blocks_per_seq) * seq_len, seq_len)
        for kv in range(N_KV):
            hs = slice(kv * HEAD_DIM, (kv + 1) * HEAD_DIM)
            q = _stack_heads(q_ref, r0, qb, kv * GROUP)
            scores, values = [], []
            if ctx:
                scores.append(_dot_nt(q, kc_s[kv]))
                values.append(vc_s[kv])
            _key_chunks(scores, values, q, k_ref, v_ref, k0, seq_len, hs, hs)
            sinks = _head_sinks(sink_ref, kv * GROUP) if sink else None
            o = _softmax_pv(scores, values, sinks)
            _store_gated(o_ref, g_ref, o, r0, qb, kv * GROUP)
        return carry

    lax.fori_loop(0, n_blocks, body, 0)


def _gqa(q, k, v, gate, seq_len, qb, rows_per_step, sink=None, ctx=None, layer_idx=0):
    m_rows = q.shape[0]
    kvw = N_KV * HEAD_DIM
    in_specs, args = [], []
    if sink is not None:
        in_specs.append(pl.BlockSpec(memory_space=pltpu.SMEM))
        args.append(sink)
    in_specs += [
        pl.BlockSpec((rows_per_step, HALF_W), lambda b: (b, 0)),
        pl.BlockSpec((rows_per_step, kvw), lambda b: (b, 0)),
        pl.BlockSpec((rows_per_step, kvw), lambda b: (b, 0)),
        pl.BlockSpec((rows_per_step, HALF_W), lambda b: (b, 0)),
    ]
    args += [q, k, v, gate]
    scratch = []
    if ctx is not None:
        past = ctx[0].shape[-2]
        cspec = pl.BlockSpec((None, None, N_KV, past, HEAD_DIM), lambda b: (b, layer_idx, 0, 0, 0))
        in_specs += [cspec, cspec]
        args += list(ctx)
        scratch = [pltpu.VMEM((N_KV, past, HEAD_DIM), BF16)] * 2
    return pl.pallas_call(
        functools.partial(_gqa_kernel, seq_len=seq_len, qb=qb, n_blocks=rows_per_step // qb,
                          ctx=ctx is not None, sink=sink is not None),
        grid=(m_rows // rows_per_step,),
        in_specs=in_specs,
        out_specs=pl.BlockSpec((rows_per_step, HALF_W), lambda b: (b, 0)),
        out_shape=jax.ShapeDtypeStruct((m_rows, HALF_W), BF16),
        scratch_shapes=scratch,
        compiler_params=_params(),
        name="gqa",
    )(*args)


def _banded_kernel(sink_ref, q_ref, k_ref, v_ref, g_ref, ck_ref, cv_ref, o_ref,
                   kpad, vpad, kc_s, vc_s, *, seq_len):
    n_blocks = seq_len // WINDOW
    zeros = jnp.zeros((WINDOW, HEAD_DIM), BF16)
    for kv in range(N_KV):
        hs = slice(kv * HEAD_DIM, (kv + 1) * HEAD_DIM)
        kpad[kv, 0:WINDOW] = zeros
        vpad[kv, 0:WINDOW] = zeros
        kpad[kv, WINDOW + seq_len:2 * WINDOW + seq_len] = zeros
        vpad[kv, WINDOW + seq_len:2 * WINDOW + seq_len] = zeros
        kpad[kv, WINDOW:WINDOW + seq_len] = k_ref[:, hs]
        vpad[kv, WINDOW:WINDOW + seq_len] = v_ref[:, hs]
        kc_s[kv] = ck_ref[kv].astype(BF16)
        vc_s[kv] = cv_ref[kv].astype(BF16)

    rows = GROUP * WINDOW
    qi = lax.broadcasted_iota(jnp.int32, (rows, 3 * WINDOW), 0) % WINDOW
    col = lax.broadcasted_iota(jnp.int32, (rows, 3 * WINDOW), 1)
    kj = col % WINDOW
    blk = col // WINDOW
    never = 4 * WINDOW

    def body(n, carry):
        r0 = pl.multiple_of(n * WINDOW, WINDOW)
        lo = jnp.where(n > 0, 0, never)
        hi = jnp.where(n < n_blocks - 1, 0, never)
        slack = jnp.where(blk == 0, kj - qi - lo, jnp.where(blk == 2, qi - kj - hi, 0))
        valid = slack >= 0
        for kv in range(N_KV):
            q = _stack_heads(q_ref, r0, WINDOW, kv * GROUP)
            kb = kpad[kv, pl.ds(r0, 3 * WINDOW), :]
            vb = vpad[kv, pl.ds(r0, 3 * WINDOW), :]
            s_ctx = _dot_nt(q, kc_s[kv])
            s_band = jnp.where(valid, _dot_nt(q, kb), NEG_INF)
            o = _softmax_pv([s_ctx, s_band], [vc_s[kv], vb], _head_sinks(sink_ref, kv * GROUP))
            _store_gated(o_ref, g_ref, o, r0, WINDOW, kv * GROUP)
        return carry

    lax.fori_loop(0, n_blocks, body, 0)


def _banded(q, k, v, gate, sink, cache_k, cache_v, layer_idx, seq_len):
    m_rows = q.shape[0]
    kvw = N_KV * HEAD_DIM
    past = cache_k.shape[-2]
    cspec = pl.BlockSpec((None, None, N_KV, past, HEAD_DIM), lambda b: (b, layer_idx, 0, 0, 0))
    pad_rows = seq_len + 2 * WINDOW
    return pl.pallas_call(
        functools.partial(_banded_kernel, seq_len=seq_len),
        grid=(m_rows // seq_len,),
        in_specs=[
            pl.BlockSpec(memory_space=pltpu.SMEM),
            pl.BlockSpec((seq_len, HALF_W), lambda b: (b, 0)),
            pl.BlockSpec((seq_len, kvw), lambda b: (b, 0)),
            pl.BlockSpec((seq_len, kvw), lambda b: (b, 0)),
            pl.BlockSpec((seq_len, HALF_W), lambda b: (b, 0)),
            cspec, cspec,
        ],
        out_specs=pl.BlockSpec((seq_len, HALF_W), lambda b: (b, 0)),
        out_shape=jax.ShapeDtypeStruct((m_rows, HALF_W), BF16),
        scratch_shapes=[
            pltpu.VMEM((N_KV, pad_rows, HEAD_DIM), BF16),
            pltpu.VMEM((N_KV, pad_rows, HEAD_DIM), BF16),
            pltpu.VMEM((N_KV, past, HEAD_DIM), BF16),
            pltpu.VMEM((N_KV, past, HEAD_DIM), BF16),
        ],
        compiler_params=_params(),
        name="banded",
    )(sink, q, k, v, gate, cache_k, cache_v)


def _diff_kernel(*refs, seq_len, qb, n_blocks, ctx, lam_init):
    it = iter(refs)
    lam_ref, on_ref = next(it), next(it)
    q_ref, k_ref, v_ref, g_ref = next(it), next(it), next(it), next(it)
    if ctx:
        ck_ref, cv_ref = next(it), next(it)
    o_ref = next(it)
    if ctx:
        kc_s, vc_s = next(it), next(it)
        for slot in range(2 * C_HEADS):
            kc_s[slot] = ck_ref[slot].astype(BF16)
        for hd in range(C_HEADS):
            vc_s[hd] = cv_ref[hd].astype(BF16)
    lf = lam_ref[...]
    lam = (jnp.exp(jnp.sum(lf[0:1] * lf[1:2], axis=-1, keepdims=True))
           - jnp.exp(jnp.sum(lf[2:3] * lf[3:4], axis=-1, keepdims=True)) + lam_init)
    out_gain = on_ref[...]
    blocks_per_seq = seq_len // qb

    def body(n, carry):
        r0 = pl.multiple_of(n * qb, qb)
        k0 = pl.multiple_of((n // blocks_per_seq) * seq_len, seq_len)
        for hd in range(C_HEADS):
            vs = slice(hd * C_V_DIM, (hd + 1) * C_V_DIM)
            maps = []
            for mp in range(2):
                slot = mp * C_HEADS + hd
                hs = slice(slot * HEAD_DIM, (slot + 1) * HEAD_DIM)
                q = q_ref[pl.ds(r0, qb), hs]
                scores, values = [], []
                if ctx:
                    scores.append(_dot_nt(q, kc_s[slot]))
                    values.append(vc_s[hd])
                _key_chunks(scores, values, q, k_ref, v_ref, k0, seq_len, hs, vs, chunk=seq_len)
                maps.append(_softmax_pv(scores, values, joint=True)[0])
            d = maps[0] - lam * maps[1]
            r = lax.rsqrt(jnp.mean(d * d, axis=-1, keepdims=True) + EPS)
            y = ((d * r) * out_gain) * (1.0 - lam_init)
            gate = g_ref[pl.ds(r0, qb), vs].astype(F32)
            o_ref[pl.ds(r0, qb), vs] = (y * gate).astype(BF16)
        return carry

    lax.fori_loop(0, n_blocks, body, 0)


def _diff(q, k, v, gate, c_lam, c_on, lam_init, seq_len, qb, rows_per_step, ctx=None, layer_idx=0):
    m_rows = q.shape[0]
    in_specs = [
        pl.BlockSpec((4, HEAD_DIM), lambda b: (0, 0)),
        pl.BlockSpec((1, C_V_DIM), lambda b: (0, 0)),
    ] + [pl.BlockSpec((rows_per_step, HALF_W), lambda b: (b, 0))] * 4
    args = [c_lam, c_on, q, k, v, gate]
    scratch = []
    if ctx is not None:
        past = ctx[0].shape[-2]
        in_specs += [
            pl.BlockSpec((None, None, 2 * C_HEADS, past, HEAD_DIM), lambda b: (b, layer_idx, 0, 0, 0)),
            pl.BlockSpec((None, None, C_HEADS, past, C_V_DIM), lambda b: (b, layer_idx, 0, 0, 0)),
        ]
        args += list(ctx)
        scratch = [pltpu.VMEM((2 * C_HEADS, past, HEAD_DIM), BF16),
                   pltpu.VMEM((C_HEADS, past, C_V_DIM), BF16)]
    return pl.pallas_call(
        functools.partial(_diff_kernel, seq_len=seq_len, qb=qb, n_blocks=rows_per_step // qb,
                          ctx=ctx is not None, lam_init=lam_init),
        grid=(m_rows // rows_per_step,),
        in_specs=in_specs,
        out_specs=pl.BlockSpec((rows_per_step, HALF_W), lambda b: (b, 0)),
        out_shape=jax.ShapeDtypeStruct((m_rows, HALF_W), BF16),
        scratch_shapes=scratch,
        compiler_params=_params(),
        name="diff_attn",
    )(*args)


def _outproj_kernel(*refs, bm, emit_h, conv_seq):
    it = iter(refs)
    y1_ref = next(it)
    if conv_seq is None:
        y2_ref = next(it)
    else:
        u_ref, b_ref, c_ref, g_ref = next(it), next(it), next(it), next(it)
        up_ref, cp_ref, un_ref, cn_ref = next(it), next(it), next(it), next(it)
        cw_ref = next(it)
    x_ref, w_ref, gt_ref = next(it), next(it), next(it)
    if emit_h:
        gn_ref, sh_ref, sc_ref = next(it), next(it), next(it)
    xo_ref = next(it)
    if emit_h:
        ho_ref = next(it)
    chunks = [slice(c, c + PROJ_CHUNK) for c in range(0, D_MODEL, PROJ_CHUNK)]
    if conv_seq is not None:
        assert bm % conv_seq == 0 or conv_seq % bm == 0
        row = lax.broadcasted_iota(jnp.int32, (bm, 1), 0)
        pos = (pl.program_id(0) * bm + row) % conv_seq
        first_row = jnp.logical_or(pos == 0, row == 0)
        last_row = jnp.logical_or(pos == conv_seq - 1, row == bm - 1)
        tile_pos = (pl.program_id(0) * bm) % conv_seq
        use_before = jnp.where(tile_pos == 0, 0.0, 1.0)
        use_after = jnp.where(tile_pos + bm == conv_seq, 0.0, 1.0) if conv_seq % bm == 0 else 0.0
        lanes = HALF_W // len(chunks)
        y2_parts = []
        for ci, cs in enumerate(chunks):
            ls = slice(ci * lanes, (ci + 1) * lanes)
            z = c_ref[:, ls].astype(F32) * u_ref[:, ls].astype(F32)
            z_before = (cp_ref[:, ls].astype(F32) * up_ref[:, ls].astype(F32))[HALO_ROWS - 1:HALO_ROWS] * use_before
            z_after = (cn_ref[:, ls].astype(F32) * un_ref[:, ls].astype(F32))[0:1] * use_after
            z_prev = jnp.where(first_row, z_before, pltpu.roll(z, 1, axis=0))
            z_next = jnp.where(last_row, z_after, pltpu.roll(z, bm - 1, axis=0))
            conv = z_prev * cw_ref[0:1, ls] + z * cw_ref[1:2, ls] + z_next * cw_ref[2:3, ls]
            y2_parts.append(((b_ref[:, ls].astype(F32) * conv) * g_ref[:, ls].astype(F32)).astype(BF16))
            xo_ref[:, cs] = x_ref[:, cs] + gt_ref[:, cs] * _dot(y1_ref[...], w_ref[0:HALF_W, cs])
        y2 = jnp.concatenate(y2_parts, axis=1)
    ssq = jnp.zeros((bm, 1), F32)
    for cs in chunks:
        if conv_seq is None:
            acc = _dot(y1_ref[...], w_ref[0:HALF_W, cs]) + _dot(y2_ref[...], w_ref[HALF_W:D_MODEL, cs])
            xn = x_ref[:, cs] + gt_ref[:, cs] * acc
        else:
            xn = xo_ref[:, cs] + gt_ref[:, cs] * _dot(y2, w_ref[HALF_W:D_MODEL, cs])
        xo_ref[:, cs] = xn
        if emit_h:
            ssq = ssq + jnp.sum(xn * xn, axis=-1, keepdims=True)
    if emit_h:
        r = lax.rsqrt(ssq * (1.0 / D_MODEL) + EPS)
        for cs in chunks:
            gain = gn_ref[:, cs] * (1.0 + sc_ref[:, cs])
            ho_ref[:, cs] = ((xo_ref[:, cs] * r) * gain + sh_ref[:, cs]).astype(BF16)


def _outproj(y1, y2, x, w_out, norm_g3, mod4, layer, row_fn, emit_h, conv=None):
    m_rows = x.shape[0]
    bm = PROJ_ROWS
    half_spec = pl.BlockSpec((bm, HALF_W), lambda m: (m, 0))
    in_specs = [half_spec]
    args = [y1]
    scratch = []
    if conv is None:
        in_specs.append(half_spec)
        args.append(y2)
    else:
        u, bgate, cgate, gate, conv_w, _ = conv
        halo_blocks = bm // HALO_ROWS
        last_halo = m_rows // HALO_ROWS - 1
        before = pl.BlockSpec((HALO_ROWS, HALF_W), lambda m: (jnp.maximum(m * halo_blocks - 1, 0), 0))
        after = pl.BlockSpec((HALO_ROWS, HALF_W), lambda m: (jnp.minimum((m + 1) * halo_blocks, last_halo), 0))
        in_specs += [half_spec] * 4 + [before, before, after, after,
                                       pl.BlockSpec((3, HALF_W), lambda m: (0, 0))]
        args += [u, bgate, cgate, gate, u, cgate, u, cgate, conv_w]
    in_specs += [
        pl.BlockSpec((bm, D_MODEL), lambda m: (m, 0)),
        pl.BlockSpec((D_MODEL, D_MODEL), lambda m: (0, 0), pipeline_mode=pl.Buffered(1)),
        _mod_spec(layer, 2, row_fn),
    ]
    args += [x, w_out, mod4]
    out_specs = [pl.BlockSpec((bm, D_MODEL), lambda m: (m, 0))]
    out_shape = [jax.ShapeDtypeStruct((m_rows, D_MODEL), F32)]
    if emit_h:
        in_specs += [
            pl.BlockSpec((None, 1, D_MODEL), lambda m: (layer + 1, 0, 0)),
            _mod_spec(layer + 1, 0, row_fn),
            _mod_spec(layer + 1, 1, row_fn),
        ]
        args += [norm_g3, mod4, mod4]
        out_specs.append(pl.BlockSpec((bm, D_MODEL), lambda m: (m, 0)))
        out_shape.append(jax.ShapeDtypeStruct((m_rows, D_MODEL), BF16))
    outs = pl.pallas_call(
        functools.partial(_outproj_kernel, bm=bm, emit_h=emit_h,
                          conv_seq=None if conv is None else conv[5]),
        grid=(m_rows // bm,),
        in_specs=in_specs,
        out_specs=out_specs,
        out_shape=out_shape,
        scratch_shapes=scratch,
        compiler_params=_params(),
        name="outproj",
    )(*args)
    return (outs[0], outs[1]) if emit_h else (outs[0], None)


def _rope_tables(n_tokens):
    t = jnp.arange(n_tokens)
    row = t // GRID_W
    col = t % GRID_W
    n_freq = HEAD_DIM // 4
    inv = ROPE_THETA ** (-jnp.arange(n_freq, dtype=F32) / n_freq)
    ang_r = row[:, None] * inv
    ang_c = col[:, None] * inv
    cos = jnp.concatenate([jnp.cos(ang_r)] * 2 + [jnp.cos(ang_c)] * 2, axis=1)
    sin = jnp.concatenate([-jnp.sin(ang_r), jnp.sin(ang_r), -jnp.sin(ang_c), jnp.sin(ang_c)], axis=1)
    return cos, sin


def kernel(x_prompt, x_sample, cache_a_k, cache_a_v, cache_c_k, cache_c_v, cache_d_k, cache_d_v,
           c, c_ctx, norm_g, ada_w, ada_b, ev_w_in, ev_w_out, a_q_norm, a_k_norm, a_sink, b_conv,
           od_w_in, od_w_out, c_q_norm, c_k_norm, c_lambda, c_out_norm, d_q_norm, d_k_norm):
    n_prompt, seq_p, _ = x_prompt.shape
    n_sample, seq_s, _ = x_sample.shape
    past = cache_a_k.shape[-2]
    scale = HEAD_DIM ** -0.5 * LOG2E

    cond = jnp.concatenate(
        [c, c_ctx[None, :], jnp.zeros((COND_ROWS - n_sample - 1, D_MODEL), F32)], axis=0)
    mod4 = _modulation(cond, ada_w, ada_b).reshape(DEPTH, COND_ROWS, 1, 3 * D_MODEL)
    norm_g3 = norm_g.reshape(DEPTH, 1, D_MODEL)
    rope_tabs = _rope_tables(seq_s)

    prompt_row = lambda m: CTX_ROW
    sample_row = lambda m: (m * PROJ_ROWS) // seq_s

    xp = x_prompt.reshape(n_prompt * seq_p, D_MODEL)
    xs = x_sample.reshape(n_sample * seq_s, D_MODEL)
    hp = _prenorm(xp, norm_g3, mod4, 0, prompt_row)
    hs = _prenorm(xs, norm_g3, mod4, 0, sample_row)

    cache_c_k8 = cache_c_k.reshape(n_sample, DEPTH // 2, 2 * C_HEADS, past, HEAD_DIM)
    prompt_rows = PROMPT_SEQS_PER_STEP * seq_p

    new_a = None
    new_c = (None, None, None)
    for layer in range(DEPTH):
        i = layer // 2
        last = layer == DEPTH - 1
        if layer % 2 == 0:
            w_out = ev_w_out[i].astype(BF16)
            slabs = [ev_w_in[i][:, a:b].astype(BF16) for a, b in ((0, 2560), (2560, 4608), (4608, 6656))]
            gq = a_q_norm[i] * scale
            gk = a_k_norm[i]
            kv_cache = (N_KV, HEAD_DIM)

            def project(h, seq_len, tabs, cache, prev=None):
                segs1 = [_Seg("norm", HALF_W), _Seg("norm", N_KV * HEAD_DIM, cache),
                         _Seg("plain", N_KV * HEAD_DIM, cache), _Seg("silu", HALF_W)]
                (q, k, v, ga), caches = _inproj(h, slabs[0], segs1, [gq, gk, None, None], tabs, seq_len,
                                                cache_slot=i, cache_prev=prev)
                (u, bg), _ = _inproj(h, slabs[1], [_Seg("plain", HALF_W)] * 2, [None] * 2, None, seq_len)
                (cg, gb), _ = _inproj(h, slabs[2], [_Seg("plain", HALF_W), _Seg("silu", HALF_W)],
                                      [None] * 2, None, seq_len)
                return q, k, v, ga, u, bg, cg, gb, caches

            q, k, v, ga, u, bg, cg, gb, new_a = project(hp, seq_p, None, kv_cache, new_a)
            y1p = _gqa(q, k, v, ga, seq_p, seq_p, prompt_rows, sink=a_sink[i])
            y2p, conv_p = None, (u, bg, cg, gb, b_conv[i], seq_p)

            q, k, v, ga, u, bg, cg, gb, _ = project(hs, seq_s, rope_tabs, None)
            y1s = _banded(q, k, v, ga, a_sink[i], cache_a_k, cache_a_v, i, seq_s)
            y2s, conv_s = None, (u, bg, cg, gb, b_conv[i], seq_s)
        else:
            lam_init = 0.8 - 0.6 * math.exp(-0.3 * layer)
            w_out = od_w_out[i].astype(BF16)
            slabs = [od_w_in[i][:, a:b].astype(BF16) for a, b in ((0, 2048), (2048, 4096), (4096, 6656))]
            gcq = c_q_norm[i] * scale
            gck = c_k_norm[i]
            gdq = d_q_norm[i] * scale
            gdk = d_k_norm[i]
            c_on = c_out_norm[i].reshape(1, C_V_DIM)

            def project(h, seq_len, tabs, with_cache, prev=(None, None, None)):
                ck_cache = (2 * C_HEADS, HEAD_DIM) if with_cache else None
                cv_cache = (C_HEADS, C_V_DIM) if with_cache else None
                kv_cache = (N_KV, HEAD_DIM) if with_cache else None
                (cq, ck), cc1 = _inproj(h, slabs[0], [_Seg("norm", HALF_W), _Seg("norm", HALF_W, ck_cache)],
                                        [gcq, gck], tabs, seq_len, cache_slot=i, cache_prev=prev[0])
                (cv, cgate), cc2 = _inproj(h, slabs[1], [_Seg("plain", HALF_W, cv_cache), _Seg("silu", HALF_W)],
                                           [None] * 2, None, seq_len, cache_slot=i, cache_prev=prev[1])
                segs3 = [_Seg("norm", HALF_W), _Seg("norm", N_KV * HEAD_DIM, kv_cache),
                         _Seg("plain", N_KV * HEAD_DIM, kv_cache), _Seg("silu", HALF_W)]
                (dq, dk, dv, dgate), cc3 = _inproj(h, slabs[2], segs3, [gdq, gdk, None, None], tabs, seq_len,
                                                   cache_slot=i, cache_prev=prev[2])
                return cq, ck, cv, cgate, dq, dk, dv, dgate, (cc1, cc2, cc3)

            cq, ck, cv, cgate, dq, dk, dv, dgate, new_c = project(hp, seq_p, None, True, new_c)
            y1p = _diff(cq, ck, cv, cgate, c_lambda[i], c_on, lam_init, seq_p, seq_p, prompt_rows)
            y2p = _gqa(dq, dk, dv, dgate, seq_p, seq_p, prompt_rows)

            cq, ck, cv, cgate, dq, dk, dv, dgate, _ = project(hs, seq_s, rope_tabs, False)
            y1s = _diff(cq, ck, cv, cgate, c_lambda[i], c_on, lam_init, seq_s, 2 * SAMPLE_QB, seq_s,
                        ctx=(cache_c_k8, cache_c_v), layer_idx=i)
            y2s = _gqa(dq, dk, dv, dgate, seq_s, 2 * SAMPLE_QB, seq_s,
                       ctx=(cache_d_k, cache_d_v), layer_idx=i)
            conv_p = conv_s = None

        xp, hp = _outproj(y1p, y2p, xp, w_out, norm_g3, mod4, layer, prompt_row, not last, conv_p)
        xs, hs = _outproj(y1s, y2s, xs, w_out, norm_g3, mod4, layer, sample_row, not last, conv_s)

    return (xp.reshape(n_prompt, seq_p, D_MODEL), xs.reshape(n_sample, seq_s, D_MODEL),
            new_a[0], new_a[1],
            new_c[0][0].reshape(n_prompt, CACHE_SLOTS, 2, C_HEADS, seq_p, HEAD_DIM), new_c[1][0],
            new_c[2][0], new_c[2][1])
```

```python
import functools
import math

import jax
import jax.numpy as jnp
from jax import lax
from jax.experimental import pallas as pl
from jax.experimental.pallas import tpu as pltpu

D_MODEL = 2048
DEPTH = 4
GRID_W = 64
HEAD_DIM = 128
WINDOW = 128
ROPE_THETA = 10000.0
EPS = 1e-6
NEG_INF = -1e30
HALF_W = D_MODEL // 2
N_KV = 2
GROUP = 4
C_HEADS = 4
C_V_DIM = 2 * HEAD_DIM
CACHE_SLOTS = DEPTH // 2
COND_ROWS = 16
CTX_ROW = 8

BF16 = jnp.bfloat16
F32 = jnp.float32

VMEM_LIMIT_BYTES = 52 * 1024 * 1024
PROJ_ROWS = 512
INPROJ_ROWS = 1024
PROJ_CHUNK = 512
SAMPLE_QB = 128
KEY_CHUNK = 512
UNROLL_SELF = 4
UNROLL_CTX = 2
HALO_ROWS = 16
LOG2E = math.log2(math.e)
PROMPT_SEQS_PER_STEP = 4


def _params(n_axes=1, flags=None):
    return pltpu.CompilerParams(
        dimension_semantics=("arbitrary",) * n_axes,
        vmem_limit_bytes=VMEM_LIMIT_BYTES,
        flags=flags,
    )


def _dot(a, b):
    return jnp.dot(a, b, preferred_element_type=F32)


def _dot_nt(a, b):
    return lax.dot_general(a, b, (((1,), (1,)), ((), ())), preferred_element_type=F32)


def _silu(x):
    return x / (1.0 + jnp.exp(-x))


def _mod_kernel(cond_ref, w_ref, b_ref, o_ref):
    s = _silu(cond_ref[...]).astype(BF16)
    o_ref[...] = _dot(s, w_ref[...].astype(BF16)) + b_ref[...]


def _modulation(cond, ada_w, ada_b):
    bn = 1024
    n_out = 3 * D_MODEL
    return pl.pallas_call(
        _mod_kernel,
        grid=(DEPTH, n_out // bn),
        in_specs=[
            pl.BlockSpec((COND_ROWS, D_MODEL), lambda l, n: (0, 0)),
            pl.BlockSpec((None, D_MODEL, bn), lambda l, n: (l, 0, n)),
            pl.BlockSpec((None, 1, bn), lambda l, n: (l, 0, n)),
        ],
        out_specs=pl.BlockSpec((None, COND_ROWS, bn), lambda l, n: (l, 0, n)),
        out_shape=jax.ShapeDtypeStruct((DEPTH, COND_ROWS, n_out), F32),
        compiler_params=_params(2),
        name="modulation",
    )(cond, ada_w, ada_b.reshape(DEPTH, 1, n_out))


def _mod_spec(layer, part, row_fn):
    return pl.BlockSpec((None, None, 1, D_MODEL), lambda m: (layer, row_fn(m), 0, part))


def _norm_mod(x, r, g, sc, sh):
    return ((x * r) * g) * (1.0 + sc) + sh


def _prenorm_kernel(x_ref, g_ref, sh_ref, sc_ref, h_ref):
    x = x_ref[...]
    r = lax.rsqrt(jnp.mean(x * x, axis=-1, keepdims=True) + EPS)
    h_ref[...] = _norm_mod(x, r, g_ref[...], sc_ref[...], sh_ref[...]).astype(BF16)


def _prenorm(x, norm_g3, mod4, layer, row_fn):
    m_rows = x.shape[0]
    bm = PROJ_ROWS
    return pl.pallas_call(
        _prenorm_kernel,
        grid=(m_rows // bm,),
        in_specs=[
            pl.BlockSpec((bm, D_MODEL), lambda m: (m, 0)),
            pl.BlockSpec((None, 1, D_MODEL), lambda m: (layer, 0, 0)),
            _mod_spec(layer, 0, row_fn),
            _mod_spec(layer, 1, row_fn),
        ],
        out_specs=pl.BlockSpec((bm, D_MODEL), lambda m: (m, 0)),
        out_shape=jax.ShapeDtypeStruct((m_rows, D_MODEL), BF16),
        compiler_params=_params(),
        name="prenorm",
    )(x, norm_g3, mod4, mod4)


class _Seg:
    def __init__(self, kind, width, cache=None):
        self.kind = kind
        self.width = width
        self.cache = cache


def _swap_halves(a, first_half):
    return jnp.where(first_half, pltpu.roll(a, 96, axis=1), pltpu.roll(a, 32, axis=1))


def _inproj_kernel(*refs, segs, rope, seq_len, bm, n_prev, first_fill, n_wtiles):
    it = iter(refs)
    h_ref = next(it)
    w_refs = [next(it) for _ in range(n_wtiles)]
    gain_refs = [tuple(next(it) for _ in range(2 if rope else 1)) if s.kind == "norm" else None
                 for s in segs]
    for _ in range(n_prev):
        next(it)
    out_refs = [next(it) for _ in segs]
    cache_refs = [next(it) if s.cache else None for s in segs]

    if rope:
        lane = lax.broadcasted_iota(jnp.int32, (bm, HEAD_DIM), 1)
        first_half = (lane % 64) < 32

    chunks = []
    col = 0
    for si, seg in enumerate(segs):
        for c0 in range(0, seg.width, PROJ_CHUNK):
            chunks.append((si, c0, min(PROJ_CHUNK, seg.width - c0), col + c0))
        col += seg.width

    def matmul(chunk):
        _, _, cw, wc = chunk
        off = wc % PROJ_CHUNK
        return _dot(h_ref[...], w_refs[wc // PROJ_CHUNK][:, off:off + cw])

    def epilogue(chunk, acc):
        si, c0, cw, _ = chunk
        seg, gain_ref, out_ref, cache_ref = segs[si], gain_refs[si], out_refs[si], cache_refs[si]
        if seg.kind == "norm":
            heads = []
            for hh in range(cw // HEAD_DIM):
                a = acc[:, hh * HEAD_DIM:(hh + 1) * HEAD_DIM]
                r = lax.rsqrt(jnp.mean(a * a, axis=-1, keepdims=True) + EPS)
                if rope:
                    y = (a * gain_ref[0][...] + _swap_halves(a, first_half) * gain_ref[1][...]) * r
                else:
                    y = (a * r) * gain_ref[0][...]
                heads.append(y)
            val = jnp.concatenate(heads, axis=1) if len(heads) > 1 else heads[0]
        elif seg.kind == "silu":
            val = _silu(acc)
        else:
            val = acc
        out_ref[:, c0:c0 + cw] = val.astype(BF16)
        if seg.cache:
            _, hw = seg.cache
            for j in range(bm // seq_len):
                for hh in range(cw // hw):
                    head = (c0 // hw) + hh
                    data = val[j * seq_len:(j + 1) * seq_len, hh * hw:(hh + 1) * hw]
                    if first_fill is None:
                        cache_ref[j, head] = data
                    else:
                        for slot in range(CACHE_SLOTS):
                            cache_ref[j, slot, head] = data if slot == first_fill else jnp.zeros_like(data)

    acc = matmul(chunks[0])
    for idx, chunk in enumerate(chunks):
        nxt = matmul(chunks[idx + 1]) if idx + 1 < len(chunks) else None
        epilogue(chunk, acc)
        acc = nxt


def _weight_tile_specs(weights):
    tiles, layer, first, n = weights
    specs = [pl.BlockSpec((None, None, D_MODEL, PROJ_CHUNK), lambda m, t=first + t: (layer, t, 0, 0),
                          pipeline_mode=pl.Buffered(1)) for t in range(n)]
    return specs, [tiles] * n


def _inproj(h, weights, segs, gains, rope_tabs, seq_len, cache_slot=0, cache_prev=None):
    m_rows = h.shape[0]
    bm = INPROJ_ROWS
    rope = rope_tabs is not None
    w_specs, w_args = _weight_tile_specs(weights)
    assert sum(s.width for s in segs) == len(w_args) * PROJ_CHUNK
    in_specs = [pl.BlockSpec((bm, D_MODEL), lambda m: (m, 0))] + w_specs
    args = [h] + w_args
    if rope:
        cos, sin_signed = rope_tabs
        blocks_per_seq = seq_len // bm
        tab_spec = pl.BlockSpec((bm, HEAD_DIM), lambda m: (m % blocks_per_seq, 0))
    for seg, g in zip(segs, gains):
        if seg.kind != "norm":
            continue
        if rope:
            g_swapped = g.reshape(2, 2, HEAD_DIM // 4)[:, ::-1].reshape(HEAD_DIM)
            in_specs += [tab_spec, tab_spec]
            args += [cos * g[None, :], sin_signed * g_swapped[None, :]]
        else:
            in_specs.append(pl.BlockSpec((1, HEAD_DIM), lambda m: (0, 0)))
            args.append(g.reshape(1, HEAD_DIM))
    n_prev = 0 if cache_prev is None else len(cache_prev)
    aliases = {}
    for k in range(n_prev):
        aliases[len(args)] = len(segs) + k
        in_specs.append(pl.BlockSpec(memory_space=pl.ANY))
        args.append(cache_prev[k])
    out_specs = [pl.BlockSpec((bm, s.width), lambda m: (m, 0)) for s in segs]
    out_shape = [jax.ShapeDtypeStruct((m_rows, s.width), BF16) for s in segs]
    for s in segs:
        if s.cache:
            nh, hw = s.cache
            if cache_prev is None:
                out_specs.append(pl.BlockSpec((bm // seq_len, CACHE_SLOTS, nh, seq_len, hw),
                                              lambda m: (m, 0, 0, 0, 0)))
            else:
                out_specs.append(pl.BlockSpec((bm // seq_len, None, nh, seq_len, hw),
                                              lambda m: (m, cache_slot, 0, 0, 0)))
            out_shape.append(jax.ShapeDtypeStruct((m_rows // seq_len, CACHE_SLOTS, nh, seq_len, hw), F32))
    outs = pl.pallas_call(
        functools.partial(_inproj_kernel, segs=segs, rope=rope, seq_len=seq_len, bm=bm, n_prev=n_prev,
                          first_fill=cache_slot if cache_prev is None else None, n_wtiles=len(w_args)),
        grid=(m_rows // bm,),
        in_specs=in_specs,
        out_specs=out_specs,
        out_shape=out_shape,
        input_output_aliases=aliases,
        compiler_params=_params(),
        name="inproj",
    )(*args)
    n = len(segs)
    return list(outs[:n]), list(outs[n:])


def _softmax_pv(scores, values, sinks=None, joint=False):
    groups = 1 if sinks is None else len(sinks)
    rows = scores[0].shape[0] // groups
    if joint or len(scores) == 1:
        probs = [[] for _ in scores]
        inv = []
        for g in range(groups):
            rs = slice(g * rows, (g + 1) * rows)
            m = jnp.max(scores[0][rs], axis=-1, keepdims=True)
            for s in scores[1:]:
                m = jnp.maximum(m, jnp.max(s[rs], axis=-1, keepdims=True))
            l = None
            if sinks is not None:
                m = jnp.maximum(m, sinks[g])
                l = jnp.exp2(sinks[g] - m)
            for i, s in enumerate(scores):
                e = jnp.exp2(s[rs] - m)
                part = jnp.sum(e, axis=-1, keepdims=True)
                l = part if l is None else l + part
                probs[i].append(e.astype(BF16))
            inv.append(1.0 / l)
        o = None
        for p, v in zip(probs, values):
            pv = _dot(p[0] if groups == 1 else jnp.concatenate(p, axis=0), v)
            o = pv if o is None else o + pv
        return [o[g * rows:(g + 1) * rows] * inv[g] for g in range(groups)]
    parts = []
    for s, v in zip(scores, values):
        m = jnp.max(s, axis=-1, keepdims=True)
        e = jnp.exp2(s - m)
        parts.append((m, jnp.sum(e, axis=-1, keepdims=True), _dot(e.astype(BF16), v)))
    outs = []
    for g in range(groups):
        rs = slice(g * rows, (g + 1) * rows)
        m_all = parts[0][0][rs]
        for m, _, _ in parts[1:]:
            m_all = jnp.maximum(m_all, m[rs])
        den = None
        if sinks is not None:
            m_all = jnp.maximum(m_all, sinks[g])
            den = jnp.exp2(sinks[g] - m_all)
        num = None
        for m, l, o in parts:
            w = jnp.exp2(m[rs] - m_all)
            den = l[rs] * w if den is None else den + l[rs] * w
            num = o[rs] * w if num is None else num + o[rs] * w
        outs.append(num * (1.0 / den))
    return outs


def _stack_heads(ref, r0, rows, first_head):
    return jnp.concatenate(
        [ref[pl.ds(r0, rows), (first_head + g) * HEAD_DIM:(first_head + g + 1) * HEAD_DIM]
         for g in range(GROUP)], axis=0)


def _head_sinks(sink_ref, first_head):
    return [sink_ref[first_head + g] * LOG2E for g in range(GROUP)]


def _key_chunks(scores, values, q, k_ref, v_ref, k0, seq_len, cols_k, cols_v, chunk=KEY_CHUNK):
    step = min(seq_len, chunk)
    for c in range(0, seq_len, step):
        scores.append(_dot_nt(q, k_ref[pl.ds(k0 + c, step), cols_k]))
        values.append(v_ref[pl.ds(k0 + c, step), cols_v])


def _store_gated(o_ref, g_ref, o, r0, rows, first_head):
    if len(o) == 1:
        o = [o[0][g * rows:(g + 1) * rows] for g in range(GROUP)]
    for g in range(GROUP):
        cs = slice((first_head + g) * HEAD_DIM, (first_head + g + 1) * HEAD_DIM)
        gate = g_ref[pl.ds(r0, rows), cs].astype(F32)
        o_ref[pl.ds(r0, rows), cs] = (o[g] * gate).astype(BF16)


def _gqa_kernel(*refs, seq_len, qb, n_blocks, ctx, sink):
    it = iter(refs)
    sink_ref = next(it) if sink else None
    q_ref, k_ref, v_ref, g_ref = next(it), next(it), next(it), next(it)
    if ctx:
        ck_ref, cv_ref = next(it), next(it)
    o_ref = next(it)
    if ctx:
        kc_s, vc_s = next(it), next(it)
        for kv in range(N_KV):
            kc_s[kv] = ck_ref[kv].astype(BF16)
            vc_s[kv] = cv_ref[kv].astype(BF16)
    blocks_per_seq = seq_len // qb

    def body(n, carry):
        r0 = pl.multiple_of(n * qb, qb)
        k0 = pl.multiple_of((n // blocks_per_seq) * seq_len, seq_len)
        for kv in range(N_KV):
            hs = slice(kv * HEAD_DIM, (kv + 1) * HEAD_DIM)
            q = _stack_heads(q_ref, r0, qb, kv * GROUP)
            scores, values = [], []
            if ctx:
                scores.append(_dot_nt(q, kc_s[kv]))
                values.append(vc_s[kv])
            _key_chunks(scores, values, q, k_ref, v_ref, k0, seq_len, hs, hs)
            sinks = _head_sinks(sink_ref, kv * GROUP) if sink else None
            o = _softmax_pv(scores, values, sinks)
            _store_gated(o_ref, g_ref, o, r0, qb, kv * GROUP)
        return carry

    lax.fori_loop(0, n_blocks, body, 0, unroll=UNROLL_CTX if ctx else UNROLL_SELF)


def _gqa(q, k, v, gate, seq_len, qb, rows_per_step, sink=None, ctx=None, layer_idx=0):
    m_rows = q.shape[0]
    kvw = N_KV * HEAD_DIM
    in_specs, args = [], []
    if sink is not None:
        in_specs.append(pl.BlockSpec(memory_space=pltpu.SMEM))
        args.append(sink)
    in_specs += [
        pl.BlockSpec((rows_per_step, HALF_W), lambda b: (b, 0)),
        pl.BlockSpec((rows_per_step, kvw), lambda b: (b, 0)),
        pl.BlockSpec((rows_per_step, kvw), lambda b: (b, 0)),
        pl.BlockSpec((rows_per_step, HALF_W), lambda b: (b, 0)),
    ]
    args += [q, k, v, gate]
    scratch = []
    if ctx is not None:
        past = ctx[0].shape[-2]
        cspec = pl.BlockSpec((None, None, N_KV, past, HEAD_DIM), lambda b: (b, layer_idx, 0, 0, 0))
        in_specs += [cspec, cspec]
        args += list(ctx)
        scratch = [pltpu.VMEM((N_KV, past, HEAD_DIM), BF16)] * 2
    return pl.pallas_call(
        functools.partial(_gqa_kernel, seq_len=seq_len, qb=qb, n_blocks=rows_per_step // qb,
                          ctx=ctx is not None, sink=sink is not None),
        grid=(m_rows // rows_per_step,),
        in_specs=in_specs,
        out_specs=pl.BlockSpec((rows_per_step, HALF_W), lambda b: (b, 0)),
        out_shape=jax.ShapeDtypeStruct((m_rows, HALF_W), BF16),
        scratch_shapes=scratch,
        compiler_params=_params(),
        name="gqa",
    )(*args)


def _banded_kernel(sink_ref, q_ref, k_ref, v_ref, g_ref, ck_ref, cv_ref, o_ref,
                   kpad, vpad, kc_s, vc_s, *, seq_len):
    n_blocks = seq_len // WINDOW
    zeros = jnp.zeros((WINDOW, HEAD_DIM), BF16)
    for kv in range(N_KV):
        hs = slice(kv * HEAD_DIM, (kv + 1) * HEAD_DIM)
        kpad[kv, 0:WINDOW] = zeros
        vpad[kv, 0:WINDOW] = zeros
        kpad[kv, WINDOW + seq_len:2 * WINDOW + seq_len] = zeros
        vpad[kv, WINDOW + seq_len:2 * WINDOW + seq_len] = zeros
        kpad[kv, WINDOW:WINDOW + seq_len] = k_ref[:, hs]
        vpad[kv, WINDOW:WINDOW + seq_len] = v_ref[:, hs]
        kc_s[kv] = ck_ref[kv].astype(BF16)
        vc_s[kv] = cv_ref[kv].astype(BF16)

    rows = GROUP * WINDOW
    qi = lax.broadcasted_iota(jnp.int32, (rows, 3 * WINDOW), 0) % WINDOW
    col = lax.broadcasted_iota(jnp.int32, (rows, 3 * WINDOW), 1)
    kj = col % WINDOW
    blk = col // WINDOW
    never = 4 * WINDOW

    def body(n, carry):
        r0 = pl.multiple_of(n * WINDOW, WINDOW)
        lo = jnp.where(n > 0, 0, never)
        hi = jnp.where(n < n_blocks - 1, 0, never)
        slack = jnp.where(blk == 0, kj - qi - lo, jnp.where(blk == 2, qi - kj - hi, 0))
        valid = slack >= 0
        for kv in range(N_KV):
            q = _stack_heads(q_ref, r0, WINDOW, kv * GROUP)
            kb = kpad[kv, pl.ds(r0, 3 * WINDOW), :]
            vb = vpad[kv, pl.ds(r0, 3 * WINDOW), :]
            s_ctx = _dot_nt(q, kc_s[kv])
            s_band = jnp.where(valid, _dot_nt(q, kb), NEG_INF)
            o = _softmax_pv([s_ctx, s_band], [vc_s[kv], vb], _head_sinks(sink_ref, kv * GROUP))
            _store_gated(o_ref, g_ref, o, r0, WINDOW, kv * GROUP)
        return carry

    lax.fori_loop(0, n_blocks, body, 0, unroll=UNROLL_CTX)


def _banded(q, k, v, gate, sink, cache_k, cache_v, layer_idx, seq_len):
    m_rows = q.shape[0]
    kvw = N_KV * HEAD_DIM
    past = cache_k.shape[-2]
    cspec = pl.BlockSpec((None, None, N_KV, past, HEAD_DIM), lambda b: (b, layer_idx, 0, 0, 0))
    pad_rows = seq_len + 2 * WINDOW
    return pl.pallas_call(
        functools.partial(_banded_kernel, seq_len=seq_len),
        grid=(m_rows // seq_len,),
        in_specs=[
            pl.BlockSpec(memory_space=pltpu.SMEM),
            pl.BlockSpec((seq_len, HALF_W), lambda b: (b, 0)),
            pl.BlockSpec((seq_len, kvw), lambda b: (b, 0)),
            pl.BlockSpec((seq_len, kvw), lambda b: (b, 0)),
            pl.BlockSpec((seq_len, HALF_W), lambda b: (b, 0)),
            cspec, cspec,
        ],
        out_specs=pl.BlockSpec((seq_len, HALF_W), lambda b: (b, 0)),
        out_shape=jax.ShapeDtypeStruct((m_rows, HALF_W), BF16),
        scratch_shapes=[
            pltpu.VMEM((N_KV, pad_rows, HEAD_DIM), BF16),
            pltpu.VMEM((N_KV, pad_rows, HEAD_DIM), BF16),
            pltpu.VMEM((N_KV, past, HEAD_DIM), BF16),
            pltpu.VMEM((N_KV, past, HEAD_DIM), BF16),
        ],
        compiler_params=_params(),
        name="banded",
    )(sink, q, k, v, gate, cache_k, cache_v)


def _diff_kernel(*refs, seq_len, qb, n_blocks, ctx, lam_init):
    it = iter(refs)
    lam_ref, on_ref = next(it), next(it)
    q_ref, k_ref, v_ref, g_ref = next(it), next(it), next(it), next(it)
    if ctx:
        ck_ref, cv_ref = next(it), next(it)
    o_ref = next(it)
    if ctx:
        kc_s, vc_s = next(it), next(it)
        for slot in range(2 * C_HEADS):
            kc_s[slot] = ck_ref[slot].astype(BF16)
        for hd in range(C_HEADS):
            vc_s[hd] = cv_ref[hd].astype(BF16)
    lf = lam_ref[...]
    lam = (jnp.exp(jnp.sum(lf[0:1] * lf[1:2], axis=-1, keepdims=True))
           - jnp.exp(jnp.sum(lf[2:3] * lf[3:4], axis=-1, keepdims=True)) + lam_init)
    out_gain = on_ref[...]
    blocks_per_seq = seq_len // qb

    def body(n, carry):
        r0 = pl.multiple_of(n * qb, qb)
        k0 = pl.multiple_of((n // blocks_per_seq) * seq_len, seq_len)
        for hd in range(C_HEADS):
            vs = slice(hd * C_V_DIM, (hd + 1) * C_V_DIM)
            maps = []
            for mp in range(2):
                slot = mp * C_HEADS + hd
                hs = slice(slot * HEAD_DIM, (slot + 1) * HEAD_DIM)
                q = q_ref[pl.ds(r0, qb), hs]
                scores, values = [], []
                if ctx:
                    scores.append(_dot_nt(q, kc_s[slot]))
                    values.append(vc_s[hd])
                _key_chunks(scores, values, q, k_ref, v_ref, k0, seq_len, hs, vs, chunk=seq_len)
                maps.append(_softmax_pv(scores, values, joint=True)[0])
            d = maps[0] - lam * maps[1]
            r = lax.rsqrt(jnp.mean(d * d, axis=-1, keepdims=True) + EPS)
            y = ((d * r) * out_gain) * (1.0 - lam_init)
            gate = g_ref[pl.ds(r0, qb), vs].astype(F32)
            o_ref[pl.ds(r0, qb), vs] = (y * gate).astype(BF16)
        return carry

    lax.fori_loop(0, n_blocks, body, 0, unroll=UNROLL_CTX)


def _diff(q, k, v, gate, c_lam, c_on, lam_init, seq_len, qb, rows_per_step, ctx=None, layer_idx=0):
    m_rows = q.shape[0]
    in_specs = [
        pl.BlockSpec((4, HEAD_DIM), lambda b: (0, 0)),
        pl.BlockSpec((1, C_V_DIM), lambda b: (0, 0)),
    ] + [pl.BlockSpec((rows_per_step, HALF_W), lambda b: (b, 0))] * 4
    args = [c_lam, c_on, q, k, v, gate]
    scratch = []
    if ctx is not None:
        past = ctx[0].shape[-2]
        in_specs += [
            pl.BlockSpec((None, None, 2 * C_HEADS, past, HEAD_DIM), lambda b: (b, layer_idx, 0, 0, 0)),
            pl.BlockSpec((None, None, C_HEADS, past, C_V_DIM), lambda b: (b, layer_idx, 0, 0, 0)),
        ]
        args += list(ctx)
        scratch = [pltpu.VMEM((2 * C_HEADS, past, HEAD_DIM), BF16),
                   pltpu.VMEM((C_HEADS, past, C_V_DIM), BF16)]
    return pl.pallas_call(
        functools.partial(_diff_kernel, seq_len=seq_len, qb=qb, n_blocks=rows_per_step // qb,
                          ctx=ctx is not None, lam_init=lam_init),
        grid=(m_rows // rows_per_step,),
        in_specs=in_specs,
        out_specs=pl.BlockSpec((rows_per_step, HALF_W), lambda b: (b, 0)),
        out_shape=jax.ShapeDtypeStruct((m_rows, HALF_W), BF16),
        scratch_shapes=scratch,
        compiler_params=_params(),
        name="diff_attn",
    )(*args)


def _outproj_kernel(*refs, bm, emit_h, conv_seq):
    it = iter(refs)
    y1_ref = next(it)
    if conv_seq is None:
        y2_ref = next(it)
    else:
        u_ref, b_ref, c_ref, g_ref = next(it), next(it), next(it), next(it)
        up_ref, cp_ref, un_ref, cn_ref = next(it), next(it), next(it), next(it)
        cw_ref = next(it)
    x_ref = next(it)
    w_refs = [next(it) for _ in range(D_MODEL // PROJ_CHUNK)]
    gt_ref = next(it)
    if emit_h:
        gn_ref, sh_ref, sc_ref = next(it), next(it), next(it)
    xo_ref = next(it)
    if emit_h:
        ho_ref = next(it)
    chunks = [slice(c, c + PROJ_CHUNK) for c in range(0, D_MODEL, PROJ_CHUNK)]
    w_top = [w.at[0:HALF_W, :] for w in w_refs]
    w_bot = [w.at[HALF_W:D_MODEL, :] for w in w_refs]
    if conv_seq is not None:
        assert bm % conv_seq == 0 or conv_seq % bm == 0
        row = lax.broadcasted_iota(jnp.int32, (bm, 1), 0)
        pos = (pl.program_id(0) * bm + row) % conv_seq
        first_row = jnp.logical_or(pos == 0, row == 0)
        last_row = jnp.logical_or(pos == conv_seq - 1, row == bm - 1)
        tile_pos = (pl.program_id(0) * bm) % conv_seq
        use_before = jnp.where(tile_pos == 0, 0.0, 1.0)
        use_after = jnp.where(tile_pos + bm == conv_seq, 0.0, 1.0) if conv_seq % bm == 0 else 0.0
        lanes = HALF_W // len(chunks)
        y2_parts = []
        for ci, cs in enumerate(chunks):
            ls = slice(ci * lanes, (ci + 1) * lanes)
            z = c_ref[:, ls].astype(F32) * u_ref[:, ls].astype(F32)
            z_before = (cp_ref[:, ls].astype(F32) * up_ref[:, ls].astype(F32))[HALO_ROWS - 1:HALO_ROWS] * use_before
            z_after = (cn_ref[:, ls].astype(F32) * un_ref[:, ls].astype(F32))[0:1] * use_after
            z_prev = jnp.where(first_row, z_before, pltpu.roll(z, 1, axis=0))
            z_next = jnp.where(last_row, z_after, pltpu.roll(z, bm - 1, axis=0))
            conv = z_prev * cw_ref[0:1, ls] + z * cw_ref[1:2, ls] + z_next * cw_ref[2:3, ls]
            y2_parts.append(((b_ref[:, ls].astype(F32) * conv) * g_ref[:, ls].astype(F32)).astype(BF16))
            xo_ref[:, cs] = x_ref[:, cs] + gt_ref[:, cs] * _dot(y1_ref[...], w_top[ci][...])
        y2 = jnp.concatenate(y2_parts, axis=1)
    ssq = jnp.zeros((bm, 1), F32)
    for ci, cs in enumerate(chunks):
        if conv_seq is None:
            acc = _dot(y1_ref[...], w_top[ci][...]) + _dot(y2_ref[...], w_bot[ci][...])
            xn = x_ref[:, cs] + gt_ref[:, cs] * acc
        else:
            xn = xo_ref[:, cs] + gt_ref[:, cs] * _dot(y2, w_bot[ci][...])
        xo_ref[:, cs] = xn
        if emit_h:
            ssq = ssq + jnp.sum(xn * xn, axis=-1, keepdims=True)
    if emit_h:
        r = lax.rsqrt(ssq * (1.0 / D_MODEL) + EPS)
        for cs in chunks:
            gain = gn_ref[:, cs] * (1.0 + sc_ref[:, cs])
            ho_ref[:, cs] = ((xo_ref[:, cs] * r) * gain + sh_ref[:, cs]).astype(BF16)


def _outproj(y1, y2, x, w_out, norm_g3, mod4, layer, row_fn, emit_h, conv=None):
    m_rows = x.shape[0]
    bm = PROJ_ROWS
    half_spec = pl.BlockSpec((bm, HALF_W), lambda m: (m, 0))
    in_specs = [half_spec]
    args = [y1]
    scratch = []
    if conv is None:
        in_specs.append(half_spec)
        args.append(y2)
    else:
        u, bgate, cgate, gate, conv_w, _ = conv
        halo_blocks = bm // HALO_ROWS
        last_halo = m_rows // HALO_ROWS - 1
        before = pl.BlockSpec((HALO_ROWS, HALF_W), lambda m: (jnp.maximum(m * halo_blocks - 1, 0), 0))
        after = pl.BlockSpec((HALO_ROWS, HALF_W), lambda m: (jnp.minimum((m + 1) * halo_blocks, last_halo), 0))
        in_specs += [half_spec] * 4 + [before, before, after, after,
                                       pl.BlockSpec((3, HALF_W), lambda m: (0, 0))]
        args += [u, bgate, cgate, gate, u, cgate, u, cgate, conv_w]
    w_specs, w_args = _weight_tile_specs(w_out)
    in_specs += [pl.BlockSpec((bm, D_MODEL), lambda m: (m, 0))] + w_specs + [_mod_spec(layer, 2, row_fn)]
    args += [x] + w_args + [mod4]
    out_specs = [pl.BlockSpec((bm, D_MODEL), lambda m: (m, 0))]
    out_shape = [jax.ShapeDtypeStruct((m_rows, D_MODEL), F32)]
    if emit_h:
        in_specs += [
            pl.BlockSpec((None, 1, D_MODEL), lambda m: (layer + 1, 0, 0)),
            _mod_spec(layer + 1, 0, row_fn),
            _mod_spec(layer + 1, 1, row_fn),
        ]
        args += [norm_g3, mod4, mod4]
        out_specs.append(pl.BlockSpec((bm, D_MODEL), lambda m: (m, 0)))
        out_shape.append(jax.ShapeDtypeStruct((m_rows, D_MODEL), BF16))
    outs = pl.pallas_call(
        functools.partial(_outproj_kernel, bm=bm, emit_h=emit_h,
                          conv_seq=None if conv is None else conv[5]),
        grid=(m_rows // bm,),
        in_specs=in_specs,
        out_specs=out_specs,
        out_shape=out_shape,
        scratch_shapes=scratch,
        compiler_params=_params(),
        name="outproj",
    )(*args)
    return (outs[0], outs[1]) if emit_h else (outs[0], None)


def _rope_tables(n_tokens):
    t = jnp.arange(n_tokens)
    row = t // GRID_W
    col = t % GRID_W
    n_freq = HEAD_DIM // 4
    inv = ROPE_THETA ** (-jnp.arange(n_freq, dtype=F32) / n_freq)
    ang_r = row[:, None] * inv
    ang_c = col[:, None] * inv
    cos = jnp.concatenate([jnp.cos(ang_r)] * 2 + [jnp.cos(ang_c)] * 2, axis=1)
    sin = jnp.concatenate([-jnp.sin(ang_r), jnp.sin(ang_r), -jnp.sin(ang_c), jnp.sin(ang_c)], axis=1)
    return cos, sin


def _tile_major(w):
    layers, d, n = w.shape
    return w.astype(BF16).reshape(layers, d, n // PROJ_CHUNK, PROJ_CHUNK).transpose(0, 2, 1, 3)


def kernel(x_prompt, x_sample, cache_a_k, cache_a_v, cache_c_k, cache_c_v, cache_d_k, cache_d_v,
           c, c_ctx, norm_g, ada_w, ada_b, ev_w_in, ev_w_out, a_q_norm, a_k_norm, a_sink, b_conv,
           od_w_in, od_w_out, c_q_norm, c_k_norm, c_lambda, c_out_norm, d_q_norm, d_k_norm):
    n_prompt, seq_p, _ = x_prompt.shape
    n_sample, seq_s, _ = x_sample.shape
    past = cache_a_k.shape[-2]
    scale = HEAD_DIM ** -0.5 * LOG2E

    cond = jnp.concatenate(
        [c, c_ctx[None, :], jnp.zeros((COND_ROWS - n_sample - 1, D_MODEL), F32)], axis=0)
    mod4 = _modulation(cond, ada_w, ada_b).reshape(DEPTH, COND_ROWS, 1, 3 * D_MODEL)
    norm_g3 = norm_g.reshape(DEPTH, 1, D_MODEL)
    rope_tabs = _rope_tables(seq_s)

    prompt_row = lambda m: CTX_ROW
    sample_row = lambda m: (m * PROJ_ROWS) // seq_s

    xp = x_prompt.reshape(n_prompt * seq_p, D_MODEL)
    xs = x_sample.reshape(n_sample * seq_s, D_MODEL)
    hp = _prenorm(xp, norm_g3, mod4, 0, prompt_row)
    hs = _prenorm(xs, norm_g3, mod4, 0, sample_row)

    ev_in_tiles, ev_out_tiles = _tile_major(ev_w_in), _tile_major(ev_w_out)
    od_in_tiles, od_out_tiles = _tile_major(od_w_in), _tile_major(od_w_out)
    cache_c_k8 = cache_c_k.reshape(n_sample, DEPTH // 2, 2 * C_HEADS, past, HEAD_DIM)
    prompt_rows = PROMPT_SEQS_PER_STEP * seq_p

    new_a = None
    new_c = (None, None, None)
    for layer in range(DEPTH):
        i = layer // 2
        last = layer == DEPTH - 1
        if layer % 2 == 0:
            w_out = (ev_out_tiles, i, 0, D_MODEL // PROJ_CHUNK)
            slabs = [(ev_in_tiles, i, first, n) for first, n in ((0, 5), (5, 4), (9, 4))]
            gq = a_q_norm[i] * scale
            gk = a_k_norm[i]
            kv_cache = (N_KV, HEAD_DIM)

            def project(h, seq_len, tabs, cache, prev=None):
                segs1 = [_Seg("norm", HALF_W), _Seg("norm", N_KV * HEAD_DIM, cache),
                         _Seg("plain", N_KV * HEAD_DIM, cache), _Seg("silu", HALF_W)]
                (q, k, v, ga), caches = _inproj(h, slabs[0], segs1, [gq, gk, None, None], tabs, seq_len,
                                                cache_slot=i, cache_prev=prev)
                (u, bg), _ = _inproj(h, slabs[1], [_Seg("plain", HALF_W)] * 2, [None] * 2, None, seq_len)
                (cg, gb), _ = _inproj(h, slabs[2], [_Seg("plain", HALF_W), _Seg("silu", HALF_W)],
                                      [None] * 2, None, seq_len)
                return q, k, v, ga, u, bg, cg, gb, caches

            q, k, v, ga, u, bg, cg, gb, new_a = project(hp, seq_p, None, kv_cache, new_a)
            y1p = _gqa(q, k, v, ga, seq_p, seq_p, prompt_rows, sink=a_sink[i])
            y2p, conv_p = None, (u, bg, cg, gb, b_conv[i], seq_p)

            q, k, v, ga, u, bg, cg, gb, _ = project(hs, seq_s, rope_tabs, None)
            y1s = _banded(q, k, v, ga, a_sink[i], cache_a_k, cache_a_v, i, seq_s)
            y2s, conv_s = None, (u, bg, cg, gb, b_conv[i], seq_s)
        else:
            lam_init = 0.8 - 0.6 * math.exp(-0.3 * layer)
            w_out = (od_out_tiles, i, 0, D_MODEL // PROJ_CHUNK)
            slabs = [(od_in_tiles, i, first, n) for first, n in ((0, 4), (4, 4), (8, 5))]
            gcq = c_q_norm[i] * scale
            gck = c_k_norm[i]
            gdq = d_q_norm[i] * scale
            gdk = d_k_norm[i]
            c_on = c_out_norm[i].reshape(1, C_V_DIM)

            def project(h, seq_len, tabs, with_cache, prev=(None, None, None)):
                ck_cache = (2 * C_HEADS, HEAD_DIM) if with_cache else None
                cv_cache = (C_HEADS, C_V_DIM) if with_cache else None
                kv_cache = (N_KV, HEAD_DIM) if with_cache else None
                (cq, ck), cc1 = _inproj(h, slabs[0], [_Seg("norm", HALF_W), _Seg("norm", HALF_W, ck_cache)],
                                        [gcq, gck], tabs, seq_len, cache_slot=i, cache_prev=prev[0])
                (cv, cgate), cc2 = _inproj(h, slabs[1], [_Seg("plain", HALF_W, cv_cache), _Seg("silu", HALF_W)],
                                           [None] * 2, None, seq_len, cache_slot=i, cache_prev=prev[1])
                segs3 = [_Seg("norm", HALF_W), _Seg("norm", N_KV * HEAD_DIM, kv_cache),
                         _Seg("plain", N_KV * HEAD_DIM, kv_cache), _Seg("silu", HALF_W)]
                (dq, dk, dv, dgate), cc3 = _inproj(h, slabs[2], segs3, [gdq, gdk, None, None], tabs, seq_len,
                                                   cache_slot=i, cache_prev=prev[2])
                return cq, ck, cv, cgate, dq, dk, dv, dgate, (cc1, cc2, cc3)

            cq, ck, cv, cgate, dq, dk, dv, dgate, new_c = project(hp, seq_p, None, True, new_c)
            y1p = _diff(cq, ck, cv, cgate, c_lambda[i], c_on, lam_init, seq_p, seq_p, prompt_rows)
            y2p = _gqa(dq, dk, dv, dgate, seq_p, seq_p, prompt_rows)

            cq, ck, cv, cgate, dq, dk, dv, dgate, _ = project(hs, seq_s, rope_tabs, False)
            y1s = _diff(cq, ck, cv, cgate, c_lambda[i], c_on, lam_init, seq_s, 2 * SAMPLE_QB, seq_s,
                        ctx=(cache_c_k8, cache_c_v), layer_idx=i)
            y2s = _gqa(dq, dk, dv, dgate, seq_s, 2 * SAMPLE_QB, seq_s,
                       ctx=(cache_d_k, cache_d_v), layer_idx=i)
            conv_p = conv_s = None

        xp, hp = _outproj(y1p, y2p, xp, w_out, norm_g3, mod4, layer, prompt_row, not last, conv_p)
        xs, hs = _outproj(y1s, y2s, xs, w_out, norm_g3, mod4, layer, sample_row, not last, conv_s)

    return (xp.reshape(n_prompt, seq_p, D_MODEL), xs.reshape(n_sample, seq_s, D_MODEL),
            new_a[0], new_a[1],
            new_c[0][0].reshape(n_prompt, CACHE_SLOTS, 2, C_HEADS, seq_p, HEAD_DIM), new_c[1][0],
            new_c[2][0], new_c[2][1])
```

```python
import functools
import math

import jax
import jax.numpy as jnp
from jax import lax
from jax.experimental import pallas as pl
from jax.experimental.pallas import tpu as pltpu

D_MODEL = 2048
DEPTH = 4
GRID_W = 64
HEAD_DIM = 128
WINDOW = 128
ROPE_THETA = 10000.0
EPS = 1e-6
NEG_INF = -1e30
HALF_W = D_MODEL // 2
N_KV = 2
GROUP = 4
C_HEADS = 4
C_V_DIM = 2 * HEAD_DIM
CACHE_SLOTS = DEPTH // 2
COND_ROWS = 16
CTX_ROW = 8

BF16 = jnp.bfloat16
F32 = jnp.float32

VMEM_LIMIT_BYTES = 52 * 1024 * 1024
PROJ_ROWS = 512
INPROJ_ROWS = 1024
PROJ_CHUNK = 512
SAMPLE_QB = 128
KEY_CHUNK = 512
UNROLL_SELF = 4
UNROLL_CTX = 2
HALO_ROWS = 16
LOG2E = math.log2(math.e)
PROMPT_SEQS_PER_STEP = 4


def _params(n_axes=1, flags=None):
    return pltpu.CompilerParams(
        dimension_semantics=("arbitrary",) * n_axes,
        vmem_limit_bytes=VMEM_LIMIT_BYTES,
        flags=flags,
    )


def _dot(a, b):
    return jnp.dot(a, b, preferred_element_type=F32)


def _dot_nt(a, b):
    return lax.dot_general(a, b, (((1,), (1,)), ((), ())), preferred_element_type=F32)


def _silu(x):
    return x / (1.0 + jnp.exp(-x))


def _mod_kernel(cond_ref, w_ref, b_ref, o_ref):
    s = _silu(cond_ref[...]).astype(BF16)
    o_ref[...] = _dot(s, w_ref[...].astype(BF16)) + b_ref[...]


def _modulation(cond, ada_w, ada_b):
    bn = 1024
    n_out = 3 * D_MODEL
    return pl.pallas_call(
        _mod_kernel,
        grid=(DEPTH, n_out // bn),
        in_specs=[
            pl.BlockSpec((COND_ROWS, D_MODEL), lambda l, n: (0, 0)),
            pl.BlockSpec((None, D_MODEL, bn), lambda l, n: (l, 0, n)),
            pl.BlockSpec((None, 1, bn), lambda l, n: (l, 0, n)),
        ],
        out_specs=pl.BlockSpec((None, COND_ROWS, bn), lambda l, n: (l, 0, n)),
        out_shape=jax.ShapeDtypeStruct((DEPTH, COND_ROWS, n_out), F32),
        compiler_params=_params(2),
        name="modulation",
    )(cond, ada_w, ada_b.reshape(DEPTH, 1, n_out))


def _mod_spec(layer, part, row_fn):
    return pl.BlockSpec((None, None, 1, D_MODEL), lambda m: (layer, row_fn(m), 0, part))


def _norm_mod(x, r, g, sc, sh):
    return ((x * r) * g) * (1.0 + sc) + sh


def _prenorm_kernel(x_ref, g_ref, sh_ref, sc_ref, h_ref):
    x = x_ref[...]
    r = lax.rsqrt(jnp.mean(x * x, axis=-1, keepdims=True) + EPS)
    h_ref[...] = _norm_mod(x, r, g_ref[...], sc_ref[...], sh_ref[...]).astype(BF16)


def _prenorm(x, norm_g3, mod4, layer, row_fn):
    m_rows = x.shape[0]
    bm = PROJ_ROWS
    return pl.pallas_call(
        _prenorm_kernel,
        grid=(m_rows // bm,),
        in_specs=[
            pl.BlockSpec((bm, D_MODEL), lambda m: (m, 0)),
            pl.BlockSpec((None, 1, D_MODEL), lambda m: (layer, 0, 0)),
            _mod_spec(layer, 0, row_fn),
            _mod_spec(layer, 1, row_fn),
        ],
        out_specs=pl.BlockSpec((bm, D_MODEL), lambda m: (m, 0)),
        out_shape=jax.ShapeDtypeStruct((m_rows, D_MODEL), BF16),
        compiler_params=_params(),
        name="prenorm",
    )(x, norm_g3, mod4, mod4)


class _Seg:
    def __init__(self, kind, width, cache=None, mxu_rowsum=False):
        self.kind = kind
        self.width = width
        self.cache = cache
        self.mxu_rowsum = mxu_rowsum


def _swap_halves(a, first_half):
    return jnp.where(first_half, pltpu.roll(a, 96, axis=1), pltpu.roll(a, 32, axis=1))


def _inproj_kernel(*refs, segs, rope, seq_len, bm, n_prev, first_fill, n_wtiles):
    it = iter(refs)
    h_ref = next(it)
    w_refs = [next(it) for _ in range(n_wtiles)]
    gain_refs = [tuple(next(it) for _ in range(2 if rope else 1)) if s.kind == "norm" else None
                 for s in segs]
    for _ in range(n_prev):
        next(it)
    out_refs = [next(it) for _ in segs]
    cache_refs = [next(it) if s.cache else None for s in segs]

    if rope:
        lane = lax.broadcasted_iota(jnp.int32, (bm, HEAD_DIM), 1)
        first_half = (lane % 64) < 32

    chunks = []
    col = 0
    for si, seg in enumerate(segs):
        for c0 in range(0, seg.width, PROJ_CHUNK):
            chunks.append((si, c0, min(PROJ_CHUNK, seg.width - c0), col + c0))
        col += seg.width

    def matmul(chunk):
        _, _, cw, wc = chunk
        off = wc % PROJ_CHUNK
        return _dot(h_ref[...], w_refs[wc // PROJ_CHUNK][:, off:off + cw])

    def epilogue(chunk, acc):
        si, c0, cw, _ = chunk
        seg, gain_ref, out_ref, cache_ref = segs[si], gain_refs[si], out_refs[si], cache_refs[si]
        if seg.kind == "norm":
            heads = []
            for hh in range(cw // HEAD_DIM):
                a = acc[:, hh * HEAD_DIM:(hh + 1) * HEAD_DIM]
                if seg.mxu_rowsum:
                    sq = a * a
                    hi = sq.astype(BF16)
                    lo = (sq - hi.astype(F32)).astype(BF16)
                    ones = jnp.ones((2 * HEAD_DIM, HEAD_DIM), BF16)
                    ssq = _dot(jnp.concatenate([hi, lo], axis=1), ones)
                    r = lax.rsqrt(ssq * (1.0 / HEAD_DIM) + EPS)
                else:
                    r = lax.rsqrt(jnp.mean(a * a, axis=-1, keepdims=True) + EPS)
                if rope:
                    y = (a * gain_ref[0][...] + _swap_halves(a, first_half) * gain_ref[1][...]) * r
                else:
                    y = (a * r) * gain_ref[0][...]
                heads.append(y)
            val = jnp.concatenate(heads, axis=1) if len(heads) > 1 else heads[0]
        elif seg.kind == "silu":
            val = _silu(acc)
        else:
            val = acc
        out_ref[:, c0:c0 + cw] = val.astype(BF16)
        if seg.cache:
            _, hw = seg.cache
            for j in range(bm // seq_len):
                for hh in range(cw // hw):
                    head = (c0 // hw) + hh
                    data = val[j * seq_len:(j + 1) * seq_len, hh * hw:(hh + 1) * hw]
                    if first_fill is None:
                        cache_ref[j, head] = data
                    else:
                        for slot in range(CACHE_SLOTS):
                            cache_ref[j, slot, head] = data if slot == first_fill else jnp.zeros_like(data)

    acc = matmul(chunks[0])
    for idx, chunk in enumerate(chunks):
        nxt = matmul(chunks[idx + 1]) if idx + 1 < len(chunks) else None
        epilogue(chunk, acc)
        acc = nxt


def _weight_tile_specs(weights):
    w, layer, first, n = weights
    specs = [pl.BlockSpec((None, D_MODEL, PROJ_CHUNK), lambda m, t=first + t: (layer, 0, t),
                          pipeline_mode=pl.Buffered(1)) for t in range(n)]
    return specs, [w] * n


def _inproj(h, weights, segs, gains, rope_tabs, seq_len, cache_slot=0, cache_prev=None):
    m_rows = h.shape[0]
    bm = INPROJ_ROWS
    rope = rope_tabs is not None
    w_specs, w_args = _weight_tile_specs(weights)
    assert sum(s.width for s in segs) == len(w_args) * PROJ_CHUNK
    in_specs = [pl.BlockSpec((bm, D_MODEL), lambda m: (m, 0))] + w_specs
    args = [h] + w_args
    if rope:
        cos, sin_signed = rope_tabs
        blocks_per_seq = seq_len // bm
        tab_spec = pl.BlockSpec((bm, HEAD_DIM), lambda m: (m % blocks_per_seq, 0))
    for seg, g in zip(segs, gains):
        if seg.kind != "norm":
            continue
        if rope:
            g_swapped = g.reshape(2, 2, HEAD_DIM // 4)[:, ::-1].reshape(HEAD_DIM)
            in_specs += [tab_spec, tab_spec]
            args += [cos * g[None, :], sin_signed * g_swapped[None, :]]
        else:
            in_specs.append(pl.BlockSpec((1, HEAD_DIM), lambda m: (0, 0)))
            args.append(g.reshape(1, HEAD_DIM))
    n_prev = 0 if cache_prev is None else len(cache_prev)
    aliases = {}
    for k in range(n_prev):
        aliases[len(args)] = len(segs) + k
        in_specs.append(pl.BlockSpec(memory_space=pl.ANY))
        args.append(cache_prev[k])
    out_specs = [pl.BlockSpec((bm, s.width), lambda m: (m, 0)) for s in segs]
    out_shape = [jax.ShapeDtypeStruct((m_rows, s.width), BF16) for s in segs]
    for s in segs:
        if s.cache:
            nh, hw = s.cache
            if cache_prev is None:
                out_specs.append(pl.BlockSpec((bm // seq_len, CACHE_SLOTS, nh, seq_len, hw),
                                              lambda m: (m, 0, 0, 0, 0)))
            else:
                out_specs.append(pl.BlockSpec((bm // seq_len, None, nh, seq_len, hw),
                                              lambda m: (m, cache_slot, 0, 0, 0)))
            out_shape.append(jax.ShapeDtypeStruct((m_rows // seq_len, CACHE_SLOTS, nh, seq_len, hw), F32))
    outs = pl.pallas_call(
        functools.partial(_inproj_kernel, segs=segs, rope=rope, seq_len=seq_len, bm=bm, n_prev=n_prev,
                          first_fill=cache_slot if cache_prev is None else None, n_wtiles=len(w_args)),
        grid=(m_rows // bm,),
        in_specs=in_specs,
        out_specs=out_specs,
        out_shape=out_shape,
        input_output_aliases=aliases,
        compiler_params=_params(),
        name="inproj",
    )(*args)
    n = len(segs)
    return list(outs[:n]), list(outs[n:])


def _softmax_pv(scores, values, sinks=None, joint=False):
    groups = 1 if sinks is None else len(sinks)
    rows = scores[0].shape[0] // groups
    if joint or len(scores) == 1:
        probs = [[] for _ in scores]
        inv = []
        for g in range(groups):
            rs = slice(g * rows, (g + 1) * rows)
            m = jnp.max(scores[0][rs], axis=-1, keepdims=True)
            for s in scores[1:]:
                m = jnp.maximum(m, jnp.max(s[rs], axis=-1, keepdims=True))
            l = None
            if sinks is not None:
                m = jnp.maximum(m, sinks[g])
                l = jnp.exp2(sinks[g] - m)
            for i, s in enumerate(scores):
                e = jnp.exp2(s[rs] - m)
                part = jnp.sum(e, axis=-1, keepdims=True)
                l = part if l is None else l + part
                probs[i].append(e.astype(BF16))
            inv.append(1.0 / l)
        o = None
        for p, v in zip(probs, values):
            pv = _dot(p[0] if groups == 1 else jnp.concatenate(p, axis=0), v)
            o = pv if o is None else o + pv
        return [o[g * rows:(g + 1) * rows] * inv[g] for g in range(groups)]
    parts = []
    for s, v in zip(scores, values):
        m = jnp.max(s, axis=-1, keepdims=True)
        e = jnp.exp2(s - m)
        parts.append((m, jnp.sum(e, axis=-1, keepdims=True), _dot(e.astype(BF16), v)))
    outs = []
    for g in range(groups):
        rs = slice(g * rows, (g + 1) * rows)
        m_all = parts[0][0][rs]
        for m, _, _ in parts[1:]:
            m_all = jnp.maximum(m_all, m[rs])
        den = None
        if sinks is not None:
            m_all = jnp.maximum(m_all, sinks[g])
            den = jnp.exp2(sinks[g] - m_all)
        num = None
        for m, l, o in parts:
            w = jnp.exp2(m[rs] - m_all)
            den = l[rs] * w if den is None else den + l[rs] * w
            num = o[rs] * w if num is None else num + o[rs] * w
        outs.append(num * (1.0 / den))
    return outs


def _stack_heads(ref, r0, rows, first_head):
    return jnp.concatenate(
        [ref[pl.ds(r0, rows), (first_head + g) * HEAD_DIM:(first_head + g + 1) * HEAD_DIM]
         for g in range(GROUP)], axis=0)


def _head_sinks(sink_ref, first_head):
    return [sink_ref[first_head + g] * LOG2E for g in range(GROUP)]


def _key_chunks(scores, values, q, k_ref, v_ref, k0, seq_len, cols_k, cols_v, chunk=KEY_CHUNK):
    step = min(seq_len, chunk)
    for c in range(0, seq_len, step):
        scores.append(_dot_nt(q, k_ref[pl.ds(k0 + c, step), cols_k]))
        values.append(v_ref[pl.ds(k0 + c, step), cols_v])


def _store_gated(o_ref, g_ref, o, r0, rows, first_head):
    if len(o) == 1:
        o = [o[0][g * rows:(g + 1) * rows] for g in range(GROUP)]
    for g in range(GROUP):
        cs = slice((first_head + g) * HEAD_DIM, (first_head + g + 1) * HEAD_DIM)
        gate = g_ref[pl.ds(r0, rows), cs].astype(F32)
        o_ref[pl.ds(r0, rows), cs] = (o[g] * gate).astype(BF16)


def _gqa_kernel(*refs, seq_len, qb, n_blocks, ctx, sink):
    it = iter(refs)
    sink_ref = next(it) if sink else None
    q_ref, k_ref, v_ref, g_ref = next(it), next(it), next(it), next(it)
    if ctx:
        ck_ref, cv_ref = next(it), next(it)
    o_ref = next(it)
    if ctx:
        kc_s, vc_s = next(it), next(it)
        for kv in range(N_KV):
            kc_s[kv] = ck_ref[kv].astype(BF16)
            vc_s[kv] = cv_ref[kv].astype(BF16)
    blocks_per_seq = seq_len // qb

    def body(n, carry):
        r0 = pl.multiple_of(n * qb, qb)
        k0 = pl.multiple_of((n // blocks_per_seq) * seq_len, seq_len)
        for kv in range(N_KV):
            hs = slice(kv * HEAD_DIM, (kv + 1) * HEAD_DIM)
            q = _stack_heads(q_ref, r0, qb, kv * GROUP)
            scores, values = [], []
            if ctx:
                scores.append(_dot_nt(q, kc_s[kv]))
                values.append(vc_s[kv])
            _key_chunks(scores, values, q, k_ref, v_ref, k0, seq_len, hs, hs)
            sinks = _head_sinks(sink_ref, kv * GROUP) if sink else None
            o = _softmax_pv(scores, values, sinks)
            _store_gated(o_ref, g_ref, o, r0, qb, kv * GROUP)
        return carry

    lax.fori_loop(0, n_blocks, body, 0, unroll=UNROLL_CTX if ctx else UNROLL_SELF)


def _gqa(q, k, v, gate, seq_len, qb, rows_per_step, sink=None, ctx=None, layer_idx=0):
    m_rows = q.shape[0]
    kvw = N_KV * HEAD_DIM
    in_specs, args = [], []
    if sink is not None:
        in_specs.append(pl.BlockSpec(memory_space=pltpu.SMEM))
        args.append(sink)
    in_specs += [
        pl.BlockSpec((rows_per_step, HALF_W), lambda b: (b, 0)),
        pl.BlockSpec((rows_per_step, kvw), lambda b: (b, 0)),
        pl.BlockSpec((rows_per_step, kvw), lambda b: (b, 0)),
        pl.BlockSpec((rows_per_step, HALF_W), lambda b: (b, 0)),
    ]
    args += [q, k, v, gate]
    scratch = []
    if ctx is not None:
        past = ctx[0].shape[-2]
        cspec = pl.BlockSpec((None, None, N_KV, past, HEAD_DIM), lambda b: (b, layer_idx, 0, 0, 0))
        in_specs += [cspec, cspec]
        args += list(ctx)
        scratch = [pltpu.VMEM((N_KV, past, HEAD_DIM), BF16)] * 2
    return pl.pallas_call(
        functools.partial(_gqa_kernel, seq_len=seq_len, qb=qb, n_blocks=rows_per_step // qb,
                          ctx=ctx is not None, sink=sink is not None),
        grid=(m_rows // rows_per_step,),
        in_specs=in_specs,
        out_specs=pl.BlockSpec((rows_per_step, HALF_W), lambda b: (b, 0)),
        out_shape=jax.ShapeDtypeStruct((m_rows, HALF_W), BF16),
        scratch_shapes=scratch,
        compiler_params=_params(),
        name="gqa",
    )(*args)


def _banded_kernel(sink_ref, q_ref, k_ref, v_ref, g_ref, ck_ref, cv_ref, o_ref,
                   kpad, vpad, kc_s, vc_s, *, seq_len):
    n_blocks = seq_len // WINDOW
    zeros = jnp.zeros((WINDOW, HEAD_DIM), BF16)
    for kv in range(N_KV):
        hs = slice(kv * HEAD_DIM, (kv + 1) * HEAD_DIM)
        kpad[kv, 0:WINDOW] = zeros
        vpad[kv, 0:WINDOW] = zeros
        kpad[kv, WINDOW + seq_len:2 * WINDOW + seq_len] = zeros
        vpad[kv, WINDOW + seq_len:2 * WINDOW + seq_len] = zeros
        kpad[kv, WINDOW:WINDOW + seq_len] = k_ref[:, hs]
        vpad[kv, WINDOW:WINDOW + seq_len] = v_ref[:, hs]
        kc_s[kv] = ck_ref[kv].astype(BF16)
        vc_s[kv] = cv_ref[kv].astype(BF16)

    rows = GROUP * WINDOW
    qi = lax.broadcasted_iota(jnp.int32, (rows, 3 * WINDOW), 0) % WINDOW
    col = lax.broadcasted_iota(jnp.int32, (rows, 3 * WINDOW), 1)
    kj = col % WINDOW
    blk = col // WINDOW
    never = 4 * WINDOW

    def body(n, carry):
        r0 = pl.multiple_of(n * WINDOW, WINDOW)
        lo = jnp.where(n > 0, 0, never)
        hi = jnp.where(n < n_blocks - 1, 0, never)
        slack = jnp.where(blk == 0, kj - qi - lo, jnp.where(blk == 2, qi - kj - hi, 0))
        valid = slack >= 0
        for kv in range(N_KV):
            q = _stack_heads(q_ref, r0, WINDOW, kv * GROUP)
            kb = kpad[kv, pl.ds(r0, 3 * WINDOW), :]
            vb = vpad[kv, pl.ds(r0, 3 * WINDOW), :]
            s_ctx = _dot_nt(q, kc_s[kv])
            s_band = jnp.where(valid, _dot_nt(q, kb), NEG_INF)
            o = _softmax_pv([s_ctx, s_band], [vc_s[kv], vb], _head_sinks(sink_ref, kv * GROUP))
            _store_gated(o_ref, g_ref, o, r0, WINDOW, kv * GROUP)
        return carry

    lax.fori_loop(0, n_blocks, body, 0, unroll=UNROLL_CTX)


def _banded(q, k, v, gate, sink, cache_k, cache_v, layer_idx, seq_len):
    m_rows = q.shape[0]
    kvw = N_KV * HEAD_DIM
    past = cache_k.shape[-2]
    cspec = pl.BlockSpec((None, None, N_KV, past, HEAD_DIM), lambda b: (b, layer_idx, 0, 0, 0))
    pad_rows = seq_len + 2 * WINDOW
    return pl.pallas_call(
        functools.partial(_banded_kernel, seq_len=seq_len),
        grid=(m_rows // seq_len,),
        in_specs=[
            pl.BlockSpec(memory_space=pltpu.SMEM),
            pl.BlockSpec((seq_len, HALF_W), lambda b: (b, 0)),
            pl.BlockSpec((seq_len, kvw), lambda b: (b, 0)),
            pl.BlockSpec((seq_len, kvw), lambda b: (b, 0)),
            pl.BlockSpec((seq_len, HALF_W), lambda b: (b, 0)),
            cspec, cspec,
        ],
        out_specs=pl.BlockSpec((seq_len, HALF_W), lambda b: (b, 0)),
        out_shape=jax.ShapeDtypeStruct((m_rows, HALF_W), BF16),
        scratch_shapes=[
            pltpu.VMEM((N_KV, pad_rows, HEAD_DIM), BF16),
            pltpu.VMEM((N_KV, pad_rows, HEAD_DIM), BF16),
            pltpu.VMEM((N_KV, past, HEAD_DIM), BF16),
            pltpu.VMEM((N_KV, past, HEAD_DIM), BF16),
        ],
        compiler_params=_params(),
        name="banded",
    )(sink, q, k, v, gate, cache_k, cache_v)


def _diff_kernel(*refs, seq_len, qb, n_blocks, ctx, lam_init):
    it = iter(refs)
    lam_ref, on_ref = next(it), next(it)
    q_ref, k_ref, v_ref, g_ref = next(it), next(it), next(it), next(it)
    if ctx:
        ck_ref, cv_ref = next(it), next(it)
    o_ref = next(it)
    if ctx:
        kc_s, vc_s = next(it), next(it)
        for slot in range(2 * C_HEADS):
            kc_s[slot] = ck_ref[slot].astype(BF16)
        for hd in range(C_HEADS):
            vc_s[hd] = cv_ref[hd].astype(BF16)
    lf = lam_ref[...]
    lam = (jnp.exp(jnp.sum(lf[0:1] * lf[1:2], axis=-1, keepdims=True))
           - jnp.exp(jnp.sum(lf[2:3] * lf[3:4], axis=-1, keepdims=True)) + lam_init)
    out_gain = on_ref[...]
    blocks_per_seq = seq_len // qb

    def body(n, carry):
        r0 = pl.multiple_of(n * qb, qb)
        k0 = pl.multiple_of((n // blocks_per_seq) * seq_len, seq_len)
        for hd in range(C_HEADS):
            vs = slice(hd * C_V_DIM, (hd + 1) * C_V_DIM)
            maps = []
            for mp in range(2):
                slot = mp * C_HEADS + hd
                hs = slice(slot * HEAD_DIM, (slot + 1) * HEAD_DIM)
                q = q_ref[pl.ds(r0, qb), hs]
                scores, values = [], []
                if ctx:
                    scores.append(_dot_nt(q, kc_s[slot]))
                    values.append(vc_s[hd])
                _key_chunks(scores, values, q, k_ref, v_ref, k0, seq_len, hs, vs, chunk=seq_len)
                maps.append(_softmax_pv(scores, values, joint=True)[0])
            d = maps[0] - lam * maps[1]
            r = lax.rsqrt(jnp.mean(d * d, axis=-1, keepdims=True) + EPS)
            y = ((d * r) * out_gain) * (1.0 - lam_init)
            gate = g_ref[pl.ds(r0, qb), vs].astype(F32)
            o_ref[pl.ds(r0, qb), vs] = (y * gate).astype(BF16)
        return carry

    lax.fori_loop(0, n_blocks, body, 0, unroll=UNROLL_CTX)


def _diff(q, k, v, gate, c_lam, c_on, lam_init, seq_len, qb, rows_per_step, ctx=None, layer_idx=0):
    m_rows = q.shape[0]
    in_specs = [
        pl.BlockSpec((4, HEAD_DIM), lambda b: (0, 0)),
        pl.BlockSpec((1, C_V_DIM), lambda b: (0, 0)),
    ] + [pl.BlockSpec((rows_per_step, HALF_W), lambda b: (b, 0))] * 4
    args = [c_lam, c_on, q, k, v, gate]
    scratch = []
    if ctx is not None:
        past = ctx[0].shape[-2]
        in_specs += [
            pl.BlockSpec((None, None, 2 * C_HEADS, past, HEAD_DIM), lambda b: (b, layer_idx, 0, 0, 0)),
            pl.BlockSpec((None, None, C_HEADS, past, C_V_DIM), lambda b: (b, layer_idx, 0, 0, 0)),
        ]
        args += list(ctx)
        scratch = [pltpu.VMEM((2 * C_HEADS, past, HEAD_DIM), BF16),
                   pltpu.VMEM((C_HEADS, past, C_V_DIM), BF16)]
    return pl.pallas_call(
        functools.partial(_diff_kernel, seq_len=seq_len, qb=qb, n_blocks=rows_per_step // qb,
                          ctx=ctx is not None, lam_init=lam_init),
        grid=(m_rows // rows_per_step,),
        in_specs=in_specs,
        out_specs=pl.BlockSpec((rows_per_step, HALF_W), lambda b: (b, 0)),
        out_shape=jax.ShapeDtypeStruct((m_rows, HALF_W), BF16),
        scratch_shapes=scratch,
        compiler_params=_params(),
        name="diff_attn",
    )(*args)


def _outproj_kernel(*refs, bm, emit_h, conv_seq):
    it = iter(refs)
    y1_ref = next(it)
    if conv_seq is None:
        y2_ref = next(it)
    else:
        u_ref, b_ref, c_ref, g_ref = next(it), next(it), next(it), next(it)
        up_ref, cp_ref, un_ref, cn_ref = next(it), next(it), next(it), next(it)
        cw_ref = next(it)
    x_ref = next(it)
    w_refs = [next(it) for _ in range(D_MODEL // PROJ_CHUNK)]
    gt_ref = next(it)
    if emit_h:
        gn_ref, sh_ref, sc_ref = next(it), next(it), next(it)
    xo_ref = next(it)
    if emit_h:
        ho_ref = next(it)
    chunks = [slice(c, c + PROJ_CHUNK) for c in range(0, D_MODEL, PROJ_CHUNK)]
    w_top = [w.at[0:HALF_W, :] for w in w_refs]
    w_bot = [w.at[HALF_W:D_MODEL, :] for w in w_refs]
    if conv_seq is not None:
        assert bm % conv_seq == 0 or conv_seq % bm == 0
        row = lax.broadcasted_iota(jnp.int32, (bm, 1), 0)
        pos = (pl.program_id(0) * bm + row) % conv_seq
        first_row = jnp.logical_or(pos == 0, row == 0)
        last_row = jnp.logical_or(pos == conv_seq - 1, row == bm - 1)
        tile_pos = (pl.program_id(0) * bm) % conv_seq
        use_before = jnp.where(tile_pos == 0, 0.0, 1.0)
        use_after = jnp.where(tile_pos + bm == conv_seq, 0.0, 1.0) if conv_seq % bm == 0 else 0.0
        lanes = HALF_W // len(chunks)
        y2_parts = []
        for ci, cs in enumerate(chunks):
            ls = slice(ci * lanes, (ci + 1) * lanes)
            z = c_ref[:, ls].astype(F32) * u_ref[:, ls].astype(F32)
            z_before = (cp_ref[:, ls].astype(F32) * up_ref[:, ls].astype(F32))[HALO_ROWS - 1:HALO_ROWS] * use_before
            z_after = (cn_ref[:, ls].astype(F32) * un_ref[:, ls].astype(F32))[0:1] * use_after
            z_prev = jnp.where(first_row, z_before, pltpu.roll(z, 1, axis=0))
            z_next = jnp.where(last_row, z_after, pltpu.roll(z, bm - 1, axis=0))
            conv = z_prev * cw_ref[0:1, ls] + z * cw_ref[1:2, ls] + z_next * cw_ref[2:3, ls]
            y2_parts.append(((b_ref[:, ls].astype(F32) * conv) * g_ref[:, ls].astype(F32)).astype(BF16))
            xo_ref[:, cs] = x_ref[:, cs] + gt_ref[:, cs] * _dot(y1_ref[...], w_top[ci][...])
        y2 = jnp.concatenate(y2_parts, axis=1)
    ssq = jnp.zeros((bm, 1), F32)
    for ci, cs in enumerate(chunks):
        if conv_seq is None:
            acc = _dot(y1_ref[...], w_top[ci][...]) + _dot(y2_ref[...], w_bot[ci][...])
            xn = x_ref[:, cs] + gt_ref[:, cs] * acc
        else:
            xn = xo_ref[:, cs] + gt_ref[:, cs] * _dot(y2, w_bot[ci][...])
        xo_ref[:, cs] = xn
        if emit_h:
            ssq = ssq + jnp.sum(xn * xn, axis=-1, keepdims=True)
    if emit_h:
        r = lax.rsqrt(ssq * (1.0 / D_MODEL) + EPS)
        for cs in chunks:
            gain = gn_ref[:, cs] * (1.0 + sc_ref[:, cs])
            ho_ref[:, cs] = ((xo_ref[:, cs] * r) * gain + sh_ref[:, cs]).astype(BF16)


def _outproj(y1, y2, x, w_out, norm_g3, mod4, layer, row_fn, emit_h, conv=None):
    m_rows = x.shape[0]
    bm = PROJ_ROWS
    half_spec = pl.BlockSpec((bm, HALF_W), lambda m: (m, 0))
    in_specs = [half_spec]
    args = [y1]
    scratch = []
    if conv is None:
        in_specs.append(half_spec)
        args.append(y2)
    else:
        u, bgate, cgate, gate, conv_w, _ = conv
        halo_blocks = bm // HALO_ROWS
        last_halo = m_rows // HALO_ROWS - 1
        before = pl.BlockSpec((HALO_ROWS, HALF_W), lambda m: (jnp.maximum(m * halo_blocks - 1, 0), 0))
        after = pl.BlockSpec((HALO_ROWS, HALF_W), lambda m: (jnp.minimum((m + 1) * halo_blocks, last_halo), 0))
        in_specs += [half_spec] * 4 + [before, before, after, after,
                                       pl.BlockSpec((3, HALF_W), lambda m: (0, 0))]
        args += [u, bgate, cgate, gate, u, cgate, u, cgate, conv_w]
    w_specs, w_args = _weight_tile_specs(w_out)
    in_specs += [pl.BlockSpec((bm, D_MODEL), lambda m: (m, 0))] + w_specs + [_mod_spec(layer, 2, row_fn)]
    args += [x] + w_args + [mod4]
    out_specs = [pl.BlockSpec((bm, D_MODEL), lambda m: (m, 0))]
    out_shape = [jax.ShapeDtypeStruct((m_rows, D_MODEL), F32)]
    if emit_h:
        in_specs += [
            pl.BlockSpec((None, 1, D_MODEL), lambda m: (layer + 1, 0, 0)),
            _mod_spec(layer + 1, 0, row_fn),
            _mod_spec(layer + 1, 1, row_fn),
        ]
        args += [norm_g3, mod4, mod4]
        out_specs.append(pl.BlockSpec((bm, D_MODEL), lambda m: (m, 0)))
        out_shape.append(jax.ShapeDtypeStruct((m_rows, D_MODEL), BF16))
    outs = pl.pallas_call(
        functools.partial(_outproj_kernel, bm=bm, emit_h=emit_h,
                          conv_seq=None if conv is None else conv[5]),
        grid=(m_rows // bm,),
        in_specs=in_specs,
        out_specs=out_specs,
        out_shape=out_shape,
        scratch_shapes=scratch,
        compiler_params=_params(),
        name="outproj",
    )(*args)
    return (outs[0], outs[1]) if emit_h else (outs[0], None)


def _rope_tables(n_tokens):
    t = jnp.arange(n_tokens)
    row = t // GRID_W
    col = t % GRID_W
    n_freq = HEAD_DIM // 4
    inv = ROPE_THETA ** (-jnp.arange(n_freq, dtype=F32) / n_freq)
    ang_r = row[:, None] * inv
    ang_c = col[:, None] * inv
    cos = jnp.concatenate([jnp.cos(ang_r)] * 2 + [jnp.cos(ang_c)] * 2, axis=1)
    sin = jnp.concatenate([-jnp.sin(ang_r), jnp.sin(ang_r), -jnp.sin(ang_c), jnp.sin(ang_c)], axis=1)
    return cos, sin


def kernel(x_prompt, x_sample, cache_a_k, cache_a_v, cache_c_k, cache_c_v, cache_d_k, cache_d_v,
           c, c_ctx, norm_g, ada_w, ada_b, ev_w_in, ev_w_out, a_q_norm, a_k_norm, a_sink, b_conv,
           od_w_in, od_w_out, c_q_norm, c_k_norm, c_lambda, c_out_norm, d_q_norm, d_k_norm):
    n_prompt, seq_p, _ = x_prompt.shape
    n_sample, seq_s, _ = x_sample.shape
    past = cache_a_k.shape[-2]
    scale = HEAD_DIM ** -0.5 * LOG2E

    cond = jnp.concatenate(
        [c, c_ctx[None, :], jnp.zeros((COND_ROWS - n_sample - 1, D_MODEL), F32)], axis=0)
    mod4 = _modulation(cond, ada_w, ada_b).reshape(DEPTH, COND_ROWS, 1, 3 * D_MODEL)
    norm_g3 = norm_g.reshape(DEPTH, 1, D_MODEL)
    rope_tabs = _rope_tables(seq_s)

    prompt_row = lambda m: CTX_ROW
    sample_row = lambda m: (m * PROJ_ROWS) // seq_s

    xp = x_prompt.reshape(n_prompt * seq_p, D_MODEL)
    xs = x_sample.reshape(n_sample * seq_s, D_MODEL)
    hp = _prenorm(xp, norm_g3, mod4, 0, prompt_row)
    hs = _prenorm(xs, norm_g3, mod4, 0, sample_row)

    ev_in_tiles, ev_out_tiles = ev_w_in.astype(BF16), ev_w_out.astype(BF16)
    od_in_tiles, od_out_tiles = od_w_in.astype(BF16), od_w_out.astype(BF16)
    cache_c_k8 = cache_c_k.reshape(n_sample, DEPTH // 2, 2 * C_HEADS, past, HEAD_DIM)
    prompt_rows = PROMPT_SEQS_PER_STEP * seq_p

    new_a = None
    new_c = (None, None, None)
    for layer in range(DEPTH):
        i = layer // 2
        last = layer == DEPTH - 1
        if layer % 2 == 0:
            w_out = (ev_out_tiles, i, 0, D_MODEL // PROJ_CHUNK)
            slabs = [(ev_in_tiles, i, first, n) for first, n in ((0, 5), (5, 4), (9, 4))]
            gq = a_q_norm[i] * scale
            gk = a_k_norm[i]
            kv_cache = (N_KV, HEAD_DIM)

            def project(h, seq_len, tabs, cache, prev=None):
                segs1 = [_Seg("norm", HALF_W), _Seg("norm", N_KV * HEAD_DIM, cache),
                         _Seg("plain", N_KV * HEAD_DIM, cache), _Seg("silu", HALF_W)]
                (q, k, v, ga), caches = _inproj(h, slabs[0], segs1, [gq, gk, None, None], tabs, seq_len,
                                                cache_slot=i, cache_prev=prev)
                (u, bg), _ = _inproj(h, slabs[1], [_Seg("plain", HALF_W)] * 2, [None] * 2, None, seq_len)
                (cg, gb), _ = _inproj(h, slabs[2], [_Seg("plain", HALF_W), _Seg("silu", HALF_W)],
                                      [None] * 2, None, seq_len)
                return q, k, v, ga, u, bg, cg, gb, caches

            q, k, v, ga, u, bg, cg, gb, new_a = project(hp, seq_p, None, kv_cache, new_a)
            y1p = _gqa(q, k, v, ga, seq_p, seq_p, prompt_rows, sink=a_sink[i])
            y2p, conv_p = None, (u, bg, cg, gb, b_conv[i], seq_p)

            q, k, v, ga, u, bg, cg, gb, _ = project(hs, seq_s, rope_tabs, None)
            y1s = _banded(q, k, v, ga, a_sink[i], cache_a_k, cache_a_v, i, seq_s)
            y2s, conv_s = None, (u, bg, cg, gb, b_conv[i], seq_s)
        else:
            lam_init = 0.8 - 0.6 * math.exp(-0.3 * layer)
            w_out = (od_out_tiles, i, 0, D_MODEL // PROJ_CHUNK)
            slabs = [(od_in_tiles, i, first, n) for first, n in ((0, 4), (4, 4), (8, 5))]
            gcq = c_q_norm[i] * scale
            gck = c_k_norm[i]
            gdq = d_q_norm[i] * scale
            gdk = d_k_norm[i]
            c_on = c_out_norm[i].reshape(1, C_V_DIM)

            def project(h, seq_len, tabs, with_cache, prev=(None, None, None)):
                ck_cache = (2 * C_HEADS, HEAD_DIM) if with_cache else None
                cv_cache = (C_HEADS, C_V_DIM) if with_cache else None
                kv_cache = (N_KV, HEAD_DIM) if with_cache else None
                on_mxu = tabs is not None
                (cq, ck), cc1 = _inproj(h, slabs[0], [_Seg("norm", HALF_W, None, on_mxu),
                                                      _Seg("norm", HALF_W, ck_cache, on_mxu)],
                                        [gcq, gck], tabs, seq_len, cache_slot=i, cache_prev=prev[0])
                (cv, cgate), cc2 = _inproj(h, slabs[1], [_Seg("plain", HALF_W, cv_cache), _Seg("silu", HALF_W)],
                                           [None] * 2, None, seq_len, cache_slot=i, cache_prev=prev[1])
                segs3 = [_Seg("norm", HALF_W), _Seg("norm", N_KV * HEAD_DIM, kv_cache),
                         _Seg("plain", N_KV * HEAD_DIM, kv_cache), _Seg("silu", HALF_W)]
                (dq, dk, dv, dgate), cc3 = _inproj(h, slabs[2], segs3, [gdq, gdk, None, None], tabs, seq_len,
                                                   cache_slot=i, cache_prev=prev[2])
                return cq, ck, cv, cgate, dq, dk, dv, dgate, (cc1, cc2, cc3)

            cq, ck, cv, cgate, dq, dk, dv, dgate, new_c = project(hp, seq_p, None, True, new_c)
            y1p = _diff(cq, ck, cv, cgate, c_lambda[i], c_on, lam_init, seq_p, seq_p, prompt_rows)
            y2p = _gqa(dq, dk, dv, dgate, seq_p, seq_p, prompt_rows)

            cq, ck, cv, cgate, dq, dk, dv, dgate, _ = project(hs, seq_s, rope_tabs, False)
            y1s = _diff(cq, ck, cv, cgate, c_lambda[i], c_on, lam_init, seq_s, 2 * SAMPLE_QB, seq_s,
                        ctx=(cache_c_k8, cache_c_v), layer_idx=i)
            y2s = _gqa(dq, dk, dv, dgate, seq_s, 2 * SAMPLE_QB, seq_s,
                       ctx=(cache_d_k, cache_d_v), layer_idx=i)
            conv_p = conv_s = None

        xp, hp = _outproj(y1p, y2p, xp, w_out, norm_g3, mod4, layer, prompt_row, not last, conv_p)
        xs, hs = _outproj(y1s, y2s, xs, w_out, norm_g3, mod4, layer, sample_row, not last, conv_s)

    return (xp.reshape(n_prompt, seq_p, D_MODEL), xs.reshape(n_sample, seq_s, D_MODEL),
            new_a[0], new_a[1],
            new_c[0][0].reshape(n_prompt, CACHE_SLOTS, 2, C_HEADS, seq_p, HEAD_DIM), new_c[1][0],
            new_c[2][0], new_c[2][1])
```

```python
import functools
import math

import jax
import jax.numpy as jnp
from jax import lax
from jax.experimental import pallas as pl
from jax.experimental.pallas import tpu as pltpu

D_MODEL = 2048
DEPTH = 4
GRID_W = 64
HEAD_DIM = 128
WINDOW = 128
ROPE_THETA = 10000.0
EPS = 1e-6
NEG_INF = -1e30
HALF_W = D_MODEL // 2
N_KV = 2
GROUP = 4
C_HEADS = 4
C_V_DIM = 2 * HEAD_DIM
CACHE_SLOTS = DEPTH // 2
COND_ROWS = 16
CTX_ROW = 8

BF16 = jnp.bfloat16
F32 = jnp.float32

VMEM_LIMIT_BYTES = 52 * 1024 * 1024
PROJ_ROWS = 512
INPROJ_ROWS = 1024
ALL_TILES = 13
C_TILES = 8
PROJ_CHUNK = 512
SAMPLE_QB = 128
KEY_CHUNK = 512
UNROLL_SELF = 4
UNROLL_CTX = 2
HALO_ROWS = 16
LOG2E = math.log2(math.e)
PROMPT_SEQS_PER_STEP = 4


def _params(n_axes=1, flags=None):
    return pltpu.CompilerParams(
        dimension_semantics=("arbitrary",) * n_axes,
        vmem_limit_bytes=VMEM_LIMIT_BYTES,
        flags=flags,
    )


def _dot(a, b):
    return jnp.dot(a, b, preferred_element_type=F32)


def _dot_nt(a, b):
    return lax.dot_general(a, b, (((1,), (1,)), ((), ())), preferred_element_type=F32)


def _silu(x):
    return x / (1.0 + jnp.exp(-x))


def _mod_kernel(cond_ref, w_ref, b_ref, o_ref):
    s = _silu(cond_ref[...]).astype(BF16)
    o_ref[...] = _dot(s, w_ref[...].astype(BF16)) + b_ref[...]


def _modulation(cond, ada_w, ada_b):
    bn = 1024
    n_out = 3 * D_MODEL
    return pl.pallas_call(
        _mod_kernel,
        grid=(DEPTH, n_out // bn),
        in_specs=[
            pl.BlockSpec((COND_ROWS, D_MODEL), lambda l, n: (0, 0)),
            pl.BlockSpec((None, D_MODEL, bn), lambda l, n: (l, 0, n)),
            pl.BlockSpec((None, 1, bn), lambda l, n: (l, 0, n)),
        ],
        out_specs=pl.BlockSpec((None, COND_ROWS, bn), lambda l, n: (l, 0, n)),
        out_shape=jax.ShapeDtypeStruct((DEPTH, COND_ROWS, n_out), F32),
        compiler_params=_params(2),
        name="modulation",
    )(cond, ada_w, ada_b.reshape(DEPTH, 1, n_out))


def _mod_spec(layer, part, row_fn):
    return pl.BlockSpec((None, None, 1, D_MODEL), lambda m: (layer, row_fn(m), 0, part))


def _norm_mod(x, r, g, sc, sh):
    return ((x * r) * g) * (1.0 + sc) + sh


def _prenorm_kernel(x_ref, g_ref, sh_ref, sc_ref, h_ref):
    x = x_ref[...]
    r = lax.rsqrt(jnp.mean(x * x, axis=-1, keepdims=True) + EPS)
    h_ref[...] = _norm_mod(x, r, g_ref[...], sc_ref[...], sh_ref[...]).astype(BF16)


def _prenorm(x, norm_g3, mod4, layer, row_fn):
    m_rows = x.shape[0]
    bm = PROJ_ROWS
    return pl.pallas_call(
        _prenorm_kernel,
        grid=(m_rows // bm,),
        in_specs=[
            pl.BlockSpec((bm, D_MODEL), lambda m: (m, 0)),
            pl.BlockSpec((None, 1, D_MODEL), lambda m: (layer, 0, 0)),
            _mod_spec(layer, 0, row_fn),
            _mod_spec(layer, 1, row_fn),
        ],
        out_specs=pl.BlockSpec((bm, D_MODEL), lambda m: (m, 0)),
        out_shape=jax.ShapeDtypeStruct((m_rows, D_MODEL), BF16),
        compiler_params=_params(),
        name="prenorm",
    )(x, norm_g3, mod4, mod4)


class _Seg:
    def __init__(self, kind, width, cache=None):
        self.kind = kind
        self.width = width
        self.cache = cache


def _swap_halves(a, first_half):
    return jnp.where(first_half, pltpu.roll(a, 96, axis=1), pltpu.roll(a, 32, axis=1))


def _inproj_kernel(*refs, segs, rope, seq_len, bm, n_prev, first_fill, n_wtiles):
    it = iter(refs)
    h_ref = next(it)
    w_refs = [next(it) for _ in range(n_wtiles)]
    gain_refs = [tuple(next(it) for _ in range(2 if rope else 1)) if s.kind == "norm" else None
                 for s in segs]
    for _ in range(n_prev):
        next(it)
    out_refs = [next(it) for _ in segs]
    cache_refs = [next(it) if s.cache else None for s in segs]

    if rope:
        lane = lax.broadcasted_iota(jnp.int32, (bm, HEAD_DIM), 1)
        first_half = (lane % 64) < 32

    chunks = []
    col = 0
    for si, seg in enumerate(segs):
        for c0 in range(0, seg.width, PROJ_CHUNK):
            chunks.append((si, c0, min(PROJ_CHUNK, seg.width - c0), col + c0))
        col += seg.width

    def matmul(chunk):
        _, _, cw, wc = chunk
        off = wc % PROJ_CHUNK
        return _dot(h_ref[...], w_refs[wc // PROJ_CHUNK][:, off:off + cw])

    def epilogue(chunk, acc):
        si, c0, cw, _ = chunk
        seg, gain_ref, out_ref, cache_ref = segs[si], gain_refs[si], out_refs[si], cache_refs[si]
        if seg.kind == "norm":
            heads = []
            for hh in range(cw // HEAD_DIM):
                a = acc[:, hh * HEAD_DIM:(hh + 1) * HEAD_DIM]
                r = lax.rsqrt(jnp.mean(a * a, axis=-1, keepdims=True) + EPS)
                if rope:
                    y = (a * gain_ref[0][...] + _swap_halves(a, first_half) * gain_ref[1][...]) * r
                else:
                    y = (a * r) * gain_ref[0][...]
                heads.append(y)
            val = jnp.concatenate(heads, axis=1) if len(heads) > 1 else heads[0]
        elif seg.kind == "silu":
            val = _silu(acc)
        else:
            val = acc
        out_ref[:, c0:c0 + cw] = val.astype(BF16)
        if seg.cache:
            _, hw = seg.cache
            for j in range(bm // seq_len):
                for hh in range(cw // hw):
                    head = (c0 // hw) + hh
                    data = val[j * seq_len:(j + 1) * seq_len, hh * hw:(hh + 1) * hw]
                    if first_fill is None:
                        cache_ref[j, head] = data
                    else:
                        for slot in range(CACHE_SLOTS):
                            cache_ref[j, slot, head] = data if slot == first_fill else jnp.zeros_like(data)

    acc = matmul(chunks[0])
    for idx, chunk in enumerate(chunks):
        nxt = matmul(chunks[idx + 1]) if idx + 1 < len(chunks) else None
        epilogue(chunk, acc)
        acc = nxt


def _weight_tile_specs(weights):
    w, layer, first, n = weights
    specs = [pl.BlockSpec((None, D_MODEL, PROJ_CHUNK), lambda m, t=first + t: (layer, 0, t),
                          pipeline_mode=pl.Buffered(1)) for t in range(n)]
    return specs, [w] * n


def _inproj(h, weights, segs, gains, rope_tabs, seq_len, cache_slot=0, cache_prev=None, bm=INPROJ_ROWS):
    m_rows = h.shape[0]
    rope = rope_tabs is not None
    w_specs, w_args = _weight_tile_specs(weights)
    assert sum(s.width for s in segs) == len(w_args) * PROJ_CHUNK
    in_specs = [pl.BlockSpec((bm, D_MODEL), lambda m: (m, 0))] + w_specs
    args = [h] + w_args
    if rope:
        cos, sin_signed = rope_tabs
        blocks_per_seq = seq_len // bm
        tab_spec = pl.BlockSpec((bm, HEAD_DIM), lambda m: (m % blocks_per_seq, 0))
    for seg, g in zip(segs, gains):
        if seg.kind != "norm":
            continue
        if rope:
            g_swapped = g.reshape(2, 2, HEAD_DIM // 4)[:, ::-1].reshape(HEAD_DIM)
            in_specs += [tab_spec, tab_spec]
            args += [cos * g[None, :], sin_signed * g_swapped[None, :]]
        else:
            in_specs.append(pl.BlockSpec((1, HEAD_DIM), lambda m: (0, 0)))
            args.append(g.reshape(1, HEAD_DIM))
    n_prev = 0 if cache_prev is None else len(cache_prev)
    aliases = {}
    for k in range(n_prev):
        aliases[len(args)] = len(segs) + k
        in_specs.append(pl.BlockSpec(memory_space=pl.ANY))
        args.append(cache_prev[k])
    out_specs = [pl.BlockSpec((bm, s.width), lambda m: (m, 0)) for s in segs]
    out_shape = [jax.ShapeDtypeStruct((m_rows, s.width), BF16) for s in segs]
    for s in segs:
        if s.cache:
            nh, hw = s.cache
            if cache_prev is None:
                out_specs.append(pl.BlockSpec((bm // seq_len, CACHE_SLOTS, nh, seq_len, hw),
                                              lambda m: (m, 0, 0, 0, 0)))
            else:
                out_specs.append(pl.BlockSpec((bm // seq_len, None, nh, seq_len, hw),
                                              lambda m: (m, cache_slot, 0, 0, 0)))
            out_shape.append(jax.ShapeDtypeStruct((m_rows // seq_len, CACHE_SLOTS, nh, seq_len, hw), F32))
    outs = pl.pallas_call(
        functools.partial(_inproj_kernel, segs=segs, rope=rope, seq_len=seq_len, bm=bm, n_prev=n_prev,
                          first_fill=cache_slot if cache_prev is None else None, n_wtiles=len(w_args)),
        grid=(m_rows // bm,),
        in_specs=in_specs,
        out_specs=out_specs,
        out_shape=out_shape,
        input_output_aliases=aliases,
        compiler_params=_params(),
        name="inproj",
    )(*args)
    n = len(segs)
    return list(outs[:n]), list(outs[n:])


def _softmax_pv(scores, values, sinks=None, joint=False):
    groups = 1 if sinks is None else len(sinks)
    rows = scores[0].shape[0] // groups
    if joint or len(scores) == 1:
        probs = [[] for _ in scores]
        inv = []
        for g in range(groups):
            rs = slice(g * rows, (g + 1) * rows)
            m = jnp.max(scores[0][rs], axis=-1, keepdims=True)
            for s in scores[1:]:
                m = jnp.maximum(m, jnp.max(s[rs], axis=-1, keepdims=True))
            l = None
            if sinks is not None:
                m = jnp.maximum(m, sinks[g])
                l = jnp.exp2(sinks[g] - m)
            for i, s in enumerate(scores):
                e = jnp.exp2(s[rs] - m)
                part = jnp.sum(e, axis=-1, keepdims=True)
                l = part if l is None else l + part
                probs[i].append(e.astype(BF16))
            inv.append(1.0 / l)
        o = None
        for p, v in zip(probs, values):
            pv = _dot(p[0] if groups == 1 else jnp.concatenate(p, axis=0), v)
            o = pv if o is None else o + pv
        return [o[g * rows:(g + 1) * rows] * inv[g] for g in range(groups)]
    parts = []
    for s, v in zip(scores, values):
        m = jnp.max(s, axis=-1, keepdims=True)
        e = jnp.exp2(s - m)
        parts.append((m, jnp.sum(e, axis=-1, keepdims=True), _dot(e.astype(BF16), v)))
    outs = []
    for g in range(groups):
        rs = slice(g * rows, (g + 1) * rows)
        m_all = parts[0][0][rs]
        for m, _, _ in parts[1:]:
            m_all = jnp.maximum(m_all, m[rs])
        den = None
        if sinks is not None:
            m_all = jnp.maximum(m_all, sinks[g])
            den = jnp.exp2(sinks[g] - m_all)
        num = None
        for m, l, o in parts:
            w = jnp.exp2(m[rs] - m_all)
            den = l[rs] * w if den is None else den + l[rs] * w
            num = o[rs] * w if num is None else num + o[rs] * w
        outs.append(num * (1.0 / den))
    return outs


def _stack_heads(ref, r0, rows, first_head):
    return jnp.concatenate(
        [ref[pl.ds(r0, rows), (first_head + g) * HEAD_DIM:(first_head + g + 1) * HEAD_DIM]
         for g in range(GROUP)], axis=0)


def _head_sinks(sink_ref, first_head):
    return [sink_ref[first_head + g] * LOG2E for g in range(GROUP)]


def _key_chunks(scores, values, q, k_ref, v_ref, k0, seq_len, cols_k, cols_v, chunk=KEY_CHUNK):
    step = min(seq_len, chunk)
    for c in range(0, seq_len, step):
        scores.append(_dot_nt(q, k_ref[pl.ds(k0 + c, step), cols_k]))
        values.append(v_ref[pl.ds(k0 + c, step), cols_v])


def _store_gated(o_ref, g_ref, o, r0, rows, first_head):
    if len(o) == 1:
        o = [o[0][g * rows:(g + 1) * rows] for g in range(GROUP)]
    for g in range(GROUP):
        cs = slice((first_head + g) * HEAD_DIM, (first_head + g + 1) * HEAD_DIM)
        gate = g_ref[pl.ds(r0, rows), cs].astype(F32)
        o_ref[pl.ds(r0, rows), cs] = (o[g] * gate).astype(BF16)


def _gqa_kernel(*refs, seq_len, qb, n_blocks, ctx, sink):
    it = iter(refs)
    sink_ref = next(it) if sink else None
    q_ref, k_ref, v_ref, g_ref = next(it), next(it), next(it), next(it)
    if ctx:
        ck_ref, cv_ref = next(it), next(it)
    o_ref = next(it)
    if ctx:
        kc_s, vc_s = next(it), next(it)
        for kv in range(N_KV):
            kc_s[kv] = ck_ref[kv].astype(BF16)
            vc_s[kv] = cv_ref[kv].astype(BF16)
    blocks_per_seq = seq_len // qb

    def body(n, carry):
        r0 = pl.multiple_of(n * qb, qb)
        k0 = pl.multiple_of((n // blocks_per_seq) * seq_len, seq_len)
        for kv in range(N_KV):
            hs = slice(kv * HEAD_DIM, (kv + 1) * HEAD_DIM)
            q = _stack_heads(q_ref, r0, qb, kv * GROUP)
            scores, values = [], []
            if ctx:
                scores.append(_dot_nt(q, kc_s[kv]))
                values.append(vc_s[kv])
            _key_chunks(scores, values, q, k_ref, v_ref, k0, seq_len, hs, hs)
            sinks = _head_sinks(sink_ref, kv * GROUP) if sink else None
            o = _softmax_pv(scores, values, sinks)
            _store_gated(o_ref, g_ref, o, r0, qb, kv * GROUP)
        return carry

    lax.fori_loop(0, n_blocks, body, 0, unroll=UNROLL_CTX if ctx else UNROLL_SELF)


def _gqa(q, k, v, gate, seq_len, qb, rows_per_step, sink=None, ctx=None, layer_idx=0):
    m_rows = q.shape[0]
    kvw = N_KV * HEAD_DIM
    in_specs, args = [], []
    if sink is not None:
        in_specs.append(pl.BlockSpec(memory_space=pltpu.SMEM))
        args.append(sink)
    in_specs += [
        pl.BlockSpec((rows_per_step, HALF_W), lambda b: (b, 0)),
        pl.BlockSpec((rows_per_step, kvw), lambda b: (b, 0)),
        pl.BlockSpec((rows_per_step, kvw), lambda b: (b, 0)),
        pl.BlockSpec((rows_per_step, HALF_W), lambda b: (b, 0)),
    ]
    args += [q, k, v, gate]
    scratch = []
    if ctx is not None:
        past = ctx[0].shape[-2]
        cspec = pl.BlockSpec((None, None, N_KV, past, HEAD_DIM), lambda b: (b, layer_idx, 0, 0, 0))
        in_specs += [cspec, cspec]
        args += list(ctx)
        scratch = [pltpu.VMEM((N_KV, past, HEAD_DIM), BF16)] * 2
    return pl.pallas_call(
        functools.partial(_gqa_kernel, seq_len=seq_len, qb=qb, n_blocks=rows_per_step // qb,
                          ctx=ctx is not None, sink=sink is not None),
        grid=(m_rows // rows_per_step,),
        in_specs=in_specs,
        out_specs=pl.BlockSpec((rows_per_step, HALF_W), lambda b: (b, 0)),
        out_shape=jax.ShapeDtypeStruct((m_rows, HALF_W), BF16),
        scratch_shapes=scratch,
        compiler_params=_params(),
        name="gqa",
    )(*args)


def _banded_kernel(sink_ref, q_ref, k_ref, v_ref, g_ref, ck_ref, cv_ref, o_ref,
                   kpad, vpad, kc_s, vc_s, *, seq_len):
    n_blocks = seq_len // WINDOW
    zeros = jnp.zeros((WINDOW, HEAD_DIM), BF16)
    for kv in range(N_KV):
        hs = slice(kv * HEAD_DIM, (kv + 1) * HEAD_DIM)
        kpad[kv, 0:WINDOW] = zeros
        vpad[kv, 0:WINDOW] = zeros
        kpad[kv, WINDOW + seq_len:2 * WINDOW + seq_len] = zeros
        vpad[kv, WINDOW + seq_len:2 * WINDOW + seq_len] = zeros
        kpad[kv, WINDOW:WINDOW + seq_len] = k_ref[:, hs]
        vpad[kv, WINDOW:WINDOW + seq_len] = v_ref[:, hs]
        kc_s[kv] = ck_ref[kv].astype(BF16)
        vc_s[kv] = cv_ref[kv].astype(BF16)

    rows = GROUP * WINDOW
    qi = lax.broadcasted_iota(jnp.int32, (rows, 3 * WINDOW), 0) % WINDOW
    col = lax.broadcasted_iota(jnp.int32, (rows, 3 * WINDOW), 1)
    kj = col % WINDOW
    blk = col // WINDOW
    never = 4 * WINDOW

    def body(n, carry):
        r0 = pl.multiple_of(n * WINDOW, WINDOW)
        lo = jnp.where(n > 0, 0, never)
        hi = jnp.where(n < n_blocks - 1, 0, never)
        slack = jnp.where(blk == 0, kj - qi - lo, jnp.where(blk == 2, qi - kj - hi, 0))
        valid = slack >= 0
        for kv in range(N_KV):
            q = _stack_heads(q_ref, r0, WINDOW, kv * GROUP)
            kb = kpad[kv, pl.ds(r0, 3 * WINDOW), :]
            vb = vpad[kv, pl.ds(r0, 3 * WINDOW), :]
            s_ctx = _dot_nt(q, kc_s[kv])
            s_band = jnp.where(valid, _dot_nt(q, kb), NEG_INF)
            o = _softmax_pv([s_ctx, s_band], [vc_s[kv], vb], _head_sinks(sink_ref, kv * GROUP))
            _store_gated(o_ref, g_ref, o, r0, WINDOW, kv * GROUP)
        return carry

    lax.fori_loop(0, n_blocks, body, 0, unroll=UNROLL_CTX)


def _banded(q, k, v, gate, sink, cache_k, cache_v, layer_idx, seq_len):
    m_rows = q.shape[0]
    kvw = N_KV * HEAD_DIM
    past = cache_k.shape[-2]
    cspec = pl.BlockSpec((None, None, N_KV, past, HEAD_DIM), lambda b: (b, layer_idx, 0, 0, 0))
    pad_rows = seq_len + 2 * WINDOW
    return pl.pallas_call(
        functools.partial(_banded_kernel, seq_len=seq_len),
        grid=(m_rows // seq_len,),
        in_specs=[
            pl.BlockSpec(memory_space=pltpu.SMEM),
            pl.BlockSpec((seq_len, HALF_W), lambda b: (b, 0)),
            pl.BlockSpec((seq_len, kvw), lambda b: (b, 0)),
            pl.BlockSpec((seq_len, kvw), lambda b: (b, 0)),
            pl.BlockSpec((seq_len, HALF_W), lambda b: (b, 0)),
            cspec, cspec,
        ],
        out_specs=pl.BlockSpec((seq_len, HALF_W), lambda b: (b, 0)),
        out_shape=jax.ShapeDtypeStruct((m_rows, HALF_W), BF16),
        scratch_shapes=[
            pltpu.VMEM((N_KV, pad_rows, HEAD_DIM), BF16),
            pltpu.VMEM((N_KV, pad_rows, HEAD_DIM), BF16),
            pltpu.VMEM((N_KV, past, HEAD_DIM), BF16),
            pltpu.VMEM((N_KV, past, HEAD_DIM), BF16),
        ],
        compiler_params=_params(),
        name="banded",
    )(sink, q, k, v, gate, cache_k, cache_v)


def _diff_kernel(*refs, seq_len, qb, n_blocks, ctx, lam_init):
    it = iter(refs)
    lam_ref, on_ref = next(it), next(it)
    q_ref, k_ref, v_ref, g_ref = next(it), next(it), next(it), next(it)
    if ctx:
        ck_ref, cv_ref = next(it), next(it)
    o_ref = next(it)
    if ctx:
        kc_s, vc_s = next(it), next(it)
        for slot in range(2 * C_HEADS):
            kc_s[slot] = ck_ref[slot].astype(BF16)
        for hd in range(C_HEADS):
            vc_s[hd] = cv_ref[hd].astype(BF16)
    lf = lam_ref[...]
    lam = (jnp.exp(jnp.sum(lf[0:1] * lf[1:2], axis=-1, keepdims=True))
           - jnp.exp(jnp.sum(lf[2:3] * lf[3:4], axis=-1, keepdims=True)) + lam_init)
    out_gain = on_ref[...]
    blocks_per_seq = seq_len // qb

    def body(n, carry):
        r0 = pl.multiple_of(n * qb, qb)
        k0 = pl.multiple_of((n // blocks_per_seq) * seq_len, seq_len)
        for hd in range(C_HEADS):
            vs = slice(hd * C_V_DIM, (hd + 1) * C_V_DIM)
            maps = []
            for mp in range(2):
                slot = mp * C_HEADS + hd
                hs = slice(slot * HEAD_DIM, (slot + 1) * HEAD_DIM)
                q = q_ref[pl.ds(r0, qb), hs]
                scores, values = [], []
                if ctx:
                    scores.append(_dot_nt(q, kc_s[slot]))
                    values.append(vc_s[hd])
                _key_chunks(scores, values, q, k_ref, v_ref, k0, seq_len, hs, vs, chunk=seq_len)
                maps.append(_softmax_pv(scores, values, joint=True)[0])
            d = maps[0] - lam * maps[1]
            r = lax.rsqrt(jnp.mean(d * d, axis=-1, keepdims=True) + EPS)
            y = ((d * r) * out_gain) * (1.0 - lam_init)
            gate = g_ref[pl.ds(r0, qb), vs].astype(F32)
            o_ref[pl.ds(r0, qb), vs] = (y * gate).astype(BF16)
        return carry

    lax.fori_loop(0, n_blocks, body, 0, unroll=UNROLL_CTX)


def _diff(q, k, v, gate, c_lam, c_on, lam_init, seq_len, qb, rows_per_step, ctx=None, layer_idx=0):
    m_rows = q.shape[0]
    in_specs = [
        pl.BlockSpec((4, HEAD_DIM), lambda b: (0, 0)),
        pl.BlockSpec((1, C_V_DIM), lambda b: (0, 0)),
    ] + [pl.BlockSpec((rows_per_step, HALF_W), lambda b: (b, 0))] * 4
    args = [c_lam, c_on, q, k, v, gate]
    scratch = []
    if ctx is not None:
        past = ctx[0].shape[-2]
        in_specs += [
            pl.BlockSpec((None, None, 2 * C_HEADS, past, HEAD_DIM), lambda b: (b, layer_idx, 0, 0, 0)),
            pl.BlockSpec((None, None, C_HEADS, past, C_V_DIM), lambda b: (b, layer_idx, 0, 0, 0)),
        ]
        args += list(ctx)
        scratch = [pltpu.VMEM((2 * C_HEADS, past, HEAD_DIM), BF16),
                   pltpu.VMEM((C_HEADS, past, C_V_DIM), BF16)]
    return pl.pallas_call(
        functools.partial(_diff_kernel, seq_len=seq_len, qb=qb, n_blocks=rows_per_step // qb,
                          ctx=ctx is not None, lam_init=lam_init),
        grid=(m_rows // rows_per_step,),
        in_specs=in_specs,
        out_specs=pl.BlockSpec((rows_per_step, HALF_W), lambda b: (b, 0)),
        out_shape=jax.ShapeDtypeStruct((m_rows, HALF_W), BF16),
        scratch_shapes=scratch,
        compiler_params=_params(),
        name="diff_attn",
    )(*args)


def _outproj_kernel(*refs, bm, emit_h, conv_seq):
    it = iter(refs)
    y1_ref = next(it)
    if conv_seq is None:
        y2_ref = next(it)
    else:
        u_ref, b_ref, c_ref, g_ref = next(it), next(it), next(it), next(it)
        up_ref, cp_ref, un_ref, cn_ref = next(it), next(it), next(it), next(it)
        cw_ref = next(it)
    x_ref = next(it)
    w_refs = [next(it) for _ in range(D_MODEL // PROJ_CHUNK)]
    gt_ref = next(it)
    if emit_h:
        gn_ref, sh_ref, sc_ref = next(it), next(it), next(it)
    xo_ref = next(it)
    if emit_h:
        ho_ref = next(it)
    chunks = [slice(c, c + PROJ_CHUNK) for c in range(0, D_MODEL, PROJ_CHUNK)]
    w_top = [w.at[0:HALF_W, :] for w in w_refs]
    w_bot = [w.at[HALF_W:D_MODEL, :] for w in w_refs]
    if conv_seq is not None:
        assert bm % conv_seq == 0 or conv_seq % bm == 0
        row = lax.broadcasted_iota(jnp.int32, (bm, 1), 0)
        pos = (pl.program_id(0) * bm + row) % conv_seq
        first_row = jnp.logical_or(pos == 0, row == 0)
        last_row = jnp.logical_or(pos == conv_seq - 1, row == bm - 1)
        tile_pos = (pl.program_id(0) * bm) % conv_seq
        use_before = jnp.where(tile_pos == 0, 0.0, 1.0)
        use_after = jnp.where(tile_pos + bm == conv_seq, 0.0, 1.0) if conv_seq % bm == 0 else 0.0
        lanes = HALF_W // len(chunks)
        y2_parts = []
        for ci, cs in enumerate(chunks):
            ls = slice(ci * lanes, (ci + 1) * lanes)
            z = c_ref[:, ls].astype(F32) * u_ref[:, ls].astype(F32)
            z_before = (cp_ref[:, ls].astype(F32) * up_ref[:, ls].astype(F32))[HALO_ROWS - 1:HALO_ROWS] * use_before
            z_after = (cn_ref[:, ls].astype(F32) * un_ref[:, ls].astype(F32))[0:1] * use_after
            z_prev = jnp.where(first_row, z_before, pltpu.roll(z, 1, axis=0))
            z_next = jnp.where(last_row, z_after, pltpu.roll(z, bm - 1, axis=0))
            conv = z_prev * cw_ref[0:1, ls] + z * cw_ref[1:2, ls] + z_next * cw_ref[2:3, ls]
            y2_parts.append(((b_ref[:, ls].astype(F32) * conv) * g_ref[:, ls].astype(F32)).astype(BF16))
            xo_ref[:, cs] = x_ref[:, cs] + gt_ref[:, cs] * _dot(y1_ref[...], w_top[ci][...])
        y2 = jnp.concatenate(y2_parts, axis=1)
    ssq = jnp.zeros((bm, 1), F32)
    for ci, cs in enumerate(chunks):
        if conv_seq is None:
            acc = _dot(y1_ref[...], w_top[ci][...]) + _dot(y2_ref[...], w_bot[ci][...])
            xn = x_ref[:, cs] + gt_ref[:, cs] * acc
        else:
            xn = xo_ref[:, cs] + gt_ref[:, cs] * _dot(y2, w_bot[ci][...])
        xo_ref[:, cs] = xn
        if emit_h:
            ssq = ssq + jnp.sum(xn * xn, axis=-1, keepdims=True)
    if emit_h:
        r = lax.rsqrt(ssq * (1.0 / D_MODEL) + EPS)
        for cs in chunks:
            gain = gn_ref[:, cs] * (1.0 + sc_ref[:, cs])
            ho_ref[:, cs] = ((xo_ref[:, cs] * r) * gain + sh_ref[:, cs]).astype(BF16)


def _outproj(y1, y2, x, w_out, norm_g3, mod4, layer, row_fn, emit_h, conv=None):
    m_rows = x.shape[0]
    bm = PROJ_ROWS
    half_spec = pl.BlockSpec((bm, HALF_W), lambda m: (m, 0))
    in_specs = [half_spec]
    args = [y1]
    scratch = []
    if conv is None:
        in_specs.append(half_spec)
        args.append(y2)
    else:
        u, bgate, cgate, gate, conv_w, _ = conv
        halo_blocks = bm // HALO_ROWS
        last_halo = m_rows // HALO_ROWS - 1
        before = pl.BlockSpec((HALO_ROWS, HALF_W), lambda m: (jnp.maximum(m * halo_blocks - 1, 0), 0))
        after = pl.BlockSpec((HALO_ROWS, HALF_W), lambda m: (jnp.minimum((m + 1) * halo_blocks, last_halo), 0))
        in_specs += [half_spec] * 4 + [before, before, after, after,
                                       pl.BlockSpec((3, HALF_W), lambda m: (0, 0))]
        args += [u, bgate, cgate, gate, u, cgate, u, cgate, conv_w]
    w_specs, w_args = _weight_tile_specs(w_out)
    in_specs += [pl.BlockSpec((bm, D_MODEL), lambda m: (m, 0))] + w_specs + [_mod_spec(layer, 2, row_fn)]
    args += [x] + w_args + [mod4]
    out_specs = [pl.BlockSpec((bm, D_MODEL), lambda m: (m, 0))]
    out_shape = [jax.ShapeDtypeStruct((m_rows, D_MODEL), F32)]
    if emit_h:
        in_specs += [
            pl.BlockSpec((None, 1, D_MODEL), lambda m: (layer + 1, 0, 0)),
            _mod_spec(layer + 1, 0, row_fn),
            _mod_spec(layer + 1, 1, row_fn),
        ]
        args += [norm_g3, mod4, mod4]
        out_specs.append(pl.BlockSpec((bm, D_MODEL), lambda m: (m, 0)))
        out_shape.append(jax.ShapeDtypeStruct((m_rows, D_MODEL), BF16))
    outs = pl.pallas_call(
        functools.partial(_outproj_kernel, bm=bm, emit_h=emit_h,
                          conv_seq=None if conv is None else conv[5]),
        grid=(m_rows // bm,),
        in_specs=in_specs,
        out_specs=out_specs,
        out_shape=out_shape,
        scratch_shapes=scratch,
        compiler_params=_params(),
        name="outproj",
    )(*args)
    return (outs[0], outs[1]) if emit_h else (outs[0], None)


def _rope_tables(n_tokens):
    t = jnp.arange(n_tokens)
    row = t // GRID_W
    col = t % GRID_W
    n_freq = HEAD_DIM // 4
    inv = ROPE_THETA ** (-jnp.arange(n_freq, dtype=F32) / n_freq)
    ang_r = row[:, None] * inv
    ang_c = col[:, None] * inv
    cos = jnp.concatenate([jnp.cos(ang_r)] * 2 + [jnp.cos(ang_c)] * 2, axis=1)
    sin = jnp.concatenate([-jnp.sin(ang_r), jnp.sin(ang_r), -jnp.sin(ang_c), jnp.sin(ang_c)], axis=1)
    return cos, sin


def kernel(x_prompt, x_sample, cache_a_k, cache_a_v, cache_c_k, cache_c_v, cache_d_k, cache_d_v,
           c, c_ctx, norm_g, ada_w, ada_b, ev_w_in, ev_w_out, a_q_norm, a_k_norm, a_sink, b_conv,
           od_w_in, od_w_out, c_q_norm, c_k_norm, c_lambda, c_out_norm, d_q_norm, d_k_norm):
    n_prompt, seq_p, _ = x_prompt.shape
    n_sample, seq_s, _ = x_sample.shape
    past = cache_a_k.shape[-2]
    scale = HEAD_DIM ** -0.5 * LOG2E

    cond = jnp.concatenate(
        [c, c_ctx[None, :], jnp.zeros((COND_ROWS - n_sample - 1, D_MODEL), F32)], axis=0)
    mod4 = _modulation(cond, ada_w, ada_b).reshape(DEPTH, COND_ROWS, 1, 3 * D_MODEL)
    norm_g3 = norm_g.reshape(DEPTH, 1, D_MODEL)
    rope_tabs = _rope_tables(seq_s)

    prompt_row = lambda m: CTX_ROW
    sample_row = lambda m: (m * PROJ_ROWS) // seq_s

    xp = x_prompt.reshape(n_prompt * seq_p, D_MODEL)
    xs = x_sample.reshape(n_sample * seq_s, D_MODEL)
    hp = _prenorm(xp, norm_g3, mod4, 0, prompt_row)
    hs = _prenorm(xs, norm_g3, mod4, 0, sample_row)

    ev_in_tiles, ev_out_tiles = ev_w_in.astype(BF16), ev_w_out.astype(BF16)
    od_in_tiles, od_out_tiles = od_w_in.astype(BF16), od_w_out.astype(BF16)
    cache_c_k8 = cache_c_k.reshape(n_sample, DEPTH // 2, 2 * C_HEADS, past, HEAD_DIM)
    prompt_rows = PROMPT_SEQS_PER_STEP * seq_p

    new_a = None
    new_c = (None, None)
    for layer in range(DEPTH):
        i = layer // 2
        last = layer == DEPTH - 1
        if layer % 2 == 0:
            w_out = (ev_out_tiles, i, 0, D_MODEL // PROJ_CHUNK)
            gq = a_q_norm[i] * scale
            gk = a_k_norm[i]
            kv_cache = (N_KV, HEAD_DIM)

            def project(h, seq_len, tabs, cache, prev=None):
                segs1 = [_Seg("norm", HALF_W), _Seg("norm", N_KV * HEAD_DIM, cache),
                         _Seg("plain", N_KV * HEAD_DIM, cache), _Seg("silu", HALF_W)]
                segs_all = segs1 + [_Seg("plain", HALF_W)] * 3 + [_Seg("silu", HALF_W)]
                outs, caches = _inproj(h, (ev_in_tiles, i, 0, ALL_TILES), segs_all, [gq, gk] + [None] * 6,
                                       tabs, seq_len, cache_slot=i, cache_prev=prev, bm=PROJ_ROWS)
                return (*outs, caches)

            q, k, v, ga, u, bg, cg, gb, new_a = project(hp, seq_p, None, kv_cache, new_a)
            y1p = _gqa(q, k, v, ga, seq_p, seq_p, prompt_rows, sink=a_sink[i])
            y2p, conv_p = None, (u, bg, cg, gb, b_conv[i], seq_p)

            q, k, v, ga, u, bg, cg, gb, _ = project(hs, seq_s, rope_tabs, None)
            y1s = _banded(q, k, v, ga, a_sink[i], cache_a_k, cache_a_v, i, seq_s)
            y2s, conv_s = None, (u, bg, cg, gb, b_conv[i], seq_s)
        else:
            lam_init = 0.8 - 0.6 * math.exp(-0.3 * layer)
            w_out = (od_out_tiles, i, 0, D_MODEL // PROJ_CHUNK)
            gcq = c_q_norm[i] * scale
            gck = c_k_norm[i]
            gdq = d_q_norm[i] * scale
            gdk = d_k_norm[i]
            c_on = c_out_norm[i].reshape(1, C_V_DIM)

            def project(h, seq_len, tabs, with_cache, prev=(None, None)):
                ck_cache = (2 * C_HEADS, HEAD_DIM) if with_cache else None
                cv_cache = (C_HEADS, C_V_DIM) if with_cache else None
                kv_cache = (N_KV, HEAD_DIM) if with_cache else None
                segs_c = [_Seg("norm", HALF_W), _Seg("norm", HALF_W, ck_cache),
                          _Seg("plain", HALF_W, cv_cache), _Seg("silu", HALF_W)]
                segs_d = [_Seg("norm", HALF_W), _Seg("norm", N_KV * HEAD_DIM, kv_cache),
                          _Seg("plain", N_KV * HEAD_DIM, kv_cache), _Seg("silu", HALF_W)]
                gains_c, gains_d = [gcq, gck, None, None], [gdq, gdk, None, None]
                if not with_cache:
                    outs, _ = _inproj(h, (od_in_tiles, i, 0, ALL_TILES), segs_c + segs_d, gains_c + gains_d,
                                      tabs, seq_len, bm=PROJ_ROWS)
                    return (*outs, None)
                outs_c, cc = _inproj(h, (od_in_tiles, i, 0, C_TILES), segs_c, gains_c, tabs, seq_len,
                                     cache_slot=i, cache_prev=prev[0], bm=PROJ_ROWS)
                outs_d, cd = _inproj(h, (od_in_tiles, i, C_TILES, ALL_TILES - C_TILES), segs_d, gains_d,
                                     tabs, seq_len, cache_slot=i, cache_prev=prev[1])
                return (*outs_c, *outs_d, (cc, cd))

            cq, ck, cv, cgate, dq, dk, dv, dgate, new_c = project(hp, seq_p, None, True, new_c)
            y1p = _diff(cq, ck, cv, cgate, c_lambda[i], c_on, lam_init, seq_p, seq_p, prompt_rows)
            y2p = _gqa(dq, dk, dv, dgate, seq_p, seq_p, prompt_rows)

            cq, ck, cv, cgate, dq, dk, dv, dgate, _ = project(hs, seq_s, rope_tabs, False)
            y1s = _diff(cq, ck, cv, cgate, c_lambda[i], c_on, lam_init, seq_s, 2 * SAMPLE_QB, seq_s,
                        ctx=(cache_c_k8, cache_c_v), layer_idx=i)
            y2s = _gqa(dq, dk, dv, dgate, seq_s, 2 * SAMPLE_QB, seq_s,
                       ctx=(cache_d_k, cache_d_v), layer_idx=i)
            conv_p = conv_s = None

        xp, hp = _outproj(y1p, y2p, xp, w_out, norm_g3, mod4, layer, prompt_row, not last, conv_p)
        xs, hs = _outproj(y1s, y2s, xs, w_out, norm_g3, mod4, layer, sample_row, not last, conv_s)

    return (xp.reshape(n_prompt, seq_p, D_MODEL), xs.reshape(n_sample, seq_s, D_MODEL),
            new_a[0], new_a[1],
            new_c[0][0].reshape(n_prompt, CACHE_SLOTS, 2, C_HEADS, seq_p, HEAD_DIM), new_c[0][1],
            new_c[1][0], new_c[1][1])
```

```python
import functools
import math

import jax
import jax.numpy as jnp
from jax import lax
from jax.experimental import pallas as pl
from jax.experimental.pallas import tpu as pltpu

D_MODEL = 2048
DEPTH = 4
GRID_W = 64
HEAD_DIM = 128
WINDOW = 128
ROPE_THETA = 10000.0
EPS = 1e-6
NEG_INF = -1e30
HALF_W = D_MODEL // 2
N_KV = 2
GROUP = 4
C_HEADS = 4
C_V_DIM = 2 * HEAD_DIM
CACHE_SLOTS = DEPTH // 2
COND_ROWS = 16
CTX_ROW = 8

BF16 = jnp.bfloat16
F32 = jnp.float32

VMEM_LIMIT_BYTES = 52 * 1024 * 1024
PROJ_ROWS = 512
INPROJ_ROWS = 1024
ALL_TILES = 13
C_TILES = 8
PROJ_CHUNK = 512
SAMPLE_QB = 128
KEY_CHUNK = 512
UNROLL_SELF = 4
UNROLL_CTX = 2
HALO_ROWS = 16
LOG2E = math.log2(math.e)
PROMPT_SEQS_PER_STEP = 4


def _params(n_axes=1, flags=None):
    return pltpu.CompilerParams(
        dimension_semantics=("arbitrary",) * n_axes,
        vmem_limit_bytes=VMEM_LIMIT_BYTES,
        flags=flags,
    )


def _dot(a, b):
    return jnp.dot(a, b, preferred_element_type=F32)


def _dot_nt(a, b):
    return lax.dot_general(a, b, (((1,), (1,)), ((), ())), preferred_element_type=F32)


def _silu(x):
    return x / (1.0 + jnp.exp(-x))


def _mod_kernel(cond_ref, w_ref, b_ref, o_ref):
    s = _silu(cond_ref[...]).astype(BF16)
    o_ref[...] = _dot(s, w_ref[...].astype(BF16)) + b_ref[...]


def _modulation(cond, ada_w, ada_b):
    bn = 1024
    n_out = 3 * D_MODEL
    return pl.pallas_call(
        _mod_kernel,
        grid=(DEPTH, n_out // bn),
        in_specs=[
            pl.BlockSpec((COND_ROWS, D_MODEL), lambda l, n: (0, 0)),
            pl.BlockSpec((None, D_MODEL, bn), lambda l, n: (l, 0, n)),
            pl.BlockSpec((None, 1, bn), lambda l, n: (l, 0, n)),
        ],
        out_specs=pl.BlockSpec((None, COND_ROWS, bn), lambda l, n: (l, 0, n)),
        out_shape=jax.ShapeDtypeStruct((DEPTH, COND_ROWS, n_out), F32),
        compiler_params=_params(2),
        name="modulation",
    )(cond, ada_w, ada_b.reshape(DEPTH, 1, n_out))


def _mod_spec(layer, part, row_fn):
    return pl.BlockSpec((None, None, 1, D_MODEL), lambda m: (layer, row_fn(m), 0, part))


def _norm_mod(x, r, g, sc, sh):
    return ((x * r) * g) * (1.0 + sc) + sh


def _prenorm_kernel(x_ref, g_ref, sh_ref, sc_ref, h_ref):
    x = x_ref[...]
    r = lax.rsqrt(jnp.mean(x * x, axis=-1, keepdims=True) + EPS)
    h_ref[...] = _norm_mod(x, r, g_ref[...], sc_ref[...], sh_ref[...]).astype(BF16)


def _prenorm(x, norm_g3, mod4, layer, row_fn):
    m_rows = x.shape[0]
    bm = PROJ_ROWS
    return pl.pallas_call(
        _prenorm_kernel,
        grid=(m_rows // bm,),
        in_specs=[
            pl.BlockSpec((bm, D_MODEL), lambda m: (m, 0)),
            pl.BlockSpec((None, 1, D_MODEL), lambda m: (layer, 0, 0)),
            _mod_spec(layer, 0, row_fn),
            _mod_spec(layer, 1, row_fn),
        ],
        out_specs=pl.BlockSpec((bm, D_MODEL), lambda m: (m, 0)),
        out_shape=jax.ShapeDtypeStruct((m_rows, D_MODEL), BF16),
        compiler_params=_params(),
        name="prenorm",
    )(x, norm_g3, mod4, mod4)


class _Seg:
    def __init__(self, kind, width, cache=None):
        self.kind = kind
        self.width = width
        self.cache = cache


def _swap_halves(a, first_half):
    return jnp.where(first_half, pltpu.roll(a, 96, axis=1), pltpu.roll(a, 32, axis=1))


def _inproj_kernel(*refs, segs, rope, seq_len, bm, n_prev, first_fill, n_wtiles):
    it = iter(refs)
    h_ref = next(it)
    w_refs = [next(it) for _ in range(n_wtiles)]
    gain_refs = [tuple(next(it) for _ in range(2 if rope else 1)) if s.kind == "norm" else None
                 for s in segs]
    for _ in range(n_prev):
        next(it)
    out_refs = [next(it) for _ in segs]
    cache_refs = [next(it) if s.cache else None for s in segs]

    if rope:
        lane = lax.broadcasted_iota(jnp.int32, (bm, HEAD_DIM), 1)
        first_half = (lane % 64) < 32

    chunks = []
    col = 0
    for si, seg in enumerate(segs):
        for c0 in range(0, seg.width, PROJ_CHUNK):
            chunks.append((si, c0, min(PROJ_CHUNK, seg.width - c0), col + c0))
        col += seg.width

    def matmul(chunk):
        _, _, cw, wc = chunk
        off = wc % PROJ_CHUNK
        return _dot(h_ref[...], w_refs[wc // PROJ_CHUNK][:, off:off + cw])

    def epilogue(chunk, acc):
        si, c0, cw, _ = chunk
        seg, gain_ref, out_ref, cache_ref = segs[si], gain_refs[si], out_refs[si], cache_refs[si]
        if seg.kind == "norm":
            heads = []
            for hh in range(cw // HEAD_DIM):
                a = acc[:, hh * HEAD_DIM:(hh + 1) * HEAD_DIM]
                r = lax.rsqrt(jnp.mean(a * a, axis=-1, keepdims=True) + EPS)
                if rope:
                    y = (a * gain_ref[0][...] + _swap_halves(a, first_half) * gain_ref[1][...]) * r
                else:
                    y = (a * r) * gain_ref[0][...]
                heads.append(y)
            val = jnp.concatenate(heads, axis=1) if len(heads) > 1 else heads[0]
        elif seg.kind == "silu":
            val = _silu(acc)
        else:
            val = acc
        out_ref[:, c0:c0 + cw] = val.astype(BF16)
        if seg.cache:
            _, hw = seg.cache
            for j in range(bm // seq_len):
                for hh in range(cw // hw):
                    head = (c0 // hw) + hh
                    data = val[j * seq_len:(j + 1) * seq_len, hh * hw:(hh + 1) * hw]
                    if first_fill is None:
                        cache_ref[j, head] = data
                    else:
                        for slot in range(CACHE_SLOTS):
                            cache_ref[j, slot, head] = data if slot == first_fill else jnp.zeros_like(data)

    acc = matmul(chunks[0])
    for idx, chunk in enumerate(chunks):
        nxt = matmul(chunks[idx + 1]) if idx + 1 < len(chunks) else None
        epilogue(chunk, acc)
        acc = nxt


def _weight_tile_specs(weights):
    w, layer, first, n = weights
    specs = [pl.BlockSpec((None, D_MODEL, PROJ_CHUNK), lambda m, t=first + t: (layer, 0, t),
                          pipeline_mode=pl.Buffered(1)) for t in range(n)]
    return specs, [w] * n


def _inproj(h, weights, segs, gains, rope_tabs, seq_len, cache_slot=0, cache_prev=None, bm=INPROJ_ROWS):
    m_rows = h.shape[0]
    rope = rope_tabs is not None
    w_specs, w_args = _weight_tile_specs(weights)
    assert sum(s.width for s in segs) == len(w_args) * PROJ_CHUNK
    in_specs = [pl.BlockSpec((bm, D_MODEL), lambda m: (m, 0))] + w_specs
    args = [h] + w_args
    if rope:
        cos, sin_signed = rope_tabs
        blocks_per_seq = seq_len // bm
        tab_spec = pl.BlockSpec((bm, HEAD_DIM), lambda m: (m % blocks_per_seq, 0))
    for seg, g in zip(segs, gains):
        if seg.kind != "norm":
            continue
        if rope:
            g_swapped = g.reshape(2, 2, HEAD_DIM // 4)[:, ::-1].reshape(HEAD_DIM)
            in_specs += [tab_spec, tab_spec]
            args += [cos * g[None, :], sin_signed * g_swapped[None, :]]
        else:
            in_specs.append(pl.BlockSpec((1, HEAD_DIM), lambda m: (0, 0)))
            args.append(g.reshape(1, HEAD_DIM))
    n_prev = 0 if cache_prev is None else len(cache_prev)
    aliases = {}
    for k in range(n_prev):
        aliases[len(args)] = len(segs) + k
        in_specs.append(pl.BlockSpec(memory_space=pl.ANY))
        args.append(cache_prev[k])
    out_specs = [pl.BlockSpec((bm, s.width), lambda m: (m, 0)) for s in segs]
    out_shape = [jax.ShapeDtypeStruct((m_rows, s.width), BF16) for s in segs]
    for s in segs:
        if s.cache:
            nh, hw = s.cache
            if cache_prev is None:
                out_specs.append(pl.BlockSpec((bm // seq_len, CACHE_SLOTS, nh, seq_len, hw),
                                              lambda m: (m, 0, 0, 0, 0)))
            else:
                out_specs.append(pl.BlockSpec((bm // seq_len, None, nh, seq_len, hw),
                                              lambda m: (m, cache_slot, 0, 0, 0)))
            out_shape.append(jax.ShapeDtypeStruct((m_rows // seq_len, CACHE_SLOTS, nh, seq_len, hw), F32))
    outs = pl.pallas_call(
        functools.partial(_inproj_kernel, segs=segs, rope=rope, seq_len=seq_len, bm=bm, n_prev=n_prev,
                          first_fill=cache_slot if cache_prev is None else None, n_wtiles=len(w_args)),
        grid=(m_rows // bm,),
        in_specs=in_specs,
        out_specs=out_specs,
        out_shape=out_shape,
        input_output_aliases=aliases,
        compiler_params=_params(),
        name="inproj",
    )(*args)
    n = len(segs)
    return list(outs[:n]), list(outs[n:])


def _softmax_pv(scores, values, sinks=None, joint=False, ones_cols=False):
    groups = 1 if sinks is None else len(sinks)
    rows = scores[0].shape[0] // groups
    dv = values[0].shape[1] - (HEAD_DIM if ones_cols else 0)

    def weighted(e, v):
        pv = _dot(e, v)
        return (pv[:, dv:], pv[:, :dv]) if ones_cols else (None, pv)

    if joint or len(scores) == 1:
        probs = [[] for _ in scores]
        sink_terms, sums = [], []
        for g in range(groups):
            rs = slice(g * rows, (g + 1) * rows)
            m = jnp.max(scores[0][rs], axis=-1, keepdims=True)
            for s in scores[1:]:
                m = jnp.maximum(m, jnp.max(s[rs], axis=-1, keepdims=True))
            if sinks is not None:
                m = jnp.maximum(m, sinks[g])
            sink_terms.append(None if sinks is None else jnp.exp2(sinks[g] - m))
            l = None
            for i, s in enumerate(scores):
                e = jnp.exp2(s[rs] - m)
                if not ones_cols:
                    part = jnp.sum(e, axis=-1, keepdims=True)
                    l = part if l is None else l + part
                probs[i].append(e.astype(BF16))
            sums.append(l)
        o, l_all = None, None
        for p, v in zip(probs, values):
            l_part, pv = weighted(p[0] if groups == 1 else jnp.concatenate(p, axis=0), v)
            o = pv if o is None else o + pv
            if ones_cols:
                l_all = l_part if l_all is None else l_all + l_part
        outs = []
        for g in range(groups):
            rs = slice(g * rows, (g + 1) * rows)
            l = l_all[rs] if ones_cols else sums[g]
            if sink_terms[g] is not None:
                l = l + sink_terms[g]
            outs.append(o[rs] * (1.0 / l))
        return outs
    parts = []
    for s, v in zip(scores, values):
        m = jnp.max(s, axis=-1, keepdims=True)
        e = jnp.exp2(s - m)
        l, pv = weighted(e.astype(BF16), v)
        parts.append((m, jnp.sum(e, axis=-1, keepdims=True) if l is None else l, pv))
    outs = []
    for g in range(groups):
        rs = slice(g * rows, (g + 1) * rows)
        m_all = parts[0][0][rs]
        for m, _, _ in parts[1:]:
            m_all = jnp.maximum(m_all, m[rs])
        den = None
        if sinks is not None:
            m_all = jnp.maximum(m_all, sinks[g])
            den = jnp.exp2(sinks[g] - m_all)
        num = None
        for m, l, o in parts:
            w = jnp.exp2(m[rs] - m_all)
            den = l[rs] * w if den is None else den + l[rs] * w
            num = o[rs] * w if num is None else num + o[rs] * w
        outs.append(num * (1.0 / den))
    return outs


def _stack_heads(ref, r0, rows, first_head):
    return jnp.concatenate(
        [ref[pl.ds(r0, rows), (first_head + g) * HEAD_DIM:(first_head + g + 1) * HEAD_DIM]
         for g in range(GROUP)], axis=0)


def _head_sinks(sink_ref, first_head):
    return [sink_ref[first_head + g] * LOG2E for g in range(GROUP)]


def _key_chunks(scores, values, q, k_ref, v_ref, k0, seq_len, cols_k, cols_v, chunk=KEY_CHUNK,
                ones_cols=False):
    step = min(seq_len, chunk)
    for c in range(0, seq_len, step):
        scores.append(_dot_nt(q, k_ref[pl.ds(k0 + c, step), cols_k]))
        v = v_ref[pl.ds(k0 + c, step), cols_v]
        if ones_cols:
            v = jnp.concatenate([v, jnp.ones((step, HEAD_DIM), BF16)], axis=1)
        values.append(v)


def _store_gated(o_ref, g_ref, o, r0, rows, first_head):
    if len(o) == 1:
        o = [o[0][g * rows:(g + 1) * rows] for g in range(GROUP)]
    for g in range(GROUP):
        cs = slice((first_head + g) * HEAD_DIM, (first_head + g + 1) * HEAD_DIM)
        gate = g_ref[pl.ds(r0, rows), cs].astype(F32)
        o_ref[pl.ds(r0, rows), cs] = (o[g] * gate).astype(BF16)


def _gqa_kernel(*refs, seq_len, qb, n_blocks, ctx, sink):
    it = iter(refs)
    sink_ref = next(it) if sink else None
    q_ref, k_ref, v_ref, g_ref = next(it), next(it), next(it), next(it)
    if ctx:
        ck_ref, cv_ref = next(it), next(it)
    o_ref = next(it)
    if ctx:
        kc_s, vc_s = next(it), next(it)
        for kv in range(N_KV):
            kc_s[kv] = ck_ref[kv].astype(BF16)
            vc_s[kv, :, 0:HEAD_DIM] = cv_ref[kv].astype(BF16)
            vc_s[kv, :, HEAD_DIM:2 * HEAD_DIM] = jnp.ones(cv_ref.shape[1:], BF16)
    blocks_per_seq = seq_len // qb

    def body(n, carry):
        r0 = pl.multiple_of(n * qb, qb)
        k0 = pl.multiple_of((n // blocks_per_seq) * seq_len, seq_len)
        for kv in range(N_KV):
            hs = slice(kv * HEAD_DIM, (kv + 1) * HEAD_DIM)
            q = _stack_heads(q_ref, r0, qb, kv * GROUP)
            scores, values = [], []
            if ctx:
                scores.append(_dot_nt(q, kc_s[kv]))
                values.append(vc_s[kv])
            _key_chunks(scores, values, q, k_ref, v_ref, k0, seq_len, hs, hs, ones_cols=True)
            sinks = _head_sinks(sink_ref, kv * GROUP) if sink else None
            o = _softmax_pv(scores, values, sinks, ones_cols=True)
            _store_gated(o_ref, g_ref, o, r0, qb, kv * GROUP)
        return carry

    lax.fori_loop(0, n_blocks, body, 0, unroll=UNROLL_CTX if ctx else UNROLL_SELF)


def _gqa(q, k, v, gate, seq_len, qb, rows_per_step, sink=None, ctx=None, layer_idx=0):
    m_rows = q.shape[0]
    kvw = N_KV * HEAD_DIM
    in_specs, args = [], []
    if sink is not None:
        in_specs.append(pl.BlockSpec(memory_space=pltpu.SMEM))
        args.append(sink)
    in_specs += [
        pl.BlockSpec((rows_per_step, HALF_W), lambda b: (b, 0)),
        pl.BlockSpec((rows_per_step, kvw), lambda b: (b, 0)),
        pl.BlockSpec((rows_per_step, kvw), lambda b: (b, 0)),
        pl.BlockSpec((rows_per_step, HALF_W), lambda b: (b, 0)),
    ]
    args += [q, k, v, gate]
    scratch = []
    if ctx is not None:
        past = ctx[0].shape[-2]
        cspec = pl.BlockSpec((None, None, N_KV, past, HEAD_DIM), lambda b: (b, layer_idx, 0, 0, 0))
        in_specs += [cspec, cspec]
        args += list(ctx)
        scratch = [pltpu.VMEM((N_KV, past, HEAD_DIM), BF16), pltpu.VMEM((N_KV, past, 2 * HEAD_DIM), BF16)]
    return pl.pallas_call(
        functools.partial(_gqa_kernel, seq_len=seq_len, qb=qb, n_blocks=rows_per_step // qb,
                          ctx=ctx is not None, sink=sink is not None),
        grid=(m_rows // rows_per_step,),
        in_specs=in_specs,
        out_specs=pl.BlockSpec((rows_per_step, HALF_W), lambda b: (b, 0)),
        out_shape=jax.ShapeDtypeStruct((m_rows, HALF_W), BF16),
        scratch_shapes=scratch,
        compiler_params=_params(),
        name="gqa",
    )(*args)


def _banded_kernel(sink_ref, q_ref, k_ref, v_ref, g_ref, ck_ref, cv_ref, o_ref,
                   kpad, vpad, kc_s, vc_s, *, seq_len):
    n_blocks = seq_len // WINDOW
    zeros = jnp.zeros((WINDOW, HEAD_DIM), BF16)
    for kv in range(N_KV):
        hs = slice(kv * HEAD_DIM, (kv + 1) * HEAD_DIM)
        kpad[kv, 0:WINDOW] = zeros
        vpad[kv, 0:WINDOW, 0:HEAD_DIM] = zeros
        kpad[kv, WINDOW + seq_len:2 * WINDOW + seq_len] = zeros
        vpad[kv, WINDOW + seq_len:2 * WINDOW + seq_len, 0:HEAD_DIM] = zeros
        kpad[kv, WINDOW:WINDOW + seq_len] = k_ref[:, hs]
        vpad[kv, WINDOW:WINDOW + seq_len, 0:HEAD_DIM] = v_ref[:, hs]
        kc_s[kv] = ck_ref[kv].astype(BF16)
        vc_s[kv, :, 0:HEAD_DIM] = cv_ref[kv].astype(BF16)
        vpad[kv, :, HEAD_DIM:2 * HEAD_DIM] = jnp.ones((seq_len + 2 * WINDOW, HEAD_DIM), BF16)
        vc_s[kv, :, HEAD_DIM:2 * HEAD_DIM] = jnp.ones(cv_ref.shape[1:], BF16)

    rows = GROUP * WINDOW
    qi = lax.broadcasted_iota(jnp.int32, (rows, 3 * WINDOW), 0) % WINDOW
    col = lax.broadcasted_iota(jnp.int32, (rows, 3 * WINDOW), 1)
    kj = col % WINDOW
    blk = col // WINDOW
    never = 4 * WINDOW

    def body(n, carry):
        r0 = pl.multiple_of(n * WINDOW, WINDOW)
        lo = jnp.where(n > 0, 0, never)
        hi = jnp.where(n < n_blocks - 1, 0, never)
        slack = jnp.where(blk == 0, kj - qi - lo, jnp.where(blk == 2, qi - kj - hi, 0))
        valid = slack >= 0
        for kv in range(N_KV):
            q = _stack_heads(q_ref, r0, WINDOW, kv * GROUP)
            kb = kpad[kv, pl.ds(r0, 3 * WINDOW), :]
            vb = vpad[kv, pl.ds(r0, 3 * WINDOW), :]
            s_ctx = _dot_nt(q, kc_s[kv])
            s_band = jnp.where(valid, _dot_nt(q, kb), NEG_INF)
            o = _softmax_pv([s_ctx, s_band], [vc_s[kv], vb], _head_sinks(sink_ref, kv * GROUP),
                            ones_cols=True)
            _store_gated(o_ref, g_ref, o, r0, WINDOW, kv * GROUP)
        return carry

    lax.fori_loop(0, n_blocks, body, 0, unroll=UNROLL_CTX)


def _banded(q, k, v, gate, sink, cache_k, cache_v, layer_idx, seq_len):
    m_rows = q.shape[0]
    kvw = N_KV * HEAD_DIM
    past = cache_k.shape[-2]
    cspec = pl.BlockSpec((None, None, N_KV, past, HEAD_DIM), lambda b: (b, layer_idx, 0, 0, 0))
    pad_rows = seq_len + 2 * WINDOW
    return pl.pallas_call(
        functools.partial(_banded_kernel, seq_len=seq_len),
        grid=(m_rows // seq_len,),
        in_specs=[
            pl.BlockSpec(memory_space=pltpu.SMEM),
            pl.BlockSpec((seq_len, HALF_W), lambda b: (b, 0)),
            pl.BlockSpec((seq_len, kvw), lambda b: (b, 0)),
            pl.BlockSpec((seq_len, kvw), lambda b: (b, 0)),
            pl.BlockSpec((seq_len, HALF_W), lambda b: (b, 0)),
            cspec, cspec,
        ],
        out_specs=pl.BlockSpec((seq_len, HALF_W), lambda b: (b, 0)),
        out_shape=jax.ShapeDtypeStruct((m_rows, HALF_W), BF16),
        scratch_shapes=[
            pltpu.VMEM((N_KV, pad_rows, HEAD_DIM), BF16),
            pltpu.VMEM((N_KV, pad_rows, 2 * HEAD_DIM), BF16),
            pltpu.VMEM((N_KV, past, HEAD_DIM), BF16),
            pltpu.VMEM((N_KV, past, 2 * HEAD_DIM), BF16),
        ],
        compiler_params=_params(),
        name="banded",
    )(sink, q, k, v, gate, cache_k, cache_v)


def _diff_kernel(*refs, seq_len, qb, n_blocks, ctx, lam_init):
    it = iter(refs)
    lam_ref, on_ref = next(it), next(it)
    q_ref, k_ref, v_ref, g_ref = next(it), next(it), next(it), next(it)
    if ctx:
        ck_ref, cv_ref = next(it), next(it)
    o_ref = next(it)
    if ctx:
        kc_s, vc_s = next(it), next(it)
        for slot in range(2 * C_HEADS):
            kc_s[slot] = ck_ref[slot].astype(BF16)
        for hd in range(C_HEADS):
            vc_s[hd] = cv_ref[hd].astype(BF16)
    lf = lam_ref[...]
    lam = (jnp.exp(jnp.sum(lf[0:1] * lf[1:2], axis=-1, keepdims=True))
           - jnp.exp(jnp.sum(lf[2:3] * lf[3:4], axis=-1, keepdims=True)) + lam_init)
    out_gain = on_ref[...]
    blocks_per_seq = seq_len // qb

    def body(n, carry):
        r0 = pl.multiple_of(n * qb, qb)
        k0 = pl.multiple_of((n // blocks_per_seq) * seq_len, seq_len)
        for hd in range(C_HEADS):
            vs = slice(hd * C_V_DIM, (hd + 1) * C_V_DIM)
            maps = []
            for mp in range(2):
                slot = mp * C_HEADS + hd
                hs = slice(slot * HEAD_DIM, (slot + 1) * HEAD_DIM)
                q = q_ref[pl.ds(r0, qb), hs]
                scores, values = [], []
                if ctx:
                    scores.append(_dot_nt(q, kc_s[slot]))
                    values.append(vc_s[hd])
                _key_chunks(scores, values, q, k_ref, v_ref, k0, seq_len, hs, vs, chunk=seq_len)
                maps.append(_softmax_pv(scores, values, joint=True)[0])
            d = maps[0] - lam * maps[1]
            r = lax.rsqrt(jnp.mean(d * d, axis=-1, keepdims=True) + EPS)
            y = ((d * r) * out_gain) * (1.0 - lam_init)
            gate = g_ref[pl.ds(r0, qb), vs].astype(F32)
            o_ref[pl.ds(r0, qb), vs] = (y * gate).astype(BF16)
        return carry

    lax.fori_loop(0, n_blocks, body, 0, unroll=UNROLL_CTX)


def _diff(q, k, v, gate, c_lam, c_on, lam_init, seq_len, qb, rows_per_step, ctx=None, layer_idx=0):
    m_rows = q.shape[0]
    in_specs = [
        pl.BlockSpec((4, HEAD_DIM), lambda b: (0, 0)),
        pl.BlockSpec((1, C_V_DIM), lambda b: (0, 0)),
    ] + [pl.BlockSpec((rows_per_step, HALF_W), lambda b: (b, 0))] * 4
    args = [c_lam, c_on, q, k, v, gate]
    scratch = []
    if ctx is not None:
        past = ctx[0].shape[-2]
        in_specs += [
            pl.BlockSpec((None, None, 2 * C_HEADS, past, HEAD_DIM), lambda b: (b, layer_idx, 0, 0, 0)),
            pl.BlockSpec((None, None, C_HEADS, past, C_V_DIM), lambda b: (b, layer_idx, 0, 0, 0)),
        ]
        args += list(ctx)
        scratch = [pltpu.VMEM((2 * C_HEADS, past, HEAD_DIM), BF16),
                   pltpu.VMEM((C_HEADS, past, C_V_DIM), BF16)]
    return pl.pallas_call(
        functools.partial(_diff_kernel, seq_len=seq_len, qb=qb, n_blocks=rows_per_step // qb,
                          ctx=ctx is not None, lam_init=lam_init),
        grid=(m_rows // rows_per_step,),
        in_specs=in_specs,
        out_specs=pl.BlockSpec((rows_per_step, HALF_W), lambda b: (b, 0)),
        out_shape=jax.ShapeDtypeStruct((m_rows, HALF_W), BF16),
        scratch_shapes=scratch,
        compiler_params=_params(),
        name="diff_attn",
    )(*args)


def _outproj_kernel(*refs, bm, emit_h, conv_seq):
    it = iter(refs)
    y1_ref = next(it)
    if conv_seq is None:
        y2_ref = next(it)
    else:
        u_ref, b_ref, c_ref, g_ref = next(it), next(it), next(it), next(it)
        up_ref, cp_ref, un_ref, cn_ref = next(it), next(it), next(it), next(it)
        cw_ref = next(it)
    x_ref = next(it)
    w_refs = [next(it) for _ in range(D_MODEL // PROJ_CHUNK)]
    gt_ref = next(it)
    if emit_h:
        gn_ref, sh_ref, sc_ref = next(it), next(it), next(it)
    xo_ref = next(it)
    if emit_h:
        ho_ref = next(it)
    chunks = [slice(c, c + PROJ_CHUNK) for c in range(0, D_MODEL, PROJ_CHUNK)]
    w_top = [w.at[0:HALF_W, :] for w in w_refs]
    w_bot = [w.at[HALF_W:D_MODEL, :] for w in w_refs]
    if conv_seq is not None:
        assert bm % conv_seq == 0 or conv_seq % bm == 0
        row = lax.broadcasted_iota(jnp.int32, (bm, 1), 0)
        pos = (pl.program_id(0) * bm + row) % conv_seq
        first_row = jnp.logical_or(pos == 0, row == 0)
        last_row = jnp.logical_or(pos == conv_seq - 1, row == bm - 1)
        tile_pos = (pl.program_id(0) * bm) % conv_seq
        use_before = jnp.where(tile_pos == 0, 0.0, 1.0)
        use_after = jnp.where(tile_pos + bm == conv_seq, 0.0, 1.0) if conv_seq % bm == 0 else 0.0
        lanes = HALF_W // len(chunks)
        y2_parts = []
        for ci, cs in enumerate(chunks):
            ls = slice(ci * lanes, (ci + 1) * lanes)
            z = c_ref[:, ls].astype(F32) * u_ref[:, ls].astype(F32)
            z_before = (cp_ref[:, ls].astype(F32) * up_ref[:, ls].astype(F32))[HALO_ROWS - 1:HALO_ROWS] * use_before
            z_after = (cn_ref[:, ls].astype(F32) * un_ref[:, ls].astype(F32))[0:1] * use_after
            z_prev = jnp.where(first_row, z_before, pltpu.roll(z, 1, axis=0))
            z_next = jnp.where(last_row, z_after, pltpu.roll(z, bm - 1, axis=0))
            conv = z_prev * cw_ref[0:1, ls] + z * cw_ref[1:2, ls] + z_next * cw_ref[2:3, ls]
            y2_parts.append(((b_ref[:, ls].astype(F32) * conv) * g_ref[:, ls].astype(F32)).astype(BF16))
            xo_ref[:, cs] = x_ref[:, cs] + gt_ref[:, cs] * _dot(y1_ref[...], w_top[ci][...])
        y2 = jnp.concatenate(y2_parts, axis=1)
    ssq = jnp.zeros((bm, 1), F32)
    for ci, cs in enumerate(chunks):
        if conv_seq is None:
            acc = _dot(y1_ref[...], w_top[ci][...]) + _dot(y2_ref[...], w_bot[ci][...])
            xn = x_ref[:, cs] + gt_ref[:, cs] * acc
        else:
            xn = xo_ref[:, cs] + gt_ref[:, cs] * _dot(y2, w_bot[ci][...])
        xo_ref[:, cs] = xn
        if emit_h:
            ssq = ssq + jnp.sum(xn * xn, axis=-1, keepdims=True)
    if emit_h:
        r = lax.rsqrt(ssq * (1.0 / D_MODEL) + EPS)
        for cs in chunks:
            gain = gn_ref[:, cs] * (1.0 + sc_ref[:, cs])
            ho_ref[:, cs] = ((xo_ref[:, cs] * r) * gain + sh_ref[:, cs]).astype(BF16)


def _outproj(y1, y2, x, w_out, norm_g3, mod4, layer, row_fn, emit_h, conv=None):
    m_rows = x.shape[0]
    bm = PROJ_ROWS
    half_spec = pl.BlockSpec((bm, HALF_W), lambda m: (m, 0))
    in_specs = [half_spec]
    args = [y1]
    scratch = []
    if conv is None:
        in_specs.append(half_spec)
        args.append(y2)
    else:
        u, bgate, cgate, gate, conv_w, _ = conv
        halo_blocks = bm // HALO_ROWS
        last_halo = m_rows // HALO_ROWS - 1
        before = pl.BlockSpec((HALO_ROWS, HALF_W), lambda m: (jnp.maximum(m * halo_blocks - 1, 0), 0))
        after = pl.BlockSpec((HALO_ROWS, HALF_W), lambda m: (jnp.minimum((m + 1) * halo_blocks, last_halo), 0))
        in_specs += [half_spec] * 4 + [before, before, after, after,
                                       pl.BlockSpec((3, HALF_W), lambda m: (0, 0))]
        args += [u, bgate, cgate, gate, u, cgate, u, cgate, conv_w]
    w_specs, w_args = _weight_tile_specs(w_out)
    in_specs += [pl.BlockSpec((bm, D_MODEL), lambda m: (m, 0))] + w_specs + [_mod_spec(layer, 2, row_fn)]
    args += [x] + w_args + [mod4]
    out_specs = [pl.BlockSpec((bm, D_MODEL), lambda m: (m, 0))]
    out_shape = [jax.ShapeDtypeStruct((m_rows, D_MODEL), F32)]
    if emit_h:
        in_specs += [
            pl.BlockSpec((None, 1, D_MODEL), lambda m: (layer + 1, 0, 0)),
            _mod_spec(layer + 1, 0, row_fn),
            _mod_spec(layer + 1, 1, row_fn),
        ]
        args += [norm_g3, mod4, mod4]
        out_specs.append(pl.BlockSpec((bm, D_MODEL), lambda m: (m, 0)))
        out_shape.append(jax.ShapeDtypeStruct((m_rows, D_MODEL), BF16))
    outs = pl.pallas_call(
        functools.partial(_outproj_kernel, bm=bm, emit_h=emit_h,
                          conv_seq=None if conv is None else conv[5]),
        grid=(m_rows // bm,),
        in_specs=in_specs,
        out_specs=out_specs,
        out_shape=out_shape,
        scratch_shapes=scratch,
        compiler_params=_params(),
        name="outproj",
    )(*args)
    return (outs[0], outs[1]) if emit_h else (outs[0], None)


def _rope_tables(n_tokens):
    t = jnp.arange(n_tokens)
    row = t // GRID_W
    col = t % GRID_W
    n_freq = HEAD_DIM // 4
    inv = ROPE_THETA ** (-jnp.arange(n_freq, dtype=F32) / n_freq)
    ang_r = row[:, None] * inv
    ang_c = col[:, None] * inv
    cos = jnp.concatenate([jnp.cos(ang_r)] * 2 + [jnp.cos(ang_c)] * 2, axis=1)
    sin = jnp.concatenate([-jnp.sin(ang_r), jnp.sin(ang_r), -jnp.sin(ang_c), jnp.sin(ang_c)], axis=1)
    return cos, sin


def kernel(x_prompt, x_sample, cache_a_k, cache_a_v, cache_c_k, cache_c_v, cache_d_k, cache_d_v,
           c, c_ctx, norm_g, ada_w, ada_b, ev_w_in, ev_w_out, a_q_norm, a_k_norm, a_sink, b_conv,
           od_w_in, od_w_out, c_q_norm, c_k_norm, c_lambda, c_out_norm, d_q_norm, d_k_norm):
    n_prompt, seq_p, _ = x_prompt.shape
    n_sample, seq_s, _ = x_sample.shape
    past = cache_a_k.shape[-2]
    scale = HEAD_DIM ** -0.5 * LOG2E

    cond = jnp.concatenate(
        [c, c_ctx[None, :], jnp.zeros((COND_ROWS - n_sample - 1, D_MODEL), F32)], axis=0)
    mod4 = _modulation(cond, ada_w, ada_b).reshape(DEPTH, COND_ROWS, 1, 3 * D_MODEL)
    norm_g3 = norm_g.reshape(DEPTH, 1, D_MODEL)
    rope_tabs = _rope_tables(seq_s)

    prompt_row = lambda m: CTX_ROW
    sample_row = lambda m: (m * PROJ_ROWS) // seq_s

    xp = x_prompt.reshape(n_prompt * seq_p, D_MODEL)
    xs = x_sample.reshape(n_sample * seq_s, D_MODEL)
    hp = _prenorm(xp, norm_g3, mod4, 0, prompt_row)
    hs = _prenorm(xs, norm_g3, mod4, 0, sample_row)

    ev_in_tiles, ev_out_tiles = ev_w_in.astype(BF16), ev_w_out.astype(BF16)
    od_in_tiles, od_out_tiles = od_w_in.astype(BF16), od_w_out.astype(BF16)
    cache_c_k8 = cache_c_k.reshape(n_sample, DEPTH // 2, 2 * C_HEADS, past, HEAD_DIM)
    prompt_rows = PROMPT_SEQS_PER_STEP * seq_p

    new_a = None
    new_c = (None, None)
    for layer in range(DEPTH):
        i = layer // 2
        last = layer == DEPTH - 1
        if layer % 2 == 0:
            w_out = (ev_out_tiles, i, 0, D_MODEL // PROJ_CHUNK)
            gq = a_q_norm[i] * scale
            gk = a_k_norm[i]
            kv_cache = (N_KV, HEAD_DIM)

            def project(h, seq_len, tabs, cache, prev=None):
                segs1 = [_Seg("norm", HALF_W), _Seg("norm", N_KV * HEAD_DIM, cache),
                         _Seg("plain", N_KV * HEAD_DIM, cache), _Seg("silu", HALF_W)]
                segs_all = segs1 + [_Seg("plain", HALF_W)] * 3 + [_Seg("silu", HALF_W)]
                outs, caches = _inproj(h, (ev_in_tiles, i, 0, ALL_TILES), segs_all, [gq, gk] + [None] * 6,
                                       tabs, seq_len, cache_slot=i, cache_prev=prev, bm=PROJ_ROWS)
                return (*outs, caches)

            q, k, v, ga, u, bg, cg, gb, new_a = project(hp, seq_p, None, kv_cache, new_a)
            y1p = _gqa(q, k, v, ga, seq_p, seq_p, prompt_rows, sink=a_sink[i])
            y2p, conv_p = None, (u, bg, cg, gb, b_conv[i], seq_p)

            q, k, v, ga, u, bg, cg, gb, _ = project(hs, seq_s, rope_tabs, None)
            y1s = _banded(q, k, v, ga, a_sink[i], cache_a_k, cache_a_v, i, seq_s)
            y2s, conv_s = None, (u, bg, cg, gb, b_conv[i], seq_s)
        else:
            lam_init = 0.8 - 0.6 * math.exp(-0.3 * layer)
            w_out = (od_out_tiles, i, 0, D_MODEL // PROJ_CHUNK)
            gcq = c_q_norm[i] * scale
            gck = c_k_norm[i]
            gdq = d_q_norm[i] * scale
            gdk = d_k_norm[i]
            c_on = c_out_norm[i].reshape(1, C_V_DIM)

            def project(h, seq_len, tabs, with_cache, prev=(None, None)):
                ck_cache = (2 * C_HEADS, HEAD_DIM) if with_cache else None
                cv_cache = (C_HEADS, C_V_DIM) if with_cache else None
                kv_cache = (N_KV, HEAD_DIM) if with_cache else None
                segs_c = [_Seg("norm", HALF_W), _Seg("norm", HALF_W, ck_cache),
                          _Seg("plain", HALF_W, cv_cache), _Seg("silu", HALF_W)]
                segs_d = [_Seg("norm", HALF_W), _Seg("norm", N_KV * HEAD_DIM, kv_cache),
                          _Seg("plain", N_KV * HEAD_DIM, kv_cache), _Seg("silu", HALF_W)]
                gains_c, gains_d = [gcq, gck, None, None], [gdq, gdk, None, None]
                if not with_cache:
                    outs, _ = _inproj(h, (od_in_tiles, i, 0, ALL_TILES), segs_c + segs_d, gains_c + gains_d,
                                      tabs, seq_len, bm=PROJ_ROWS)
                    return (*outs, None)
                outs_c, cc = _inproj(h, (od_in_tiles, i, 0, C_TILES), segs_c, gains_c, tabs, seq_len,
                                     cache_slot=i, cache_prev=prev[0], bm=PROJ_ROWS)
                outs_d, cd = _inproj(h, (od_in_tiles, i, C_TILES, ALL_TILES - C_TILES), segs_d, gains_d,
                                     tabs, seq_len, cache_slot=i, cache_prev=prev[1])
                return (*outs_c, *outs_d, (cc, cd))

            cq, ck, cv, cgate, dq, dk, dv, dgate, new_c = project(hp, seq_p, None, True, new_c)
            y1p = _diff(cq, ck, cv, cgate, c_lambda[i], c_on, lam_init, seq_p, seq_p, prompt_rows)
            y2p = _gqa(dq, dk, dv, dgate, seq_p, seq_p, prompt_rows)

            cq, ck, cv, cgate, dq, dk, dv, dgate, _ = project(hs, seq_s, rope_tabs, False)
            y1s = _diff(cq, ck, cv, cgate, c_lambda[i], c_on, lam_init, seq_s, 2 * SAMPLE_QB, seq_s,
                        ctx=(cache_c_k8, cache_c_v), layer_idx=i)
            y2s = _gqa(dq, dk, dv, dgate, seq_s, 2 * SAMPLE_QB, seq_s,
                       ctx=(cache_d_k, cache_d_v), layer_idx=i)
            conv_p = conv_s = None

        xp, hp = _outproj(y1p, y2p, xp, w_out, norm_g3, mod4, layer, prompt_row, not last, conv_p)
        xs, hs = _outproj(y1s, y2s, xs, w_out, norm_g3, mod4, layer, sample_row, not last, conv_s)

    return (xp.reshape(n_prompt, seq_p, D_MODEL), xs.reshape(n_sample, seq_s, D_MODEL),
            new_a[0], new_a[1],
            new_c[0][0].reshape(n_prompt, CACHE_SLOTS, 2, C_HEADS, seq_p, HEAD_DIM), new_c[0][1],
            new_c[1][0], new_c[1][1])
```

```python
import functools
import math

import jax
import jax.numpy as jnp
from jax import lax
from jax.experimental import pallas as pl
from jax.experimental.pallas import tpu as pltpu

D_MODEL = 2048
DEPTH = 4
GRID_W = 64
HEAD_DIM = 128
WINDOW = 128
ROPE_THETA = 10000.0
EPS = 1e-6
NEG_INF = -1e30
HALF_W = D_MODEL // 2
N_KV = 2
GROUP = 4
C_HEADS = 4
C_V_DIM = 2 * HEAD_DIM
CACHE_SLOTS = DEPTH // 2
COND_ROWS = 16
CTX_ROW = 8

BF16 = jnp.bfloat16
F32 = jnp.float32

VMEM_LIMIT_BYTES = 56 * 1024 * 1024
PROJ_ROWS = 512
INPROJ_ROWS = 1024
ALL_TILES = 13
C_TILES = 8
PROJ_CHUNK = 512
SAMPLE_QB = 128
KEY_CHUNK = 512
UNROLL_SELF = 4
UNROLL_CTX = 2
HALO_ROWS = 16
LOG2E = math.log2(math.e)
PROMPT_SEQS_PER_STEP = 4


def _params(n_axes=1, flags=None):
    return pltpu.CompilerParams(
        dimension_semantics=("arbitrary",) * n_axes,
        vmem_limit_bytes=VMEM_LIMIT_BYTES,
        flags=flags,
    )


def _dot(a, b):
    return jnp.dot(a, b, preferred_element_type=F32)


def _dot_nt(a, b):
    return lax.dot_general(a, b, (((1,), (1,)), ((), ())), preferred_element_type=F32)


def _silu(x):
    return x / (1.0 + jnp.exp(-x))


def _mod_kernel(cond_ref, w_ref, b_ref, o_ref):
    s = _silu(cond_ref[...]).astype(BF16)
    o_ref[...] = _dot(s, w_ref[...].astype(BF16)) + b_ref[...]


def _modulation(cond, ada_w, ada_b):
    bn = 1024
    n_out = 3 * D_MODEL
    return pl.pallas_call(
        _mod_kernel,
        grid=(DEPTH, n_out // bn),
        in_specs=[
            pl.BlockSpec((COND_ROWS, D_MODEL), lambda l, n: (0, 0)),
            pl.BlockSpec((None, D_MODEL, bn), lambda l, n: (l, 0, n)),
            pl.BlockSpec((None, 1, bn), lambda l, n: (l, 0, n)),
        ],
        out_specs=pl.BlockSpec((None, COND_ROWS, bn), lambda l, n: (l, 0, n)),
        out_shape=jax.ShapeDtypeStruct((DEPTH, COND_ROWS, n_out), F32),
        compiler_params=_params(2),
        name="modulation",
    )(cond, ada_w, ada_b.reshape(DEPTH, 1, n_out))


def _mod_spec(layer, part, row_fn):
    return pl.BlockSpec((None, None, 1, D_MODEL), lambda m: (layer, row_fn(m), 0, part))


def _norm_mod(x, r, g, sc, sh):
    return ((x * r) * g) * (1.0 + sc) + sh


class _Seg:
    def __init__(self, kind, width, cache=None):
        self.kind = kind
        self.width = width
        self.cache = cache


def _swap_halves(a, first_half):
    return jnp.where(first_half, pltpu.roll(a, 96, axis=1), pltpu.roll(a, 32, axis=1))


def _inproj_kernel(*refs, segs, rope, seq_len, bm, n_prev, first_fill, n_wtiles, prenorm):
    it = iter(refs)
    h_ref = next(it)
    if prenorm:
        gn_ref, sh_ref, sc_ref = next(it), next(it), next(it)
    w_refs = [next(it) for _ in range(n_wtiles)]
    gain_refs = [tuple(next(it) for _ in range(2 if rope else 1)) if s.kind == "norm" else None
                 for s in segs]
    for _ in range(n_prev):
        next(it)
    out_refs = [next(it) for _ in segs]
    cache_refs = [next(it) if s.cache else None for s in segs]
    if prenorm:
        x_ref, h_ref = h_ref, next(it)
        x = x_ref[...]
        r = lax.rsqrt(jnp.mean(x * x, axis=-1, keepdims=True) + EPS)
        h_ref[...] = _norm_mod(x, r, gn_ref[...], sc_ref[...], sh_ref[...]).astype(BF16)

    if rope:
        lane = lax.broadcasted_iota(jnp.int32, (bm, HEAD_DIM), 1)
        first_half = (lane % 64) < 32

    chunks = []
    col = 0
    for si, seg in enumerate(segs):
        for c0 in range(0, seg.width, PROJ_CHUNK):
            chunks.append((si, c0, min(PROJ_CHUNK, seg.width - c0), col + c0))
        col += seg.width

    def matmul(chunk):
        _, _, cw, wc = chunk
        off = wc % PROJ_CHUNK
        return _dot(h_ref[...], w_refs[wc // PROJ_CHUNK][:, off:off + cw])

    def epilogue(chunk, acc):
        si, c0, cw, _ = chunk
        seg, gain_ref, out_ref, cache_ref = segs[si], gain_refs[si], out_refs[si], cache_refs[si]
        if seg.kind == "norm":
            heads = []
            for hh in range(cw // HEAD_DIM):
                a = acc[:, hh * HEAD_DIM:(hh + 1) * HEAD_DIM]
                r = lax.rsqrt(jnp.mean(a * a, axis=-1, keepdims=True) + EPS)
                if rope:
                    y = (a * gain_ref[0][...] + _swap_halves(a, first_half) * gain_ref[1][...]) * r
                else:
                    y = (a * r) * gain_ref[0][...]
                heads.append(y)
            val = jnp.concatenate(heads, axis=1) if len(heads) > 1 else heads[0]
        elif seg.kind == "silu":
            val = _silu(acc)
        else:
            val = acc
        out_ref[:, c0:c0 + cw] = val.astype(BF16)
        if seg.cache:
            _, hw = seg.cache
            for j in range(bm // seq_len):
                for hh in range(cw // hw):
                    head = (c0 // hw) + hh
                    data = val[j * seq_len:(j + 1) * seq_len, hh * hw:(hh + 1) * hw]
                    if first_fill is None:
                        cache_ref[j, head] = data
                    else:
                        for slot in range(CACHE_SLOTS):
                            cache_ref[j, slot, head] = data if slot == first_fill else jnp.zeros_like(data)

    acc = matmul(chunks[0])
    for idx, chunk in enumerate(chunks):
        nxt = matmul(chunks[idx + 1]) if idx + 1 < len(chunks) else None
        epilogue(chunk, acc)
        acc = nxt


def _weight_tile_specs(weights):
    w, layer, first, n = weights
    specs = [pl.BlockSpec((None, D_MODEL, PROJ_CHUNK), lambda m, t=first + t: (layer, 0, t),
                          pipeline_mode=pl.Buffered(1)) for t in range(n)]
    return specs, [w] * n


def _inproj(h, weights, segs, gains, rope_tabs, seq_len, cache_slot=0, cache_prev=None, bm=INPROJ_ROWS,
            prenorm=None):
    m_rows = h.shape[0]
    rope = rope_tabs is not None
    w_specs, w_args = _weight_tile_specs(weights)
    assert sum(s.width for s in segs) == len(w_args) * PROJ_CHUNK
    in_specs = [pl.BlockSpec((bm, D_MODEL), lambda m: (m, 0))]
    args = [h]
    scratch = []
    if prenorm is not None:
        norm_g3, mod4, layer, row_fn = prenorm
        in_specs += [pl.BlockSpec((None, 1, D_MODEL), lambda m: (layer, 0, 0)),
                     _mod_spec(layer, 0, row_fn), _mod_spec(layer, 1, row_fn)]
        args += [norm_g3, mod4, mod4]
        scratch = [pltpu.VMEM((bm, D_MODEL), BF16)]
    in_specs += w_specs
    args += w_args
    if rope:
        cos, sin_signed = rope_tabs
        blocks_per_seq = seq_len // bm
        tab_spec = pl.BlockSpec((bm, HEAD_DIM), lambda m: (m % blocks_per_seq, 0))
    for seg, g in zip(segs, gains):
        if seg.kind != "norm":
            continue
        if rope:
            g_swapped = g.reshape(2, 2, HEAD_DIM // 4)[:, ::-1].reshape(HEAD_DIM)
            in_specs += [tab_spec, tab_spec]
            args += [cos * g[None, :], sin_signed * g_swapped[None, :]]
        else:
            in_specs.append(pl.BlockSpec((1, HEAD_DIM), lambda m: (0, 0)))
            args.append(g.reshape(1, HEAD_DIM))
    n_prev = 0 if cache_prev is None else len(cache_prev)
    aliases = {}
    for k in range(n_prev):
        aliases[len(args)] = len(segs) + k
        in_specs.append(pl.BlockSpec(memory_space=pl.ANY))
        args.append(cache_prev[k])
    out_specs = [pl.BlockSpec((bm, s.width), lambda m: (m, 0)) for s in segs]
    out_shape = [jax.ShapeDtypeStruct((m_rows, s.width), BF16) for s in segs]
    for s in segs:
        if s.cache:
            nh, hw = s.cache
            if cache_prev is None:
                out_specs.append(pl.BlockSpec((bm // seq_len, CACHE_SLOTS, nh, seq_len, hw),
                                              lambda m: (m, 0, 0, 0, 0)))
            else:
                out_specs.append(pl.BlockSpec((bm // seq_len, None, nh, seq_len, hw),
                                              lambda m: (m, cache_slot, 0, 0, 0)))
            out_shape.append(jax.ShapeDtypeStruct((m_rows // seq_len, CACHE_SLOTS, nh, seq_len, hw), F32))
    outs = pl.pallas_call(
        functools.partial(_inproj_kernel, segs=segs, rope=rope, seq_len=seq_len, bm=bm, n_prev=n_prev,
                          first_fill=cache_slot if cache_prev is None else None, n_wtiles=len(w_args),
                          prenorm=prenorm is not None),
        grid=(m_rows // bm,),
        in_specs=in_specs,
        out_specs=out_specs,
        out_shape=out_shape,
        input_output_aliases=aliases,
        scratch_shapes=scratch,
        compiler_params=_params(),
        name="inproj",
    )(*args)
    n = len(segs)
    return list(outs[:n]), list(outs[n:])


def _softmax_pv(scores, values, sinks=None, joint=False, ones_cols=False):
    groups = 1 if sinks is None else len(sinks)
    rows = scores[0].shape[0] // groups
    dv = values[0].shape[1] - (HEAD_DIM if ones_cols else 0)

    def weighted(e, v):
        pv = _dot(e, v)
        return (pv[:, dv:], pv[:, :dv]) if ones_cols else (None, pv)

    if joint or len(scores) == 1:
        probs = [[] for _ in scores]
        sink_terms, sums = [], []
        for g in range(groups):
            rs = slice(g * rows, (g + 1) * rows)
            m = jnp.max(scores[0][rs], axis=-1, keepdims=True)
            for s in scores[1:]:
                m = jnp.maximum(m, jnp.max(s[rs], axis=-1, keepdims=True))
            if sinks is not None:
                m = jnp.maximum(m, sinks[g])
            sink_terms.append(None if sinks is None else jnp.exp2(sinks[g] - m))
            l = None
            for i, s in enumerate(scores):
                e = jnp.exp2(s[rs] - m)
                if not ones_cols:
                    part = jnp.sum(e, axis=-1, keepdims=True)
                    l = part if l is None else l + part
                probs[i].append(e.astype(BF16))
            sums.append(l)
        o, l_all = None, None
        for p, v in zip(probs, values):
            l_part, pv = weighted(p[0] if groups == 1 else jnp.concatenate(p, axis=0), v)
            o = pv if o is None else o + pv
            if ones_cols:
                l_all = l_part if l_all is None else l_all + l_part
        outs = []
        for g in range(groups):
            rs = slice(g * rows, (g + 1) * rows)
            l = l_all[rs] if ones_cols else sums[g]
            if sink_terms[g] is not None:
                l = l + sink_terms[g]
            outs.append(o[rs] * (1.0 / l))
        return outs
    parts = []
    for s, v in zip(scores, values):
        m = jnp.max(s, axis=-1, keepdims=True)
        e = jnp.exp2(s - m)
        l, pv = weighted(e.astype(BF16), v)
        parts.append((m, jnp.sum(e, axis=-1, keepdims=True) if l is None else l, pv))
    outs = []
    for g in range(groups):
        rs = slice(g * rows, (g + 1) * rows)
        m_all = parts[0][0][rs]
        for m, _, _ in parts[1:]:
            m_all = jnp.maximum(m_all, m[rs])
        den = None
        if sinks is not None:
            m_all = jnp.maximum(m_all, sinks[g])
            den = jnp.exp2(sinks[g] - m_all)
        num = None
        for m, l, o in parts:
            w = jnp.exp2(m[rs] - m_all)
            den = l[rs] * w if den is None else den + l[rs] * w
            num = o[rs] * w if num is None else num + o[rs] * w
        outs.append(num * (1.0 / den))
    return outs


def _stack_heads(ref, r0, rows, first_head):
    return jnp.concatenate(
        [ref[pl.ds(r0, rows), (first_head + g) * HEAD_DIM:(first_head + g + 1) * HEAD_DIM]
         for g in range(GROUP)], axis=0)


def _head_sinks(sink_ref, first_head):
    return [sink_ref[first_head + g] * LOG2E for g in range(GROUP)]


def _key_chunks(scores, values, q, k_ref, v_ref, k0, seq_len, cols_k, cols_v, chunk=KEY_CHUNK,
                ones_cols=False):
    step = min(seq_len, chunk)
    for c in range(0, seq_len, step):
        scores.append(_dot_nt(q, k_ref[pl.ds(k0 + c, step), cols_k]))
        v = v_ref[pl.ds(k0 + c, step), cols_v]
        if ones_cols:
            v = jnp.concatenate([v, jnp.ones((step, HEAD_DIM), BF16)], axis=1)
        values.append(v)


def _store_gated(o_ref, g_ref, o, r0, rows, first_head):
    if len(o) == 1:
        o = [o[0][g * rows:(g + 1) * rows] for g in range(GROUP)]
    for g in range(GROUP):
        cs = slice((first_head + g) * HEAD_DIM, (first_head + g + 1) * HEAD_DIM)
        gate = g_ref[pl.ds(r0, rows), cs].astype(F32)
        o_ref[pl.ds(r0, rows), cs] = (o[g] * gate).astype(BF16)


def _gqa_kernel(*refs, seq_len, qb, n_blocks, ctx, sink):
    it = iter(refs)
    sink_ref = next(it) if sink else None
    q_ref, k_ref, v_ref, g_ref = next(it), next(it), next(it), next(it)
    if ctx:
        ck_ref, cv_ref = next(it), next(it)
    o_ref = next(it)
    if ctx:
        kc_s, vc_s = next(it), next(it)
        for kv in range(N_KV):
            kc_s[kv] = ck_ref[kv].astype(BF16)
            vc_s[kv, :, 0:HEAD_DIM] = cv_ref[kv].astype(BF16)
            vc_s[kv, :, HEAD_DIM:2 * HEAD_DIM] = jnp.ones(cv_ref.shape[1:], BF16)
    blocks_per_seq = seq_len // qb

    def body(n, carry):
        r0 = pl.multiple_of(n * qb, qb)
        k0 = pl.multiple_of((n // blocks_per_seq) * seq_len, seq_len)
        for kv in range(N_KV):
            hs = slice(kv * HEAD_DIM, (kv + 1) * HEAD_DIM)
            q = _stack_heads(q_ref, r0, qb, kv * GROUP)
            scores, values = [], []
            if ctx:
                scores.append(_dot_nt(q, kc_s[kv]))
                values.append(vc_s[kv])
            _key_chunks(scores, values, q, k_ref, v_ref, k0, seq_len, hs, hs, ones_cols=True)
            sinks = _head_sinks(sink_ref, kv * GROUP) if sink else None
            o = _softmax_pv(scores, values, sinks, ones_cols=True)
            _store_gated(o_ref, g_ref, o, r0, qb, kv * GROUP)
        return carry

    lax.fori_loop(0, n_blocks, body, 0, unroll=UNROLL_CTX if ctx else UNROLL_SELF)


def _gqa(q, k, v, gate, seq_len, qb, rows_per_step, sink=None, ctx=None, layer_idx=0):
    m_rows = q.shape[0]
    kvw = N_KV * HEAD_DIM
    in_specs, args = [], []
    if sink is not None:
        in_specs.append(pl.BlockSpec(memory_space=pltpu.SMEM))
        args.append(sink)
    in_specs += [
        pl.BlockSpec((rows_per_step, HALF_W), lambda b: (b, 0)),
        pl.BlockSpec((rows_per_step, kvw), lambda b: (b, 0)),
        pl.BlockSpec((rows_per_step, kvw), lambda b: (b, 0)),
        pl.BlockSpec((rows_per_step, HALF_W), lambda b: (b, 0)),
    ]
    args += [q, k, v, gate]
    scratch = []
    if ctx is not None:
        past = ctx[0].shape[-2]
        cspec = pl.BlockSpec((None, None, N_KV, past, HEAD_DIM), lambda b: (b, layer_idx, 0, 0, 0))
        in_specs += [cspec, cspec]
        args += list(ctx)
        scratch = [pltpu.VMEM((N_KV, past, HEAD_DIM), BF16), pltpu.VMEM((N_KV, past, 2 * HEAD_DIM), BF16)]
    return pl.pallas_call(
        functools.partial(_gqa_kernel, seq_len=seq_len, qb=qb, n_blocks=rows_per_step // qb,
                          ctx=ctx is not None, sink=sink is not None),
        grid=(m_rows // rows_per_step,),
        in_specs=in_specs,
        out_specs=pl.BlockSpec((rows_per_step, HALF_W), lambda b: (b, 0)),
        out_shape=jax.ShapeDtypeStruct((m_rows, HALF_W), BF16),
        scratch_shapes=scratch,
        compiler_params=_params(),
        name="gqa",
    )(*args)


def _banded_kernel(sink_ref, q_ref, k_ref, v_ref, g_ref, ck_ref, cv_ref, o_ref,
                   kpad, vpad, kc_s, vc_s, *, seq_len):
    n_blocks = seq_len // WINDOW
    zeros = jnp.zeros((WINDOW, HEAD_DIM), BF16)
    for kv in range(N_KV):
        hs = slice(kv * HEAD_DIM, (kv + 1) * HEAD_DIM)
        kpad[kv, 0:WINDOW] = zeros
        vpad[kv, 0:WINDOW, 0:HEAD_DIM] = zeros
        kpad[kv, WINDOW + seq_len:2 * WINDOW + seq_len] = zeros
        vpad[kv, WINDOW + seq_len:2 * WINDOW + seq_len, 0:HEAD_DIM] = zeros
        kpad[kv, WINDOW:WINDOW + seq_len] = k_ref[:, hs]
        vpad[kv, WINDOW:WINDOW + seq_len, 0:HEAD_DIM] = v_ref[:, hs]
        kc_s[kv] = ck_ref[kv].astype(BF16)
        vc_s[kv, :, 0:HEAD_DIM] = cv_ref[kv].astype(BF16)
        vpad[kv, :, HEAD_DIM:2 * HEAD_DIM] = jnp.ones((seq_len + 2 * WINDOW, HEAD_DIM), BF16)
        vc_s[kv, :, HEAD_DIM:2 * HEAD_DIM] = jnp.ones(cv_ref.shape[1:], BF16)

    rows = GROUP * WINDOW
    qi = lax.broadcasted_iota(jnp.int32, (rows, 3 * WINDOW), 0) % WINDOW
    col = lax.broadcasted_iota(jnp.int32, (rows, 3 * WINDOW), 1)
    kj = col % WINDOW
    blk = col // WINDOW
    never = 4 * WINDOW

    def body(n, carry):
        r0 = pl.multiple_of(n * WINDOW, WINDOW)
        lo = jnp.where(n > 0, 0, never)
        hi = jnp.where(n < n_blocks - 1, 0, never)
        slack = jnp.where(blk == 0, kj - qi - lo, jnp.where(blk == 2, qi - kj - hi, 0))
        valid = slack >= 0
        for kv in range(N_KV):
            q = _stack_heads(q_ref, r0, WINDOW, kv * GROUP)
            kb = kpad[kv, pl.ds(r0, 3 * WINDOW), :]
            vb = vpad[kv, pl.ds(r0, 3 * WINDOW), :]
            s_ctx = _dot_nt(q, kc_s[kv])
            s_band = jnp.where(valid, _dot_nt(q, kb), NEG_INF)
            o = _softmax_pv([s_ctx, s_band], [vc_s[kv], vb], _head_sinks(sink_ref, kv * GROUP),
                            ones_cols=True)
            _store_gated(o_ref, g_ref, o, r0, WINDOW, kv * GROUP)
        return carry

    lax.fori_loop(0, n_blocks, body, 0, unroll=UNROLL_CTX)


def _banded(q, k, v, gate, sink, cache_k, cache_v, layer_idx, seq_len):
    m_rows = q.shape[0]
    kvw = N_KV * HEAD_DIM
    past = cache_k.shape[-2]
    cspec = pl.BlockSpec((None, None, N_KV, past, HEAD_DIM), lambda b: (b, layer_idx, 0, 0, 0))
    pad_rows = seq_len + 2 * WINDOW
    return pl.pallas_call(
        functools.partial(_banded_kernel, seq_len=seq_len),
        grid=(m_rows // seq_len,),
        in_specs=[
            pl.BlockSpec(memory_space=pltpu.SMEM),
            pl.BlockSpec((seq_len, HALF_W), lambda b: (b, 0)),
            pl.BlockSpec((seq_len, kvw), lambda b: (b, 0)),
            pl.BlockSpec((seq_len, kvw), lambda b: (b, 0)),
            pl.BlockSpec((seq_len, HALF_W), lambda b: (b, 0)),
            cspec, cspec,
        ],
        out_specs=pl.BlockSpec((seq_len, HALF_W), lambda b: (b, 0)),
        out_shape=jax.ShapeDtypeStruct((m_rows, HALF_W), BF16),
        scratch_shapes=[
            pltpu.VMEM((N_KV, pad_rows, HEAD_DIM), BF16),
            pltpu.VMEM((N_KV, pad_rows, 2 * HEAD_DIM), BF16),
            pltpu.VMEM((N_KV, past, HEAD_DIM), BF16),
            pltpu.VMEM((N_KV, past, 2 * HEAD_DIM), BF16),
        ],
        compiler_params=_params(),
        name="banded",
    )(sink, q, k, v, gate, cache_k, cache_v)


def _diff_kernel(*refs, seq_len, qb, n_blocks, ctx, lam_init):
    it = iter(refs)
    lam_ref, on_ref = next(it), next(it)
    q_ref, k_ref, v_ref, g_ref = next(it), next(it), next(it), next(it)
    if ctx:
        ck_ref, cv_ref = next(it), next(it)
    o_ref = next(it)
    if ctx:
        kc_s, vc_s = next(it), next(it)
        for slot in range(2 * C_HEADS):
            kc_s[slot] = ck_ref[slot].astype(BF16)
        for hd in range(C_HEADS):
            vc_s[hd] = cv_ref[hd].astype(BF16)
    lf = lam_ref[...]
    lam = (jnp.exp(jnp.sum(lf[0:1] * lf[1:2], axis=-1, keepdims=True))
           - jnp.exp(jnp.sum(lf[2:3] * lf[3:4], axis=-1, keepdims=True)) + lam_init)
    out_gain = on_ref[...]
    blocks_per_seq = seq_len // qb

    def body(n, carry):
        r0 = pl.multiple_of(n * qb, qb)
        k0 = pl.multiple_of((n // blocks_per_seq) * seq_len, seq_len)
        for hd in range(C_HEADS):
            vs = slice(hd * C_V_DIM, (hd + 1) * C_V_DIM)
            maps = []
            for mp in range(2):
                slot = mp * C_HEADS + hd
                hs = slice(slot * HEAD_DIM, (slot + 1) * HEAD_DIM)
                q = q_ref[pl.ds(r0, qb), hs]
                scores, values = [], []
                if ctx:
                    scores.append(_dot_nt(q, kc_s[slot]))
                    values.append(vc_s[hd])
                _key_chunks(scores, values, q, k_ref, v_ref, k0, seq_len, hs, vs, chunk=seq_len)
                maps.append(_softmax_pv(scores, values, joint=True)[0])
            d = maps[0] - lam * maps[1]
            r = lax.rsqrt(jnp.mean(d * d, axis=-1, keepdims=True) + EPS)
            y = ((d * r) * out_gain) * (1.0 - lam_init)
            gate = g_ref[pl.ds(r0, qb), vs].astype(F32)
            o_ref[pl.ds(r0, qb), vs] = (y * gate).astype(BF16)
        return carry

    lax.fori_loop(0, n_blocks, body, 0, unroll=UNROLL_CTX)


def _diff(q, k, v, gate, c_lam, c_on, lam_init, seq_len, qb, rows_per_step, ctx=None, layer_idx=0):
    m_rows = q.shape[0]
    in_specs = [
        pl.BlockSpec((4, HEAD_DIM), lambda b: (0, 0)),
        pl.BlockSpec((1, C_V_DIM), lambda b: (0, 0)),
    ] + [pl.BlockSpec((rows_per_step, HALF_W), lambda b: (b, 0))] * 4
    args = [c_lam, c_on, q, k, v, gate]
    scratch = []
    if ctx is not None:
        past = ctx[0].shape[-2]
        in_specs += [
            pl.BlockSpec((None, None, 2 * C_HEADS, past, HEAD_DIM), lambda b: (b, layer_idx, 0, 0, 0)),
            pl.BlockSpec((None, None, C_HEADS, past, C_V_DIM), lambda b: (b, layer_idx, 0, 0, 0)),
        ]
        args += list(ctx)
        scratch = [pltpu.VMEM((2 * C_HEADS, past, HEAD_DIM), BF16),
                   pltpu.VMEM((C_HEADS, past, C_V_DIM), BF16)]
    return pl.pallas_call(
        functools.partial(_diff_kernel, seq_len=seq_len, qb=qb, n_blocks=rows_per_step // qb,
                          ctx=ctx is not None, lam_init=lam_init),
        grid=(m_rows // rows_per_step,),
        in_specs=in_specs,
        out_specs=pl.BlockSpec((rows_per_step, HALF_W), lambda b: (b, 0)),
        out_shape=jax.ShapeDtypeStruct((m_rows, HALF_W), BF16),
        scratch_shapes=scratch,
        compiler_params=_params(),
        name="diff_attn",
    )(*args)


def _outproj_kernel(*refs, bm, emit_h, conv_seq):
    it = iter(refs)
    y1_ref = next(it)
    if conv_seq is None:
        y2_ref = next(it)
    else:
        u_ref, b_ref, c_ref, g_ref = next(it), next(it), next(it), next(it)
        up_ref, cp_ref, un_ref, cn_ref = next(it), next(it), next(it), next(it)
        cw_ref = next(it)
    x_ref = next(it)
    w_refs = [next(it) for _ in range(D_MODEL // PROJ_CHUNK)]
    gt_ref = next(it)
    if emit_h:
        gn_ref, sh_ref, sc_ref = next(it), next(it), next(it)
    xo_ref = next(it)
    if emit_h:
        ho_ref = next(it)
    chunks = [slice(c, c + PROJ_CHUNK) for c in range(0, D_MODEL, PROJ_CHUNK)]
    w_top = [w.at[0:HALF_W, :] for w in w_refs]
    w_bot = [w.at[HALF_W:D_MODEL, :] for w in w_refs]
    if conv_seq is not None:
        assert bm % conv_seq == 0 or conv_seq % bm == 0
        row = lax.broadcasted_iota(jnp.int32, (bm, 1), 0)
        pos = (pl.program_id(0) * bm + row) % conv_seq
        first_row = jnp.logical_or(pos == 0, row == 0)
        last_row = jnp.logical_or(pos == conv_seq - 1, row == bm - 1)
        tile_pos = (pl.program_id(0) * bm) % conv_seq
        use_before = jnp.where(tile_pos == 0, 0.0, 1.0)
        use_after = jnp.where(tile_pos + bm == conv_seq, 0.0, 1.0) if conv_seq % bm == 0 else 0.0
        lanes = HALF_W // len(chunks)
        y2_parts = []
        for ci, cs in enumerate(chunks):
            ls = slice(ci * lanes, (ci + 1) * lanes)
            z = c_ref[:, ls].astype(F32) * u_ref[:, ls].astype(F32)
            z_before = (cp_ref[:, ls].astype(F32) * up_ref[:, ls].astype(F32))[HALO_ROWS - 1:HALO_ROWS] * use_before
            z_after = (cn_ref[:, ls].astype(F32) * un_ref[:, ls].astype(F32))[0:1] * use_after
            z_prev = jnp.where(first_row, z_before, pltpu.roll(z, 1, axis=0))
            z_next = jnp.where(last_row, z_after, pltpu.roll(z, bm - 1, axis=0))
            conv = z_prev * cw_ref[0:1, ls] + z * cw_ref[1:2, ls] + z_next * cw_ref[2:3, ls]
            y2_parts.append(((b_ref[:, ls].astype(F32) * conv) * g_ref[:, ls].astype(F32)).astype(BF16))
            xo_ref[:, cs] = x_ref[:, cs] + gt_ref[:, cs] * _dot(y1_ref[...], w_top[ci][...])
        y2 = jnp.concatenate(y2_parts, axis=1)
    ssq = jnp.zeros((bm, 1), F32)
    for ci, cs in enumerate(chunks):
        if conv_seq is None:
            acc = _dot(y1_ref[...], w_top[ci][...]) + _dot(y2_ref[...], w_bot[ci][...])
            xn = x_ref[:, cs] + gt_ref[:, cs] * acc
        else:
            xn = xo_ref[:, cs] + gt_ref[:, cs] * _dot(y2, w_bot[ci][...])
        xo_ref[:, cs] = xn
        if emit_h:
            ssq = ssq + jnp.sum(xn * xn, axis=-1, keepdims=True)
    if emit_h:
        r = lax.rsqrt(ssq * (1.0 / D_MODEL) + EPS)
        for cs in chunks:
            gain = gn_ref[:, cs] * (1.0 + sc_ref[:, cs])
            ho_ref[:, cs] = ((xo_ref[:, cs] * r) * gain + sh_ref[:, cs]).astype(BF16)


def _outproj(y1, y2, x, w_out, norm_g3, mod4, layer, row_fn, emit_h, conv=None):
    m_rows = x.shape[0]
    bm = PROJ_ROWS
    half_spec = pl.BlockSpec((bm, HALF_W), lambda m: (m, 0))
    in_specs = [half_spec]
    args = [y1]
    scratch = []
    if conv is None:
        in_specs.append(half_spec)
        args.append(y2)
    else:
        u, bgate, cgate, gate, conv_w, _ = conv
        halo_blocks = bm // HALO_ROWS
        last_halo = m_rows // HALO_ROWS - 1
        before = pl.BlockSpec((HALO_ROWS, HALF_W), lambda m: (jnp.maximum(m * halo_blocks - 1, 0), 0))
        after = pl.BlockSpec((HALO_ROWS, HALF_W), lambda m: (jnp.minimum((m + 1) * halo_blocks, last_halo), 0))
        in_specs += [half_spec] * 4 + [before, before, after, after,
                                       pl.BlockSpec((3, HALF_W), lambda m: (0, 0))]
        args += [u, bgate, cgate, gate, u, cgate, u, cgate, conv_w]
    w_specs, w_args = _weight_tile_specs(w_out)
    in_specs += [pl.BlockSpec((bm, D_MODEL), lambda m: (m, 0))] + w_specs + [_mod_spec(layer, 2, row_fn)]
    args += [x] + w_args + [mod4]
    out_specs = [pl.BlockSpec((bm, D_MODEL), lambda m: (m, 0))]
    out_shape = [jax.ShapeDtypeStruct((m_rows, D_MODEL), F32)]
    if emit_h:
        in_specs += [
            pl.BlockSpec((None, 1, D_MODEL), lambda m: (layer + 1, 0, 0)),
            _mod_spec(layer + 1, 0, row_fn),
            _mod_spec(layer + 1, 1, row_fn),
        ]
        args += [norm_g3, mod4, mod4]
        out_specs.append(pl.BlockSpec((bm, D_MODEL), lambda m: (m, 0)))
        out_shape.append(jax.ShapeDtypeStruct((m_rows, D_MODEL), BF16))
    outs = pl.pallas_call(
        functools.partial(_outproj_kernel, bm=bm, emit_h=emit_h,
                          conv_seq=None if conv is None else conv[5]),
        grid=(m_rows // bm,),
        in_specs=in_specs,
        out_specs=out_specs,
        out_shape=out_shape,
        scratch_shapes=scratch,
        compiler_params=_params(),
        name="outproj",
    )(*args)
    return (outs[0], outs[1]) if emit_h else (outs[0], None)


def _rope_tables(n_tokens):
    t = jnp.arange(n_tokens)
    row = t // GRID_W
    col = t % GRID_W
    n_freq = HEAD_DIM // 4
    inv = ROPE_THETA ** (-jnp.arange(n_freq, dtype=F32) / n_freq)
    ang_r = row[:, None] * inv
    ang_c = col[:, None] * inv
    cos = jnp.concatenate([jnp.cos(ang_r)] * 2 + [jnp.cos(ang_c)] * 2, axis=1)
    sin = jnp.concatenate([-jnp.sin(ang_r), jnp.sin(ang_r), -jnp.sin(ang_c), jnp.sin(ang_c)], axis=1)
    return cos, sin


def kernel(x_prompt, x_sample, cache_a_k, cache_a_v, cache_c_k, cache_c_v, cache_d_k, cache_d_v,
           c, c_ctx, norm_g, ada_w, ada_b, ev_w_in, ev_w_out, a_q_norm, a_k_norm, a_sink, b_conv,
           od_w_in, od_w_out, c_q_norm, c_k_norm, c_lambda, c_out_norm, d_q_norm, d_k_norm):
    n_prompt, seq_p, _ = x_prompt.shape
    n_sample, seq_s, _ = x_sample.shape
    past = cache_a_k.shape[-2]
    scale = HEAD_DIM ** -0.5 * LOG2E

    cond = jnp.concatenate(
        [c, c_ctx[None, :], jnp.zeros((COND_ROWS - n_sample - 1, D_MODEL), F32)], axis=0)
    mod4 = _modulation(cond, ada_w, ada_b).reshape(DEPTH, COND_ROWS, 1, 3 * D_MODEL)
    norm_g3 = norm_g.reshape(DEPTH, 1, D_MODEL)
    rope_tabs = _rope_tables(seq_s)

    prompt_row = lambda m: CTX_ROW
    sample_row = lambda m: (m * PROJ_ROWS) // seq_s

    xp = x_prompt.reshape(n_prompt * seq_p, D_MODEL)
    xs = x_sample.reshape(n_sample * seq_s, D_MODEL)
    hp, hs = xp, xs

    ev_in_tiles, ev_out_tiles = ev_w_in.astype(BF16), ev_w_out.astype(BF16)
    od_in_tiles, od_out_tiles = od_w_in.astype(BF16), od_w_out.astype(BF16)
    cache_c_k8 = cache_c_k.reshape(n_sample, DEPTH // 2, 2 * C_HEADS, past, HEAD_DIM)
    prompt_rows = PROMPT_SEQS_PER_STEP * seq_p

    new_a = None
    new_c = (None, None)
    for layer in range(DEPTH):
        i = layer // 2
        last = layer == DEPTH - 1
        if layer % 2 == 0:
            w_out = (ev_out_tiles, i, 0, D_MODEL // PROJ_CHUNK)
            gq = a_q_norm[i] * scale
            gk = a_k_norm[i]
            kv_cache = (N_KV, HEAD_DIM)

            def project(h, seq_len, tabs, cache, prev, row_fn):
                segs1 = [_Seg("norm", HALF_W), _Seg("norm", N_KV * HEAD_DIM, cache),
                         _Seg("plain", N_KV * HEAD_DIM, cache), _Seg("silu", HALF_W)]
                segs_all = segs1 + [_Seg("plain", HALF_W)] * 3 + [_Seg("silu", HALF_W)]
                outs, caches = _inproj(h, (ev_in_tiles, i, 0, ALL_TILES), segs_all, [gq, gk] + [None] * 6,
                                       tabs, seq_len, cache_slot=i, cache_prev=prev, bm=PROJ_ROWS,
                                       prenorm=(norm_g3, mod4, layer, row_fn) if layer == 0 else None)
                return (*outs, caches)

            q, k, v, ga, u, bg, cg, gb, new_a = project(hp, seq_p, None, kv_cache, new_a, prompt_row)
            y1p = _gqa(q, k, v, ga, seq_p, seq_p, prompt_rows, sink=a_sink[i])
            y2p, conv_p = None, (u, bg, cg, gb, b_conv[i], seq_p)

            q, k, v, ga, u, bg, cg, gb, _ = project(hs, seq_s, rope_tabs, None, None, sample_row)
            y1s = _banded(q, k, v, ga, a_sink[i], cache_a_k, cache_a_v, i, seq_s)
            y2s, conv_s = None, (u, bg, cg, gb, b_conv[i], seq_s)
        else:
            lam_init = 0.8 - 0.6 * math.exp(-0.3 * layer)
            w_out = (od_out_tiles, i, 0, D_MODEL // PROJ_CHUNK)
            gcq = c_q_norm[i] * scale
            gck = c_k_norm[i]
            gdq = d_q_norm[i] * scale
            gdk = d_k_norm[i]
            c_on = c_out_norm[i].reshape(1, C_V_DIM)

            def project(h, seq_len, tabs, with_cache, prev=(None, None)):
                ck_cache = (2 * C_HEADS, HEAD_DIM) if with_cache else None
                cv_cache = (C_HEADS, C_V_DIM) if with_cache else None
                kv_cache = (N_KV, HEAD_DIM) if with_cache else None
                segs_c = [_Seg("norm", HALF_W), _Seg("norm", HALF_W, ck_cache),
                          _Seg("plain", HALF_W, cv_cache), _Seg("silu", HALF_W)]
                segs_d = [_Seg("norm", HALF_W), _Seg("norm", N_KV * HEAD_DIM, kv_cache),
                          _Seg("plain", N_KV * HEAD_DIM, kv_cache), _Seg("silu", HALF_W)]
                gains_c, gains_d = [gcq, gck, None, None], [gdq, gdk, None, None]
                if not with_cache:
                    outs, _ = _inproj(h, (od_in_tiles, i, 0, ALL_TILES), segs_c + segs_d, gains_c + gains_d,
                                      tabs, seq_len, bm=PROJ_ROWS)
                    return (*outs, None)
                outs_c, cc = _inproj(h, (od_in_tiles, i, 0, C_TILES), segs_c, gains_c, tabs, seq_len,
                                     cache_slot=i, cache_prev=prev[0], bm=PROJ_ROWS)
                outs_d, cd = _inproj(h, (od_in_tiles, i, C_TILES, ALL_TILES - C_TILES), segs_d, gains_d,
                                     tabs, seq_len, cache_slot=i, cache_prev=prev[1])
                return (*outs_c, *outs_d, (cc, cd))

            cq, ck, cv, cgate, dq, dk, dv, dgate, new_c = project(hp, seq_p, None, True, new_c)
            y1p = _diff(cq, ck, cv, cgate, c_lambda[i], c_on, lam_init, seq_p, seq_p, prompt_rows)
            y2p = _gqa(dq, dk, dv, dgate, seq_p, seq_p, prompt_rows)

            cq, ck, cv, cgate, dq, dk, dv, dgate, _ = project(hs, seq_s, rope_tabs, False)
            y1s = _diff(cq, ck, cv, cgate, c_lambda[i], c_on, lam_init, seq_s, 2 * SAMPLE_QB, seq_s,
                        ctx=(cache_c_k8, cache_c_v), layer_idx=i)
            y2s = _gqa(dq, dk, dv, dgate, seq_s, 2 * SAMPLE_QB, seq_s,
                       ctx=(cache_d_k, cache_d_v), layer_idx=i)
            conv_p = conv_s = None

        xp, hp = _outproj(y1p, y2p, xp, w_out, norm_g3, mod4, layer, prompt_row, not last, conv_p)
        xs, hs = _outproj(y1s, y2s, xs, w_out, norm_g3, mod4, layer, sample_row, not last, conv_s)

    return (xp.reshape(n_prompt, seq_p, D_MODEL), xs.reshape(n_sample, seq_s, D_MODEL),
            new_a[0], new_a[1],
            new_c[0][0].reshape(n_prompt, CACHE_SLOTS, 2, C_HEADS, seq_p, HEAD_DIM), new_c[0][1],
            new_c[1][0], new_c[1][1])
```

```python
import functools
import math

import jax
import jax.numpy as jnp
from jax import lax
from jax.experimental import pallas as pl
from jax.experimental.pallas import tpu as pltpu

D_MODEL = 2048
DEPTH = 4
GRID_W = 64
HEAD_DIM = 128
WINDOW = 128
ROPE_THETA = 10000.0
EPS = 1e-6
NEG_INF = -1e30
HALF_W = D_MODEL // 2
N_KV = 2
GROUP = 4
C_HEADS = 4
C_V_DIM = 2 * HEAD_DIM
CACHE_SLOTS = DEPTH // 2
COND_ROWS = 16
CTX_ROW = 8

BF16 = jnp.bfloat16
F32 = jnp.float32

VMEM_LIMIT_BYTES = 56 * 1024 * 1024
PROJ_ROWS = 512
INPROJ_ROWS = 1024
ALL_TILES = 13
C_TILES = 8
PROJ_CHUNK = 512
SAMPLE_QB = 128
KEY_CHUNK = 512
UNROLL_SELF = 4
UNROLL_CTX = 2
HALO_ROWS = 16
LOG2E = math.log2(math.e)
PROMPT_SEQS_PER_STEP = 4


def _params(n_axes=1):
    return pltpu.CompilerParams(
        dimension_semantics=("arbitrary",) * n_axes,
        vmem_limit_bytes=VMEM_LIMIT_BYTES,
    )


def _dot(a, b):
    return jnp.dot(a, b, preferred_element_type=F32)


def _dot_nt(a, b):
    return lax.dot_general(a, b, (((1,), (1,)), ((), ())), preferred_element_type=F32)


def _silu(x):
    return x / (1.0 + jnp.exp(-x))


def _mod_kernel(cond_ref, w_ref, b_ref, o_ref):
    s = _silu(cond_ref[...]).astype(BF16)
    o_ref[...] = _dot(s, w_ref[...].astype(BF16)) + b_ref[...]


def _modulation(cond, ada_w, ada_b):
    bn = 1024
    n_out = 3 * D_MODEL
    return pl.pallas_call(
        _mod_kernel,
        grid=(DEPTH, n_out // bn),
        in_specs=[
            pl.BlockSpec((COND_ROWS, D_MODEL), lambda l, n: (0, 0)),
            pl.BlockSpec((None, D_MODEL, bn), lambda l, n: (l, 0, n)),
            pl.BlockSpec((None, 1, bn), lambda l, n: (l, 0, n)),
        ],
        out_specs=pl.BlockSpec((None, COND_ROWS, bn), lambda l, n: (l, 0, n)),
        out_shape=jax.ShapeDtypeStruct((DEPTH, COND_ROWS, n_out), F32),
        compiler_params=_params(2),
        name="modulation",
    )(cond, ada_w, ada_b.reshape(DEPTH, 1, n_out))


def _mod_spec(layer, part, row_fn):
    return pl.BlockSpec((None, None, 1, D_MODEL), lambda m: (layer, row_fn(m), 0, part))


def _norm_mod(x, r, g, sc, sh):
    return ((x * r) * g) * (1.0 + sc) + sh


class _Seg:
    def __init__(self, kind, width, cache=None):
        self.kind = kind
        self.width = width
        self.cache = cache


def _swap_halves(a, first_half):
    return jnp.where(first_half, pltpu.roll(a, 96, axis=1), pltpu.roll(a, 32, axis=1))


def _inproj_kernel(*refs, segs, rope, seq_len, bm, n_prev, first_fill, n_wtiles, prenorm):
    it = iter(refs)
    h_ref = next(it)
    if prenorm:
        gn_ref, sh_ref, sc_ref = next(it), next(it), next(it)
    w_refs = [next(it) for _ in range(n_wtiles)]
    gain_refs = [tuple(next(it) for _ in range(2 if rope else 1)) if s.kind == "norm" else None
                 for s in segs]
    for _ in range(n_prev):
        next(it)
    out_refs = [next(it) for _ in segs]
    cache_refs = [next(it) if s.cache else None for s in segs]
    if prenorm:
        x_ref, h_ref = h_ref, next(it)
        x = x_ref[...]
        r = lax.rsqrt(jnp.mean(x * x, axis=-1, keepdims=True) + EPS)
        h_ref[...] = _norm_mod(x, r, gn_ref[...], sc_ref[...], sh_ref[...]).astype(BF16)

    if rope:
        lane = lax.broadcasted_iota(jnp.int32, (bm, HEAD_DIM), 1)
        first_half = (lane % 64) < 32

    chunks = []
    col = 0
    for si, seg in enumerate(segs):
        for c0 in range(0, seg.width, PROJ_CHUNK):
            chunks.append((si, c0, min(PROJ_CHUNK, seg.width - c0), col + c0))
        col += seg.width
    chunks.sort(key=lambda chunk: ("norm", "silu", "plain").index(segs[chunk[0]].kind))

    def matmul(chunk):
        _, _, cw, wc = chunk
        off = wc % PROJ_CHUNK
        return _dot(h_ref[...], w_refs[wc // PROJ_CHUNK][:, off:off + cw])

    def epilogue(chunk, acc):
        si, c0, cw, _ = chunk
        seg, gain_ref, out_ref, cache_ref = segs[si], gain_refs[si], out_refs[si], cache_refs[si]
        if seg.kind == "norm":
            heads = []
            for hh in range(cw // HEAD_DIM):
                a = acc[:, hh * HEAD_DIM:(hh + 1) * HEAD_DIM]
                r = lax.rsqrt(jnp.mean(a * a, axis=-1, keepdims=True) + EPS)
                if rope:
                    y = (a * gain_ref[0][...] + _swap_halves(a, first_half) * gain_ref[1][...]) * r
                else:
                    y = (a * r) * gain_ref[0][...]
                heads.append(y)
            val = jnp.concatenate(heads, axis=1) if len(heads) > 1 else heads[0]
        elif seg.kind == "silu":
            val = _silu(acc)
        else:
            val = acc
        out_ref[:, c0:c0 + cw] = val.astype(BF16)
        if seg.cache:
            _, hw = seg.cache
            for j in range(bm // seq_len):
                for hh in range(cw // hw):
                    head = (c0 // hw) + hh
                    data = val[j * seq_len:(j + 1) * seq_len, hh * hw:(hh + 1) * hw]
                    if first_fill is None:
                        cache_ref[j, head] = data
                    else:
                        for slot in range(CACHE_SLOTS):
                            cache_ref[j, slot, head] = data if slot == first_fill else jnp.zeros_like(data)

    acc = matmul(chunks[0])
    for idx, chunk in enumerate(chunks):
        nxt = matmul(chunks[idx + 1]) if idx + 1 < len(chunks) else None
        epilogue(chunk, acc)
        acc = nxt


def _weight_tile_specs(weights):
    w, layer, first, n = weights
    specs = [pl.BlockSpec((None, D_MODEL, PROJ_CHUNK), lambda m, t=first + t: (layer, 0, t),
                          pipeline_mode=pl.Buffered(1)) for t in range(n)]
    return specs, [w] * n


def _inproj(h, weights, segs, gains, rope_tabs, seq_len, cache_slot=0, cache_prev=None, bm=INPROJ_ROWS,
            prenorm=None):
    m_rows = h.shape[0]
    rope = rope_tabs is not None
    w_specs, w_args = _weight_tile_specs(weights)
    assert sum(s.width for s in segs) == len(w_args) * PROJ_CHUNK
    in_specs = [pl.BlockSpec((bm, D_MODEL), lambda m: (m, 0))]
    args = [h]
    scratch = []
    if prenorm is not None:
        norm_g3, mod4, layer, row_fn = prenorm
        in_specs += [pl.BlockSpec((None, 1, D_MODEL), lambda m: (layer, 0, 0)),
                     _mod_spec(layer, 0, row_fn), _mod_spec(layer, 1, row_fn)]
        args += [norm_g3, mod4, mod4]
        scratch = [pltpu.VMEM((bm, D_MODEL), BF16)]
    in_specs += w_specs
    args += w_args
    if rope:
        cos, sin_signed = rope_tabs
        blocks_per_seq = seq_len // bm
        tab_spec = pl.BlockSpec((bm, HEAD_DIM), lambda m: (m % blocks_per_seq, 0))
    for seg, g in zip(segs, gains):
        if seg.kind != "norm":
            continue
        if rope:
            g_swapped = g.reshape(2, 2, HEAD_DIM // 4)[:, ::-1].reshape(HEAD_DIM)
            in_specs += [tab_spec, tab_spec]
            args += [cos * g[None, :], sin_signed * g_swapped[None, :]]
        else:
            in_specs.append(pl.BlockSpec((1, HEAD_DIM), lambda m: (0, 0)))
            args.append(g.reshape(1, HEAD_DIM))
    n_prev = 0 if cache_prev is None else len(cache_prev)
    aliases = {}
    for k in range(n_prev):
        aliases[len(args)] = len(segs) + k
        in_specs.append(pl.BlockSpec(memory_space=pl.ANY))
        args.append(cache_prev[k])
    out_specs = [pl.BlockSpec((bm, s.width), lambda m: (m, 0)) for s in segs]
    out_shape = [jax.ShapeDtypeStruct((m_rows, s.width), BF16) for s in segs]
    for s in segs:
        if s.cache:
            nh, hw = s.cache
            if cache_prev is None:
                out_specs.append(pl.BlockSpec((bm // seq_len, CACHE_SLOTS, nh, seq_len, hw),
                                              lambda m: (m, 0, 0, 0, 0)))
            else:
                out_specs.append(pl.BlockSpec((bm // seq_len, None, nh, seq_len, hw),
                                              lambda m: (m, cache_slot, 0, 0, 0)))
            out_shape.append(jax.ShapeDtypeStruct((m_rows // seq_len, CACHE_SLOTS, nh, seq_len, hw), F32))
    outs = pl.pallas_call(
        functools.partial(_inproj_kernel, segs=segs, rope=rope, seq_len=seq_len, bm=bm, n_prev=n_prev,
                          first_fill=cache_slot if cache_prev is None else None, n_wtiles=len(w_args),
                          prenorm=prenorm is not None),
        grid=(m_rows // bm,),
        in_specs=in_specs,
        out_specs=out_specs,
        out_shape=out_shape,
        input_output_aliases=aliases,
        scratch_shapes=scratch,
        compiler_params=_params(),
        name="inproj",
    )(*args)
    n = len(segs)
    return list(outs[:n]), list(outs[n:])


def _softmax_pv(scores, values, sinks=None, joint=False, ones_cols=False):
    groups = 1 if sinks is None else len(sinks)
    rows = scores[0].shape[0] // groups
    dv = values[0].shape[1] - (HEAD_DIM if ones_cols else 0)

    def weighted(e, v):
        pv = _dot(e, v)
        return (pv[:, dv:], pv[:, :dv]) if ones_cols else (None, pv)

    if joint or len(scores) == 1:
        probs = [[] for _ in scores]
        sink_terms, sums = [], []
        for g in range(groups):
            rs = slice(g * rows, (g + 1) * rows)
            m = jnp.max(scores[0][rs], axis=-1, keepdims=True)
            for s in scores[1:]:
                m = jnp.maximum(m, jnp.max(s[rs], axis=-1, keepdims=True))
            if sinks is not None:
                m = jnp.maximum(m, sinks[g])
            sink_terms.append(None if sinks is None else jnp.exp2(sinks[g] - m))
            l = None
            for i, s in enumerate(scores):
                e = jnp.exp2(s[rs] - m)
                if not ones_cols:
                    part = jnp.sum(e, axis=-1, keepdims=True)
                    l = part if l is None else l + part
                probs[i].append(e.astype(BF16))
            sums.append(l)
        o, l_all = None, None
        for p, v in zip(probs, values):
            l_part, pv = weighted(p[0] if groups == 1 else jnp.concatenate(p, axis=0), v)
            o = pv if o is None else o + pv
            if ones_cols:
                l_all = l_part if l_all is None else l_all + l_part
        outs = []
        for g in range(groups):
            rs = slice(g * rows, (g + 1) * rows)
            l = l_all[rs] if ones_cols else sums[g]
            if sink_terms[g] is not None:
                l = l + sink_terms[g]
            outs.append(o[rs] * (1.0 / l))
        return outs
    parts = []
    for s, v in zip(scores, values):
        m = jnp.max(s, axis=-1, keepdims=True)
        e = jnp.exp2(s - m)
        l, pv = weighted(e.astype(BF16), v)
        parts.append((m, jnp.sum(e, axis=-1, keepdims=True) if l is None else l, pv))
    outs = []
    for g in range(groups):
        rs = slice(g * rows, (g + 1) * rows)
        m_all = parts[0][0][rs]
        for m, _, _ in parts[1:]:
            m_all = jnp.maximum(m_all, m[rs])
        den = None
        if sinks is not None:
            m_all = jnp.maximum(m_all, sinks[g])
            den = jnp.exp2(sinks[g] - m_all)
        num = None
        for m, l, o in parts:
            w = jnp.exp2(m[rs] - m_all)
            den = l[rs] * w if den is None else den + l[rs] * w
            num = o[rs] * w if num is None else num + o[rs] * w
        outs.append(num * (1.0 / den))
    return outs


def _stack_heads(ref, r0, rows, first_head):
    return jnp.concatenate(
        [ref[pl.ds(r0, rows), (first_head + g) * HEAD_DIM:(first_head + g + 1) * HEAD_DIM]
         for g in range(GROUP)], axis=0)


def _head_sinks(sink_ref, first_head):
    return [sink_ref[first_head + g] * LOG2E for g in range(GROUP)]


def _key_chunks(scores, values, q, k_ref, v_ref, k0, seq_len, cols_k, cols_v, chunk=KEY_CHUNK,
                ones_cols=False):
    step = min(seq_len, chunk)
    for c in range(0, seq_len, step):
        scores.append(_dot_nt(q, k_ref[pl.ds(k0 + c, step), cols_k]))
        v = v_ref[pl.ds(k0 + c, step), cols_v]
        if ones_cols:
            v = jnp.concatenate([v, jnp.ones((step, HEAD_DIM), BF16)], axis=1)
        values.append(v)


def _store_gated(o_ref, g_ref, o, r0, rows, first_head):
    if len(o) == 1:
        o = [o[0][g * rows:(g + 1) * rows] for g in range(GROUP)]
    for g in range(GROUP):
        cs = slice((first_head + g) * HEAD_DIM, (first_head + g + 1) * HEAD_DIM)
        gate = g_ref[pl.ds(r0, rows), cs].astype(F32)
        o_ref[pl.ds(r0, rows), cs] = (o[g] * gate).astype(BF16)


def _gqa_kernel(*refs, seq_len, qb, n_blocks, ctx, sink):
    it = iter(refs)
    sink_ref = next(it) if sink else None
    q_ref, k_ref, v_ref, g_ref = next(it), next(it), next(it), next(it)
    if ctx:
        ck_ref, cv_ref = next(it), next(it)
    o_ref = next(it)
    if ctx:
        kc_s, vc_s = next(it), next(it)
        for kv in range(N_KV):
            kc_s[kv] = ck_ref[kv].astype(BF16)
            vc_s[kv, :, 0:HEAD_DIM] = cv_ref[kv].astype(BF16)
            vc_s[kv, :, HEAD_DIM:2 * HEAD_DIM] = jnp.ones(cv_ref.shape[1:], BF16)
    blocks_per_seq = seq_len // qb

    def body(n, carry):
        r0 = pl.multiple_of(n * qb, qb)
        k0 = pl.multiple_of((n // blocks_per_seq) * seq_len, seq_len)
        for kv in range(N_KV):
            hs = slice(kv * HEAD_DIM, (kv + 1) * HEAD_DIM)
            q = _stack_heads(q_ref, r0, qb, kv * GROUP)
            scores, values = [], []
            if ctx:
                scores.append(_dot_nt(q, kc_s[kv]))
                values.append(vc_s[kv])
            _key_chunks(scores, values, q, k_ref, v_ref, k0, seq_len, hs, hs, ones_cols=True)
            sinks = _head_sinks(sink_ref, kv * GROUP) if sink else None
            o = _softmax_pv(scores, values, sinks, ones_cols=True)
            _store_gated(o_ref, g_ref, o, r0, qb, kv * GROUP)
        return carry

    lax.fori_loop(0, n_blocks, body, 0, unroll=UNROLL_CTX if ctx else UNROLL_SELF)


def _gqa(q, k, v, gate, seq_len, qb, rows_per_step, sink=None, ctx=None, layer_idx=0):
    m_rows = q.shape[0]
    kvw = N_KV * HEAD_DIM
    in_specs, args = [], []
    if sink is not None:
        in_specs.append(pl.BlockSpec(memory_space=pltpu.SMEM))
        args.append(sink)
    in_specs += [
        pl.BlockSpec((rows_per_step, HALF_W), lambda b: (b, 0)),
        pl.BlockSpec((rows_per_step, kvw), lambda b: (b, 0)),
        pl.BlockSpec((rows_per_step, kvw), lambda b: (b, 0)),
        pl.BlockSpec((rows_per_step, HALF_W), lambda b: (b, 0)),
    ]
    args += [q, k, v, gate]
    scratch = []
    if ctx is not None:
        past = ctx[0].shape[-2]
        cspec = pl.BlockSpec((None, None, N_KV, past, HEAD_DIM), lambda b: (b, layer_idx, 0, 0, 0))
        in_specs += [cspec, cspec]
        args += list(ctx)
        scratch = [pltpu.VMEM((N_KV, past, HEAD_DIM), BF16), pltpu.VMEM((N_KV, past, 2 * HEAD_DIM), BF16)]
    return pl.pallas_call(
        functools.partial(_gqa_kernel, seq_len=seq_len, qb=qb, n_blocks=rows_per_step // qb,
                          ctx=ctx is not None, sink=sink is not None),
        grid=(m_rows // rows_per_step,),
        in_specs=in_specs,
        out_specs=pl.BlockSpec((rows_per_step, HALF_W), lambda b: (b, 0)),
        out_shape=jax.ShapeDtypeStruct((m_rows, HALF_W), BF16),
        scratch_shapes=scratch,
        compiler_params=_params(),
        name="gqa",
    )(*args)


def _banded_kernel(sink_ref, q_ref, k_ref, v_ref, g_ref, ck_ref, cv_ref, o_ref,
                   kpad, vpad, kc_s, vc_s, *, seq_len):
    n_blocks = seq_len // WINDOW
    zeros = jnp.zeros((WINDOW, HEAD_DIM), BF16)
    for kv in range(N_KV):
        hs = slice(kv * HEAD_DIM, (kv + 1) * HEAD_DIM)
        kpad[kv, 0:WINDOW] = zeros
        vpad[kv, 0:WINDOW, 0:HEAD_DIM] = zeros
        kpad[kv, WINDOW + seq_len:2 * WINDOW + seq_len] = zeros
        vpad[kv, WINDOW + seq_len:2 * WINDOW + seq_len, 0:HEAD_DIM] = zeros
        kpad[kv, WINDOW:WINDOW + seq_len] = k_ref[:, hs]
        vpad[kv, WINDOW:WINDOW + seq_len, 0:HEAD_DIM] = v_ref[:, hs]
        kc_s[kv] = ck_ref[kv].astype(BF16)
        vc_s[kv, :, 0:HEAD_DIM] = cv_ref[kv].astype(BF16)
        vpad[kv, :, HEAD_DIM:2 * HEAD_DIM] = jnp.ones((seq_len + 2 * WINDOW, HEAD_DIM), BF16)
        vc_s[kv, :, HEAD_DIM:2 * HEAD_DIM] = jnp.ones(cv_ref.shape[1:], BF16)

    rows = GROUP * WINDOW
    qi = lax.broadcasted_iota(jnp.int32, (rows, 3 * WINDOW), 0) % WINDOW
    col = lax.broadcasted_iota(jnp.int32, (rows, 3 * WINDOW), 1)
    kj = col % WINDOW
    blk = col // WINDOW
    never = 4 * WINDOW

    def body(n, carry):
        r0 = pl.multiple_of(n * WINDOW, WINDOW)
        lo = jnp.where(n > 0, 0, never)
        hi = jnp.where(n < n_blocks - 1, 0, never)
        slack = jnp.where(blk == 0, kj - qi - lo, jnp.where(blk == 2, qi - kj - hi, 0))
        valid = slack >= 0
        for kv in range(N_KV):
            q = _stack_heads(q_ref, r0, WINDOW, kv * GROUP)
            kb = kpad[kv, pl.ds(r0, 3 * WINDOW), :]
            vb = vpad[kv, pl.ds(r0, 3 * WINDOW), :]
            s_ctx = _dot_nt(q, kc_s[kv])
            s_band = jnp.where(valid, _dot_nt(q, kb), NEG_INF)
            o = _softmax_pv([s_ctx, s_band], [vc_s[kv], vb], _head_sinks(sink_ref, kv * GROUP),
                            ones_cols=True)
            _store_gated(o_ref, g_ref, o, r0, WINDOW, kv * GROUP)
        return carry

    lax.fori_loop(0, n_blocks, body, 0, unroll=UNROLL_CTX)


def _banded(q, k, v, gate, sink, cache_k, cache_v, layer_idx, seq_len):
    m_rows = q.shape[0]
    kvw = N_KV * HEAD_DIM
    past = cache_k.shape[-2]
    cspec = pl.BlockSpec((None, None, N_KV, past, HEAD_DIM), lambda b: (b, layer_idx, 0, 0, 0))
    pad_rows = seq_len + 2 * WINDOW
    return pl.pallas_call(
        functools.partial(_banded_kernel, seq_len=seq_len),
        grid=(m_rows // seq_len,),
        in_specs=[
            pl.BlockSpec(memory_space=pltpu.SMEM),
            pl.BlockSpec((seq_len, HALF_W), lambda b: (b, 0)),
            pl.BlockSpec((seq_len, kvw), lambda b: (b, 0)),
            pl.BlockSpec((seq_len, kvw), lambda b: (b, 0)),
            pl.BlockSpec((seq_len, HALF_W), lambda b: (b, 0)),
            cspec, cspec,
        ],
        out_specs=pl.BlockSpec((seq_len, HALF_W), lambda b: (b, 0)),
        out_shape=jax.ShapeDtypeStruct((m_rows, HALF_W), BF16),
        scratch_shapes=[
            pltpu.VMEM((N_KV, pad_rows, HEAD_DIM), BF16),
            pltpu.VMEM((N_KV, pad_rows, 2 * HEAD_DIM), BF16),
            pltpu.VMEM((N_KV, past, HEAD_DIM), BF16),
            pltpu.VMEM((N_KV, past, 2 * HEAD_DIM), BF16),
        ],
        compiler_params=_params(),
        name="banded",
    )(sink, q, k, v, gate, cache_k, cache_v)


def _diff_kernel(*refs, seq_len, qb, n_blocks, ctx, lam_init):
    it = iter(refs)
    lam_ref, on_ref = next(it), next(it)
    q_ref, k_ref, v_ref, g_ref = next(it), next(it), next(it), next(it)
    if ctx:
        ck_ref, cv_ref = next(it), next(it)
    o_ref = next(it)
    if ctx:
        kc_s, vc_s = next(it), next(it)
        for slot in range(2 * C_HEADS):
            kc_s[slot] = ck_ref[slot].astype(BF16)
        for hd in range(C_HEADS):
            vc_s[hd] = cv_ref[hd].astype(BF16)
    lf = lam_ref[...]
    lam = (jnp.exp(jnp.sum(lf[0:1] * lf[1:2], axis=-1, keepdims=True))
           - jnp.exp(jnp.sum(lf[2:3] * lf[3:4], axis=-1, keepdims=True)) + lam_init)
    out_gain = on_ref[...]
    blocks_per_seq = seq_len // qb

    def body(n, carry):
        r0 = pl.multiple_of(n * qb, qb)
        k0 = pl.multiple_of((n // blocks_per_seq) * seq_len, seq_len)
        for hd in range(C_HEADS):
            vs = slice(hd * C_V_DIM, (hd + 1) * C_V_DIM)
            maps = []
            for mp in range(2):
                slot = mp * C_HEADS + hd
                hs = slice(slot * HEAD_DIM, (slot + 1) * HEAD_DIM)
                q = q_ref[pl.ds(r0, qb), hs]
                scores, values = [], []
                if ctx:
                    scores.append(_dot_nt(q, kc_s[slot]))
                    values.append(vc_s[hd])
                _key_chunks(scores, values, q, k_ref, v_ref, k0, seq_len, hs, vs, chunk=seq_len)
                maps.append(_softmax_pv(scores, values, joint=True)[0])
            d = maps[0] - lam * maps[1]
            r = lax.rsqrt(jnp.mean(d * d, axis=-1, keepdims=True) + EPS)
            y = ((d * r) * out_gain) * (1.0 - lam_init)
            gate = g_ref[pl.ds(r0, qb), vs].astype(F32)
            o_ref[pl.ds(r0, qb), vs] = (y * gate).astype(BF16)
        return carry

    lax.fori_loop(0, n_blocks, body, 0, unroll=UNROLL_CTX)


def _diff(q, k, v, gate, c_lam, c_on, lam_init, seq_len, qb, rows_per_step, ctx=None, layer_idx=0):
    m_rows = q.shape[0]
    in_specs = [
        pl.BlockSpec((4, HEAD_DIM), lambda b: (0, 0)),
        pl.BlockSpec((1, C_V_DIM), lambda b: (0, 0)),
    ] + [pl.BlockSpec((rows_per_step, HALF_W), lambda b: (b, 0))] * 4
    args = [c_lam, c_on, q, k, v, gate]
    scratch = []
    if ctx is not None:
        past = ctx[0].shape[-2]
        in_specs += [
            pl.BlockSpec((None, None, 2 * C_HEADS, past, HEAD_DIM), lambda b: (b, layer_idx, 0, 0, 0)),
            pl.BlockSpec((None, None, C_HEADS, past, C_V_DIM), lambda b: (b, layer_idx, 0, 0, 0)),
        ]
        args += list(ctx)
        scratch = [pltpu.VMEM((2 * C_HEADS, past, HEAD_DIM), BF16),
                   pltpu.VMEM((C_HEADS, past, C_V_DIM), BF16)]
    return pl.pallas_call(
        functools.partial(_diff_kernel, seq_len=seq_len, qb=qb, n_blocks=rows_per_step // qb,
                          ctx=ctx is not None, lam_init=lam_init),
        grid=(m_rows // rows_per_step,),
        in_specs=in_specs,
        out_specs=pl.BlockSpec((rows_per_step, HALF_W), lambda b: (b, 0)),
        out_shape=jax.ShapeDtypeStruct((m_rows, HALF_W), BF16),
        scratch_shapes=scratch,
        compiler_params=_params(),
        name="diff_attn",
    )(*args)


def _outproj_kernel(*refs, bm, emit_h, conv_seq):
    it = iter(refs)
    y1_ref = next(it)
    if conv_seq is None:
        y2_ref = next(it)
    else:
        u_ref, b_ref, c_ref, g_ref = next(it), next(it), next(it), next(it)
        up_ref, cp_ref, un_ref, cn_ref = next(it), next(it), next(it), next(it)
        cw_ref = next(it)
    x_ref = next(it)
    w_refs = [next(it) for _ in range(D_MODEL // PROJ_CHUNK)]
    gt_ref = next(it)
    if emit_h:
        gn_ref, sh_ref, sc_ref = next(it), next(it), next(it)
    xo_ref = next(it)
    if emit_h:
        ho_ref = next(it)
    chunks = [slice(c, c + PROJ_CHUNK) for c in range(0, D_MODEL, PROJ_CHUNK)]
    w_top = [w.at[0:HALF_W, :] for w in w_refs]
    w_bot = [w.at[HALF_W:D_MODEL, :] for w in w_refs]
    if conv_seq is not None:
        assert bm % conv_seq == 0 or conv_seq % bm == 0
        row = lax.broadcasted_iota(jnp.int32, (bm, 1), 0)
        pos = (pl.program_id(0) * bm + row) % conv_seq
        first_row = jnp.logical_or(pos == 0, row == 0)
        last_row = jnp.logical_or(pos == conv_seq - 1, row == bm - 1)
        tile_pos = (pl.program_id(0) * bm) % conv_seq
        use_before = jnp.where(tile_pos == 0, 0.0, 1.0)
        use_after = jnp.where(tile_pos + bm == conv_seq, 0.0, 1.0) if conv_seq % bm == 0 else 0.0
        lanes = HALF_W // len(chunks)
        y2_parts = []
        for ci, cs in enumerate(chunks):
            ls = slice(ci * lanes, (ci + 1) * lanes)
            z = c_ref[:, ls].astype(F32) * u_ref[:, ls].astype(F32)
            z_before = (cp_ref[:, ls].astype(F32) * up_ref[:, ls].astype(F32))[HALO_ROWS - 1:HALO_ROWS] * use_before
            z_after = (cn_ref[:, ls].astype(F32) * un_ref[:, ls].astype(F32))[0:1] * use_after
            z_prev = jnp.where(first_row, z_before, pltpu.roll(z, 1, axis=0))
            z_next = jnp.where(last_row, z_after, pltpu.roll(z, bm - 1, axis=0))
            conv = z_prev * cw_ref[0:1, ls] + z * cw_ref[1:2, ls] + z_next * cw_ref[2:3, ls]
            y2_parts.append(((b_ref[:, ls].astype(F32) * conv) * g_ref[:, ls].astype(F32)).astype(BF16))
            xo_ref[:, cs] = x_ref[:, cs] + gt_ref[:, cs] * _dot(y1_ref[...], w_top[ci][...])
        y2 = jnp.concatenate(y2_parts, axis=1)
    ssq = jnp.zeros((bm, 1), F32)
    for ci, cs in enumerate(chunks):
        if conv_seq is None:
            acc = _dot(y1_ref[...], w_top[ci][...]) + _dot(y2_ref[...], w_bot[ci][...])
            xn = x_ref[:, cs] + gt_ref[:, cs] * acc
        else:
            xn = xo_ref[:, cs] + gt_ref[:, cs] * _dot(y2, w_bot[ci][...])
        xo_ref[:, cs] = xn
        if emit_h:
            ssq = ssq + jnp.sum(xn * xn, axis=-1, keepdims=True)
    if emit_h:
        r = lax.rsqrt(ssq * (1.0 / D_MODEL) + EPS)
        for cs in chunks:
            gain = gn_ref[:, cs] * (1.0 + sc_ref[:, cs])
            ho_ref[:, cs] = ((xo_ref[:, cs] * r) * gain + sh_ref[:, cs]).astype(BF16)


def _outproj(y1, y2, x, w_out, norm_g3, mod4, layer, row_fn, emit_h, conv=None):
    m_rows = x.shape[0]
    bm = PROJ_ROWS
    half_spec = pl.BlockSpec((bm, HALF_W), lambda m: (m, 0))
    in_specs = [half_spec]
    args = [y1]
    scratch = []
    if conv is None:
        in_specs.append(half_spec)
        args.append(y2)
    else:
        u, bgate, cgate, gate, conv_w, _ = conv
        halo_blocks = bm // HALO_ROWS
        last_halo = m_rows // HALO_ROWS - 1
        before = pl.BlockSpec((HALO_ROWS, HALF_W), lambda m: (jnp.maximum(m * halo_blocks - 1, 0), 0))
        after = pl.BlockSpec((HALO_ROWS, HALF_W), lambda m: (jnp.minimum((m + 1) * halo_blocks, last_halo), 0))
        in_specs += [half_spec] * 4 + [before, before, after, after,
                                       pl.BlockSpec((3, HALF_W), lambda m: (0, 0))]
        args += [u, bgate, cgate, gate, u, cgate, u, cgate, conv_w]
    w_specs, w_args = _weight_tile_specs(w_out)
    in_specs += [pl.BlockSpec((bm, D_MODEL), lambda m: (m, 0))] + w_specs + [_mod_spec(layer, 2, row_fn)]
    args += [x] + w_args + [mod4]
    out_specs = [pl.BlockSpec((bm, D_MODEL), lambda m: (m, 0))]
    out_shape = [jax.ShapeDtypeStruct((m_rows, D_MODEL), F32)]
    if emit_h:
        in_specs += [
            pl.BlockSpec((None, 1, D_MODEL), lambda m: (layer + 1, 0, 0)),
            _mod_spec(layer + 1, 0, row_fn),
            _mod_spec(layer + 1, 1, row_fn),
        ]
        args += [norm_g3, mod4, mod4]
        out_specs.append(pl.BlockSpec((bm, D_MODEL), lambda m: (m, 0)))
        out_shape.append(jax.ShapeDtypeStruct((m_rows, D_MODEL), BF16))
    outs = pl.pallas_call(
        functools.partial(_outproj_kernel, bm=bm, emit_h=emit_h,
                          conv_seq=None if conv is None else conv[5]),
        grid=(m_rows // bm,),
        in_specs=in_specs,
        out_specs=out_specs,
        out_shape=out_shape,
        scratch_shapes=scratch,
        compiler_params=_params(),
        name="outproj",
    )(*args)
    return (outs[0], outs[1]) if emit_h else (outs[0], None)


def _rope_tables(n_tokens):
    t = jnp.arange(n_tokens)
    row = t // GRID_W
    col = t % GRID_W
    n_freq = HEAD_DIM // 4
    inv = ROPE_THETA ** (-jnp.arange(n_freq, dtype=F32) / n_freq)
    ang_r = row[:, None] * inv
    ang_c = col[:, None] * inv
    cos = jnp.concatenate([jnp.cos(ang_r)] * 2 + [jnp.cos(ang_c)] * 2, axis=1)
    sin = jnp.concatenate([-jnp.sin(ang_r), jnp.sin(ang_r), -jnp.sin(ang_c), jnp.sin(ang_c)], axis=1)
    return cos, sin


def kernel(x_prompt, x_sample, cache_a_k, cache_a_v, cache_c_k, cache_c_v, cache_d_k, cache_d_v,
           c, c_ctx, norm_g, ada_w, ada_b, ev_w_in, ev_w_out, a_q_norm, a_k_norm, a_sink, b_conv,
           od_w_in, od_w_out, c_q_norm, c_k_norm, c_lambda, c_out_norm, d_q_norm, d_k_norm):
    n_prompt, seq_p, _ = x_prompt.shape
    n_sample, seq_s, _ = x_sample.shape
    past = cache_a_k.shape[-2]
    scale = HEAD_DIM ** -0.5 * LOG2E

    cond = jnp.concatenate(
        [c, c_ctx[None, :], jnp.zeros((COND_ROWS - n_sample - 1, D_MODEL), F32)], axis=0)
    mod4 = _modulation(cond, ada_w, ada_b).reshape(DEPTH, COND_ROWS, 1, 3 * D_MODEL)
    norm_g3 = norm_g.reshape(DEPTH, 1, D_MODEL)
    rope_tabs = _rope_tables(seq_s)

    prompt_row = lambda m: CTX_ROW
    sample_row = lambda m: (m * PROJ_ROWS) // seq_s

    xp = x_prompt.reshape(n_prompt * seq_p, D_MODEL)
    xs = x_sample.reshape(n_sample * seq_s, D_MODEL)
    hp, hs = xp, xs

    ev_in_tiles, ev_out_tiles = ev_w_in.astype(BF16), ev_w_out.astype(BF16)
    od_in_tiles, od_out_tiles = od_w_in.astype(BF16), od_w_out.astype(BF16)
    cache_c_k8 = cache_c_k.reshape(n_sample, DEPTH // 2, 2 * C_HEADS, past, HEAD_DIM)
    prompt_rows = PROMPT_SEQS_PER_STEP * seq_p

    new_a = None
    new_c = (None, None)
    for layer in range(DEPTH):
        i = layer // 2
        last = layer == DEPTH - 1
        if layer % 2 == 0:
            w_out = (ev_out_tiles, i, 0, D_MODEL // PROJ_CHUNK)
            gq = a_q_norm[i] * scale
            gk = a_k_norm[i]
            kv_cache = (N_KV, HEAD_DIM)

            def project(h, seq_len, tabs, cache, prev, row_fn):
                segs1 = [_Seg("norm", HALF_W), _Seg("norm", N_KV * HEAD_DIM, cache),
                         _Seg("plain", N_KV * HEAD_DIM, cache), _Seg("silu", HALF_W)]
                segs_all = segs1 + [_Seg("plain", HALF_W)] * 3 + [_Seg("silu", HALF_W)]
                outs, caches = _inproj(h, (ev_in_tiles, i, 0, ALL_TILES), segs_all, [gq, gk] + [None] * 6,
                                       tabs, seq_len, cache_slot=i, cache_prev=prev, bm=PROJ_ROWS,
                                       prenorm=(norm_g3, mod4, layer, row_fn) if layer == 0 else None)
                return (*outs, caches)

            q, k, v, ga, u, bg, cg, gb, new_a = project(hp, seq_p, None, kv_cache, new_a, prompt_row)
            y1p = _gqa(q, k, v, ga, seq_p, seq_p, prompt_rows, sink=a_sink[i])
            y2p, conv_p = None, (u, bg, cg, gb, b_conv[i], seq_p)

            q, k, v, ga, u, bg, cg, gb, _ = project(hs, seq_s, rope_tabs, None, None, sample_row)
            y1s = _banded(q, k, v, ga, a_sink[i], cache_a_k, cache_a_v, i, seq_s)
            y2s, conv_s = None, (u, bg, cg, gb, b_conv[i], seq_s)
        else:
            lam_init = 0.8 - 0.6 * math.exp(-0.3 * layer)
            w_out = (od_out_tiles, i, 0, D_MODEL // PROJ_CHUNK)
            gcq = c_q_norm[i] * scale
            gck = c_k_norm[i]
            gdq = d_q_norm[i] * scale
            gdk = d_k_norm[i]
            c_on = c_out_norm[i].reshape(1, C_V_DIM)

            def project(h, seq_len, tabs, with_cache, prev=(None, None)):
                ck_cache = (2 * C_HEADS, HEAD_DIM) if with_cache else None
                cv_cache = (C_HEADS, C_V_DIM) if with_cache else None
                kv_cache = (N_KV, HEAD_DIM) if with_cache else None
                segs_c = [_Seg("norm", HALF_W), _Seg("norm", HALF_W, ck_cache),
                          _Seg("plain", HALF_W, cv_cache), _Seg("silu", HALF_W)]
                segs_d = [_Seg("norm", HALF_W), _Seg("norm", N_KV * HEAD_DIM, kv_cache),
                          _Seg("plain", N_KV * HEAD_DIM, kv_cache), _Seg("silu", HALF_W)]
                gains_c, gains_d = [gcq, gck, None, None], [gdq, gdk, None, None]
                if not with_cache:
                    outs, _ = _inproj(h, (od_in_tiles, i, 0, ALL_TILES), segs_c + segs_d, gains_c + gains_d,
                                      tabs, seq_len, bm=PROJ_ROWS)
                    return (*outs, None)
                outs_c, cc = _inproj(h, (od_in_tiles, i, 0, C_TILES), segs_c, gains_c, tabs, seq_len,
                                     cache_slot=i, cache_prev=prev[0], bm=PROJ_ROWS)
                outs_d, cd = _inproj(h, (od_in_tiles, i, C_TILES, ALL_TILES - C_TILES), segs_d, gains_d,
                                     tabs, seq_len, cache_slot=i, cache_prev=prev[1])
                return (*outs_c, *outs_d, (cc, cd))

            cq, ck, cv, cgate, dq, dk, dv, dgate, new_c = project(hp, seq_p, None, True, new_c)
            y1p = _diff(cq, ck, cv, cgate, c_lambda[i], c_on, lam_init, seq_p, seq_p, prompt_rows)
            y2p = _gqa(dq, dk, dv, dgate, seq_p, seq_p, prompt_rows)

            cq, ck, cv, cgate, dq, dk, dv, dgate, _ = project(hs, seq_s, rope_tabs, False)
            y1s = _diff(cq, ck, cv, cgate, c_lambda[i], c_on, lam_init, seq_s, 2 * SAMPLE_QB, seq_s,
                        ctx=(cache_c_k8, cache_c_v), layer_idx=i)
            y2s = _gqa(dq, dk, dv, dgate, seq_s, 2 * SAMPLE_QB, seq_s,
                       ctx=(cache_d_k, cache_d_v), layer_idx=i)
            conv_p = conv_s = None

        xp, hp = _outproj(y1p, y2p, xp, w_out, norm_g3, mod4, layer, prompt_row, not last, conv_p)
        xs, hs = _outproj(y1s, y2s, xs, w_out, norm_g3, mod4, layer, sample_row, not last, conv_s)

    return (xp.reshape(n_prompt, seq_p, D_MODEL), xs.reshape(n_sample, seq_s, D_MODEL),
            new_a[0], new_a[1],
            new_c[0][0].reshape(n_prompt, CACHE_SLOTS, 2, C_HEADS, seq_p, HEAD_DIM), new_c[0][1],
            new_c[1][0], new_c[1][1])
```

```python
import functools
import math

import jax
import jax.numpy as jnp
from jax import lax
from jax.experimental import pallas as pl
from jax.experimental.pallas import tpu as pltpu

D_MODEL = 2048
DEPTH = 4
GRID_W = 64
HEAD_DIM = 128
WINDOW = 128
ROPE_THETA = 10000.0
EPS = 1e-6
NEG_INF = -1e30
HALF_W = D_MODEL // 2
N_KV = 2
GROUP = 4
C_HEADS = 4
C_V_DIM = 2 * HEAD_DIM
CACHE_SLOTS = DEPTH // 2
COND_ROWS = 16
CTX_ROW = 8

BF16 = jnp.bfloat16
F32 = jnp.float32

VMEM_LIMIT_BYTES = 56 * 1024 * 1024
PROJ_ROWS = 512
INPROJ_ROWS = 1024
ALL_TILES = 13
C_TILES = 8
PROJ_CHUNK = 512
SAMPLE_QB = 128
KEY_CHUNK = 512
UNROLL_SELF = 4
UNROLL_CTX = 2
HALO_ROWS = 16
LOG2E = math.log2(math.e)
PROMPT_SEQS_PER_STEP = 4


def _params(n_axes=1):
    return pltpu.CompilerParams(
        dimension_semantics=("arbitrary",) * n_axes,
        vmem_limit_bytes=VMEM_LIMIT_BYTES,
    )


def _dot(a, b):
    return jnp.dot(a, b, preferred_element_type=F32)


def _dot_nt(a, b):
    return lax.dot_general(a, b, (((1,), (1,)), ((), ())), preferred_element_type=F32)


def _silu(x):
    return x / (1.0 + jnp.exp(-x))


def _mod_kernel(cond_ref, w_ref, b_ref, o_ref):
    s = _silu(cond_ref[...]).astype(BF16)
    o_ref[...] = _dot(s, w_ref[...].astype(BF16)) + b_ref[...]


def _modulation(cond, ada_w, ada_b):
    bn = 1024
    n_out = 3 * D_MODEL
    return pl.pallas_call(
        _mod_kernel,
        grid=(DEPTH, n_out // bn),
        in_specs=[
            pl.BlockSpec((COND_ROWS, D_MODEL), lambda l, n: (0, 0)),
            pl.BlockSpec((None, D_MODEL, bn), lambda l, n: (l, 0, n)),
            pl.BlockSpec((None, 1, bn), lambda l, n: (l, 0, n)),
        ],
        out_specs=pl.BlockSpec((None, COND_ROWS, bn), lambda l, n: (l, 0, n)),
        out_shape=jax.ShapeDtypeStruct((DEPTH, COND_ROWS, n_out), F32),
        compiler_params=_params(2),
        name="modulation",
    )(cond, ada_w, ada_b.reshape(DEPTH, 1, n_out))


def _mod_spec(layer, part, row_fn):
    return pl.BlockSpec((None, None, 1, D_MODEL), lambda m: (layer, row_fn(m), 0, part))


def _norm_mod(x, r, g, sc, sh):
    return ((x * r) * g) * (1.0 + sc) + sh


class _Seg:
    def __init__(self, kind, width, cache=None):
        self.kind = kind
        self.width = width
        self.cache = cache


def _swap_halves(a, first_half):
    return jnp.where(first_half, pltpu.roll(a, 96, axis=1), pltpu.roll(a, 32, axis=1))


def _inproj_kernel(*refs, segs, rope, seq_len, bm, n_prev, first_fill, n_wtiles, prenorm):
    it = iter(refs)
    h_ref = next(it)
    if prenorm:
        gn_ref, sh_ref, sc_ref = next(it), next(it), next(it)
    w_refs = [next(it) for _ in range(n_wtiles)]
    gain_refs = [tuple(next(it) for _ in range(2 if rope else 1)) if s.kind == "norm" else None
                 for s in segs]
    for _ in range(n_prev):
        next(it)
    out_refs = [next(it) for _ in segs]
    cache_refs = [next(it) if s.cache else None for s in segs]
    if prenorm:
        x_ref, h_ref = h_ref, next(it)
        x = x_ref[...]
        r = lax.rsqrt(jnp.mean(x * x, axis=-1, keepdims=True) + EPS)
        h_ref[...] = _norm_mod(x, r, gn_ref[...], sc_ref[...], sh_ref[...]).astype(BF16)

    if rope:
        lane = lax.broadcasted_iota(jnp.int32, (bm, HEAD_DIM), 1)
        first_half = (lane % 64) < 32

    chunks = []
    col = 0
    for si, seg in enumerate(segs):
        for c0 in range(0, seg.width, PROJ_CHUNK):
            chunks.append((si, c0, min(PROJ_CHUNK, seg.width - c0), col + c0))
        col += seg.width
    chunks.sort(key=lambda chunk: ("norm", "silu", "plain").index(segs[chunk[0]].kind))

    def matmul(chunk):
        _, _, cw, wc = chunk
        off = wc % PROJ_CHUNK
        return _dot(h_ref[...], w_refs[wc // PROJ_CHUNK][:, off:off + cw])

    def epilogue(chunk, acc):
        si, c0, cw, _ = chunk
        seg, gain_ref, out_ref, cache_ref = segs[si], gain_refs[si], out_refs[si], cache_refs[si]
        if seg.kind == "norm":
            heads = []
            for hh in range(cw // HEAD_DIM):
                a = acc[:, hh * HEAD_DIM:(hh + 1) * HEAD_DIM]
                r = lax.rsqrt(jnp.mean(a * a, axis=-1, keepdims=True) + EPS)
                if rope:
                    y = (a * gain_ref[0][...] + _swap_halves(a, first_half) * gain_ref[1][...]) * r
                else:
                    y = (a * r) * gain_ref[0][...]
                heads.append(y)
            val = jnp.concatenate(heads, axis=1) if len(heads) > 1 else heads[0]
        elif seg.kind == "silu":
            val = _silu(acc)
        else:
            val = acc
        out_ref[:, c0:c0 + cw] = val.astype(BF16)
        if seg.cache:
            _, hw = seg.cache
            for j in range(bm // seq_len):
                for hh in range(cw // hw):
                    head = (c0 // hw) + hh
                    data = val[j * seq_len:(j + 1) * seq_len, hh * hw:(hh + 1) * hw]
                    if first_fill is None:
                        cache_ref[j, head] = data
                    else:
                        for slot in range(CACHE_SLOTS):
                            cache_ref[j, slot, head] = data if slot == first_fill else jnp.zeros_like(data)

    acc = matmul(chunks[0])
    for idx, chunk in enumerate(chunks):
        nxt = matmul(chunks[idx + 1]) if idx + 1 < len(chunks) else None
        epilogue(chunk, acc)
        acc = nxt


def _weight_tile_specs(weights):
    w, layer, first, n = weights
    specs = [pl.BlockSpec((None, D_MODEL, PROJ_CHUNK), lambda m, t=first + t: (layer, 0, t),
                          pipeline_mode=pl.Buffered(1)) for t in range(n)]
    return specs, [w] * n


def _inproj(h, weights, segs, gains, rope_tabs, seq_len, cache_slot=0, cache_prev=None, bm=INPROJ_ROWS,
            prenorm=None):
    m_rows = h.shape[0]
    rope = rope_tabs is not None
    w_specs, w_args = _weight_tile_specs(weights)
    assert sum(s.width for s in segs) == len(w_args) * PROJ_CHUNK
    in_specs = [pl.BlockSpec((bm, D_MODEL), lambda m: (m, 0))]
    args = [h]
    scratch = []
    if prenorm is not None:
        norm_g3, mod4, layer, row_fn = prenorm
        in_specs += [pl.BlockSpec((None, 1, D_MODEL), lambda m: (layer, 0, 0)),
                     _mod_spec(layer, 0, row_fn), _mod_spec(layer, 1, row_fn)]
        args += [norm_g3, mod4, mod4]
        scratch = [pltpu.VMEM((bm, D_MODEL), BF16)]
    in_specs += w_specs
    args += w_args
    if rope:
        cos, sin_signed = rope_tabs
        blocks_per_seq = seq_len // bm
        tab_spec = pl.BlockSpec((bm, HEAD_DIM), lambda m: (m % blocks_per_seq, 0))
    for seg, g in zip(segs, gains):
        if seg.kind != "norm":
            continue
        if rope:
            g_swapped = g.reshape(2, 2, HEAD_DIM // 4)[:, ::-1].reshape(HEAD_DIM)
            in_specs += [tab_spec, tab_spec]
            args += [cos * g[None, :], sin_signed * g_swapped[None, :]]
        else:
            in_specs.append(pl.BlockSpec((1, HEAD_DIM), lambda m: (0, 0)))
            args.append(g.reshape(1, HEAD_DIM))
    n_prev = 0 if cache_prev is None else len(cache_prev)
    aliases = {}
    for k in range(n_prev):
        aliases[len(args)] = len(segs) + k
        in_specs.append(pl.BlockSpec(memory_space=pl.ANY))
        args.append(cache_prev[k])
    out_specs = [pl.BlockSpec((bm, s.width), lambda m: (m, 0)) for s in segs]
    out_shape = [jax.ShapeDtypeStruct((m_rows, s.width), BF16) for s in segs]
    for s in segs:
        if s.cache:
            nh, hw = s.cache
            if cache_prev is None:
                out_specs.append(pl.BlockSpec((bm // seq_len, CACHE_SLOTS, nh, seq_len, hw),
                                              lambda m: (m, 0, 0, 0, 0)))
            else:
                out_specs.append(pl.BlockSpec((bm // seq_len, None, nh, seq_len, hw),
                                              lambda m: (m, cache_slot, 0, 0, 0)))
            out_shape.append(jax.ShapeDtypeStruct((m_rows // seq_len, CACHE_SLOTS, nh, seq_len, hw), F32))
    outs = pl.pallas_call(
        functools.partial(_inproj_kernel, segs=segs, rope=rope, seq_len=seq_len, bm=bm, n_prev=n_prev,
                          first_fill=cache_slot if cache_prev is None else None, n_wtiles=len(w_args),
                          prenorm=prenorm is not None),
        grid=(m_rows // bm,),
        in_specs=in_specs,
        out_specs=out_specs,
        out_shape=out_shape,
        input_output_aliases=aliases,
        scratch_shapes=scratch,
        compiler_params=_params(),
        name="inproj",
    )(*args)
    n = len(segs)
    return list(outs[:n]), list(outs[n:])


def _softmax_pv(scores, values, sinks=None, joint=False, ones_cols=False):
    groups = 1 if sinks is None else len(sinks)
    rows = scores[0].shape[0] // groups
    dv = values[0].shape[1] - (HEAD_DIM if ones_cols else 0)

    def weighted(e, v):
        pv = _dot(e, v)
        return (pv[:, dv:], pv[:, :dv]) if ones_cols else (None, pv)

    if joint or len(scores) == 1:
        probs = [[] for _ in scores]
        sink_terms, sums = [], []
        for g in range(groups):
            rs = slice(g * rows, (g + 1) * rows)
            m = jnp.max(scores[0][rs], axis=-1, keepdims=True)
            for s in scores[1:]:
                m = jnp.maximum(m, jnp.max(s[rs], axis=-1, keepdims=True))
            if sinks is not None:
                m = jnp.maximum(m, sinks[g])
            sink_terms.append(None if sinks is None else jnp.exp2(sinks[g] - m))
            l = None
            for i, s in enumerate(scores):
                e = jnp.exp2(s[rs] - m)
                if not ones_cols:
                    part = jnp.sum(e, axis=-1, keepdims=True)
                    l = part if l is None else l + part
                probs[i].append(e.astype(BF16))
            sums.append(l)
        o, l_all = None, None
        for p, v in zip(probs, values):
            l_part, pv = weighted(p[0] if groups == 1 else jnp.concatenate(p, axis=0), v)
            o = pv if o is None else o + pv
            if ones_cols:
                l_all = l_part if l_all is None else l_all + l_part
        outs = []
        for g in range(groups):
            rs = slice(g * rows, (g + 1) * rows)
            l = l_all[rs] if ones_cols else sums[g]
            if sink_terms[g] is not None:
                l = l + sink_terms[g]
            outs.append(o[rs] * (1.0 / l))
        return outs
    parts = []
    for s, v in zip(scores, values):
        m = jnp.max(s, axis=-1, keepdims=True)
        e = jnp.exp2(s - m)
        l, pv = weighted(e.astype(BF16), v)
        parts.append((m, jnp.sum(e, axis=-1, keepdims=True) if l is None else l, pv))
    outs = []
    for g in range(groups):
        rs = slice(g * rows, (g + 1) * rows)
        m_all = parts[0][0][rs]
        for m, _, _ in parts[1:]:
            m_all = jnp.maximum(m_all, m[rs])
        den = None
        if sinks is not None:
            m_all = jnp.maximum(m_all, sinks[g])
            den = jnp.exp2(sinks[g] - m_all)
        num = None
        for m, l, o in parts:
            w = jnp.exp2(m[rs] - m_all)
            den = l[rs] * w if den is None else den + l[rs] * w
            num = o[rs] * w if num is None else num + o[rs] * w
        outs.append(num * (1.0 / den))
    return outs


def _stack_heads(ref, r0, rows, first_head):
    return jnp.concatenate(
        [ref[pl.ds(r0, rows), (first_head + g) * HEAD_DIM:(first_head + g + 1) * HEAD_DIM]
         for g in range(GROUP)], axis=0)


def _head_sinks(sink_ref, first_head):
    return [sink_ref[first_head + g] * LOG2E for g in range(GROUP)]


def _key_chunks(scores, values, q, k_ref, v_ref, k0, seq_len, cols_k, cols_v, chunk=KEY_CHUNK,
                ones_cols=False):
    step = min(seq_len, chunk)
    for c in range(0, seq_len, step):
        scores.append(_dot_nt(q, k_ref[pl.ds(k0 + c, step), cols_k]))
        v = v_ref[pl.ds(k0 + c, step), cols_v]
        if ones_cols:
            v = jnp.concatenate([v, jnp.ones((step, HEAD_DIM), BF16)], axis=1)
        values.append(v)


def _store_gated(o_ref, g_ref, o, r0, rows, first_head):
    if len(o) == 1:
        o = [o[0][g * rows:(g + 1) * rows] for g in range(GROUP)]
    for g in range(GROUP):
        cs = slice((first_head + g) * HEAD_DIM, (first_head + g + 1) * HEAD_DIM)
        gate = g_ref[pl.ds(r0, rows), cs].astype(F32)
        o_ref[pl.ds(r0, rows), cs] = (o[g] * gate).astype(BF16)


def _gqa_kernel(*refs, seq_len, qb, n_blocks, ctx, sink):
    it = iter(refs)
    sink_ref = next(it) if sink else None
    q_ref, k_ref, v_ref, g_ref = next(it), next(it), next(it), next(it)
    if ctx:
        ck_ref, cv_ref = next(it), next(it)
    o_ref = next(it)
    if ctx:
        kc_s, vc_s = next(it), next(it)
        for kv in range(N_KV):
            kc_s[kv] = ck_ref[kv].astype(BF16)
            vc_s[kv, :, 0:HEAD_DIM] = cv_ref[kv].astype(BF16)
            vc_s[kv, :, HEAD_DIM:2 * HEAD_DIM] = jnp.ones(cv_ref.shape[1:], BF16)
    blocks_per_seq = seq_len // qb

    def body(n, carry):
        r0 = pl.multiple_of(n * qb, qb)
        k0 = pl.multiple_of((n // blocks_per_seq) * seq_len, seq_len)
        for kv in range(N_KV):
            hs = slice(kv * HEAD_DIM, (kv + 1) * HEAD_DIM)
            q = _stack_heads(q_ref, r0, qb, kv * GROUP)
            scores, values = [], []
            if ctx:
                scores.append(_dot_nt(q, kc_s[kv]))
                values.append(vc_s[kv])
            _key_chunks(scores, values, q, k_ref, v_ref, k0, seq_len, hs, hs, ones_cols=True)
            sinks = _head_sinks(sink_ref, kv * GROUP) if sink else None
            o = _softmax_pv(scores, values, sinks, ones_cols=True)
            _store_gated(o_ref, g_ref, o, r0, qb, kv * GROUP)
        return carry

    lax.fori_loop(0, n_blocks, body, 0, unroll=UNROLL_SELF)


def _gqa(q, k, v, gate, seq_len, qb, rows_per_step, sink=None, ctx=None, layer_idx=0):
    m_rows = q.shape[0]
    kvw = N_KV * HEAD_DIM
    in_specs, args = [], []
    if sink is not None:
        in_specs.append(pl.BlockSpec(memory_space=pltpu.SMEM))
        args.append(sink)
    in_specs += [
        pl.BlockSpec((rows_per_step, HALF_W), lambda b: (b, 0)),
        pl.BlockSpec((rows_per_step, kvw), lambda b: (b, 0)),
        pl.BlockSpec((rows_per_step, kvw), lambda b: (b, 0)),
        pl.BlockSpec((rows_per_step, HALF_W), lambda b: (b, 0)),
    ]
    args += [q, k, v, gate]
    scratch = []
    if ctx is not None:
        past = ctx[0].shape[-2]
        cspec = pl.BlockSpec((None, None, N_KV, past, HEAD_DIM), lambda b: (b, layer_idx, 0, 0, 0))
        in_specs += [cspec, cspec]
        args += list(ctx)
        scratch = [pltpu.VMEM((N_KV, past, HEAD_DIM), BF16), pltpu.VMEM((N_KV, past, 2 * HEAD_DIM), BF16)]
    return pl.pallas_call(
        functools.partial(_gqa_kernel, seq_len=seq_len, qb=qb, n_blocks=rows_per_step // qb,
                          ctx=ctx is not None, sink=sink is not None),
        grid=(m_rows // rows_per_step,),
        in_specs=in_specs,
        out_specs=pl.BlockSpec((rows_per_step, HALF_W), lambda b: (b, 0)),
        out_shape=jax.ShapeDtypeStruct((m_rows, HALF_W), BF16),
        scratch_shapes=scratch,
        compiler_params=_params(),
        name="gqa",
    )(*args)


def _banded_kernel(sink_ref, q_ref, k_ref, v_ref, g_ref, ck_ref, cv_ref, o_ref,
                   kpad, vpad, kc_s, vc_s, *, seq_len):
    n_blocks = seq_len // WINDOW
    zeros = jnp.zeros((WINDOW, HEAD_DIM), BF16)
    for kv in range(N_KV):
        hs = slice(kv * HEAD_DIM, (kv + 1) * HEAD_DIM)
        kpad[kv, 0:WINDOW] = zeros
        vpad[kv, 0:WINDOW, 0:HEAD_DIM] = zeros
        kpad[kv, WINDOW + seq_len:2 * WINDOW + seq_len] = zeros
        vpad[kv, WINDOW + seq_len:2 * WINDOW + seq_len, 0:HEAD_DIM] = zeros
        kpad[kv, WINDOW:WINDOW + seq_len] = k_ref[:, hs]
        vpad[kv, WINDOW:WINDOW + seq_len, 0:HEAD_DIM] = v_ref[:, hs]
        kc_s[kv] = ck_ref[kv].astype(BF16)
        vc_s[kv, :, 0:HEAD_DIM] = cv_ref[kv].astype(BF16)
        vpad[kv, :, HEAD_DIM:2 * HEAD_DIM] = jnp.ones((seq_len + 2 * WINDOW, HEAD_DIM), BF16)
        vc_s[kv, :, HEAD_DIM:2 * HEAD_DIM] = jnp.ones(cv_ref.shape[1:], BF16)

    rows = GROUP * WINDOW
    qi = lax.broadcasted_iota(jnp.int32, (rows, 3 * WINDOW), 0) % WINDOW
    col = lax.broadcasted_iota(jnp.int32, (rows, 3 * WINDOW), 1)
    kj = col % WINDOW
    blk = col // WINDOW
    never = 4 * WINDOW

    def body(n, carry):
        r0 = pl.multiple_of(n * WINDOW, WINDOW)
        lo = jnp.where(n > 0, 0, never)
        hi = jnp.where(n < n_blocks - 1, 0, never)
        slack = jnp.where(blk == 0, kj - qi - lo, jnp.where(blk == 2, qi - kj - hi, 0))
        valid = slack >= 0
        for kv in range(N_KV):
            q = _stack_heads(q_ref, r0, WINDOW, kv * GROUP)
            kb = kpad[kv, pl.ds(r0, 3 * WINDOW), :]
            vb = vpad[kv, pl.ds(r0, 3 * WINDOW), :]
            s_ctx = _dot_nt(q, kc_s[kv])
            s_band = jnp.where(valid, _dot_nt(q, kb), NEG_INF)
            o = _softmax_pv([s_ctx, s_band], [vc_s[kv], vb], _head_sinks(sink_ref, kv * GROUP),
                            ones_cols=True)
            _store_gated(o_ref, g_ref, o, r0, WINDOW, kv * GROUP)
        return carry

    lax.fori_loop(0, n_blocks, body, 0, unroll=UNROLL_CTX)


def _banded(q, k, v, gate, sink, cache_k, cache_v, layer_idx, seq_len):
    m_rows = q.shape[0]
    kvw = N_KV * HEAD_DIM
    past = cache_k.shape[-2]
    cspec = pl.BlockSpec((None, None, N_KV, past, HEAD_DIM), lambda b: (b, layer_idx, 0, 0, 0))
    pad_rows = seq_len + 2 * WINDOW
    return pl.pallas_call(
        functools.partial(_banded_kernel, seq_len=seq_len),
        grid=(m_rows // seq_len,),
        in_specs=[
            pl.BlockSpec(memory_space=pltpu.SMEM),
            pl.BlockSpec((seq_len, HALF_W), lambda b: (b, 0)),
            pl.BlockSpec((seq_len, kvw), lambda b: (b, 0)),
            pl.BlockSpec((seq_len, kvw), lambda b: (b, 0)),
            pl.BlockSpec((seq_len, HALF_W), lambda b: (b, 0)),
            cspec, cspec,
        ],
        out_specs=pl.BlockSpec((seq_len, HALF_W), lambda b: (b, 0)),
        out_shape=jax.ShapeDtypeStruct((m_rows, HALF_W), BF16),
        scratch_shapes=[
            pltpu.VMEM((N_KV, pad_rows, HEAD_DIM), BF16),
            pltpu.VMEM((N_KV, pad_rows, 2 * HEAD_DIM), BF16),
            pltpu.VMEM((N_KV, past, HEAD_DIM), BF16),
            pltpu.VMEM((N_KV, past, 2 * HEAD_DIM), BF16),
        ],
        compiler_params=_params(),
        name="banded",
    )(sink, q, k, v, gate, cache_k, cache_v)


def _diff_kernel(*refs, seq_len, qb, n_blocks, ctx, lam_init):
    it = iter(refs)
    lam_ref, on_ref = next(it), next(it)
    q_ref, k_ref, v_ref, g_ref = next(it), next(it), next(it), next(it)
    if ctx:
        ck_ref, cv_ref = next(it), next(it)
    o_ref = next(it)
    if ctx:
        kc_s, vc_s = next(it), next(it)
        for slot in range(2 * C_HEADS):
            kc_s[slot] = ck_ref[slot].astype(BF16)
        for hd in range(C_HEADS):
            vc_s[hd] = cv_ref[hd].astype(BF16)
    lf = lam_ref[...]
    lam = (jnp.exp(jnp.sum(lf[0:1] * lf[1:2], axis=-1, keepdims=True))
           - jnp.exp(jnp.sum(lf[2:3] * lf[3:4], axis=-1, keepdims=True)) + lam_init)
    out_gain = on_ref[...]
    blocks_per_seq = seq_len // qb

    def body(n, carry):
        r0 = pl.multiple_of(n * qb, qb)
        k0 = pl.multiple_of((n // blocks_per_seq) * seq_len, seq_len)
        for hd in range(C_HEADS):
            vs = slice(hd * C_V_DIM, (hd + 1) * C_V_DIM)
            maps = []
            for mp in range(2):
                slot = mp * C_HEADS + hd
                hs = slice(slot * HEAD_DIM, (slot + 1) * HEAD_DIM)
                q = q_ref[pl.ds(r0, qb), hs]
                scores, values = [], []
                if ctx:
                    scores.append(_dot_nt(q, kc_s[slot]))
                    values.append(vc_s[hd])
                _key_chunks(scores, values, q, k_ref, v_ref, k0, seq_len, hs, vs, chunk=seq_len)
                maps.append(_softmax_pv(scores, values, joint=True)[0])
            d = maps[0] - lam * maps[1]
            r = lax.rsqrt(jnp.mean(d * d, axis=-1, keepdims=True) + EPS)
            y = ((d * r) * out_gain) * (1.0 - lam_init)
            gate = g_ref[pl.ds(r0, qb), vs].astype(F32)
            o_ref[pl.ds(r0, qb), vs] = (y * gate).astype(BF16)
        return carry

    lax.fori_loop(0, n_blocks, body, 0, unroll=UNROLL_CTX)


def _diff(q, k, v, gate, c_lam, c_on, lam_init, seq_len, qb, rows_per_step, ctx=None, layer_idx=0):
    m_rows = q.shape[0]
    in_specs = [
        pl.BlockSpec((4, HEAD_DIM), lambda b: (0, 0)),
        pl.BlockSpec((1, C_V_DIM), lambda b: (0, 0)),
    ] + [pl.BlockSpec((rows_per_step, HALF_W), lambda b: (b, 0))] * 4
    args = [c_lam, c_on, q, k, v, gate]
    scratch = []
    if ctx is not None:
        past = ctx[0].shape[-2]
        in_specs += [
            pl.BlockSpec((None, None, 2 * C_HEADS, past, HEAD_DIM), lambda b: (b, layer_idx, 0, 0, 0)),
            pl.BlockSpec((None, None, C_HEADS, past, C_V_DIM), lambda b: (b, layer_idx, 0, 0, 0)),
        ]
        args += list(ctx)
        scratch = [pltpu.VMEM((2 * C_HEADS, past, HEAD_DIM), BF16),
                   pltpu.VMEM((C_HEADS, past, C_V_DIM), BF16)]
    return pl.pallas_call(
        functools.partial(_diff_kernel, seq_len=seq_len, qb=qb, n_blocks=rows_per_step // qb,
                          ctx=ctx is not None, lam_init=lam_init),
        grid=(m_rows // rows_per_step,),
        in_specs=in_specs,
        out_specs=pl.BlockSpec((rows_per_step, HALF_W), lambda b: (b, 0)),
        out_shape=jax.ShapeDtypeStruct((m_rows, HALF_W), BF16),
        scratch_shapes=scratch,
        compiler_params=_params(),
        name="diff_attn",
    )(*args)


def _outproj_kernel(*refs, bm, emit_h, conv_seq):
    it = iter(refs)
    y1_ref = next(it)
    if conv_seq is None:
        y2_ref = next(it)
    else:
        u_ref, b_ref, c_ref, g_ref = next(it), next(it), next(it), next(it)
        up_ref, cp_ref, un_ref, cn_ref = next(it), next(it), next(it), next(it)
        cw_ref = next(it)
    x_ref = next(it)
    w_refs = [next(it) for _ in range(D_MODEL // PROJ_CHUNK)]
    gt_ref = next(it)
    if emit_h:
        gn_ref, sh_ref, sc_ref = next(it), next(it), next(it)
    xo_ref = next(it)
    if emit_h:
        ho_ref = next(it)
    chunks = [slice(c, c + PROJ_CHUNK) for c in range(0, D_MODEL, PROJ_CHUNK)]
    w_top = [w.at[0:HALF_W, :] for w in w_refs]
    w_bot = [w.at[HALF_W:D_MODEL, :] for w in w_refs]
    if conv_seq is not None:
        assert bm % conv_seq == 0 or conv_seq % bm == 0
        row = lax.broadcasted_iota(jnp.int32, (bm, 1), 0)
        pos = (pl.program_id(0) * bm + row) % conv_seq
        first_row = jnp.logical_or(pos == 0, row == 0)
        last_row = jnp.logical_or(pos == conv_seq - 1, row == bm - 1)
        tile_pos = (pl.program_id(0) * bm) % conv_seq
        use_before = jnp.where(tile_pos == 0, 0.0, 1.0)
        use_after = jnp.where(tile_pos + bm == conv_seq, 0.0, 1.0) if conv_seq % bm == 0 else 0.0
        lanes = HALF_W // len(chunks)
        y2_parts = []
        for ci, cs in enumerate(chunks):
            ls = slice(ci * lanes, (ci + 1) * lanes)
            z = c_ref[:, ls].astype(F32) * u_ref[:, ls].astype(F32)
            z_before = (cp_ref[:, ls].astype(F32) * up_ref[:, ls].astype(F32))[HALO_ROWS - 1:HALO_ROWS] * use_before
            z_after = (cn_ref[:, ls].astype(F32) * un_ref[:, ls].astype(F32))[0:1] * use_after
            z_prev = jnp.where(first_row, z_before, pltpu.roll(z, 1, axis=0))
            z_next = jnp.where(last_row, z_after, pltpu.roll(z, bm - 1, axis=0))
            conv = z_prev * cw_ref[0:1, ls] + z * cw_ref[1:2, ls] + z_next * cw_ref[2:3, ls]
            y2_parts.append(((b_ref[:, ls].astype(F32) * conv) * g_ref[:, ls].astype(F32)).astype(BF16))
            xo_ref[:, cs] = x_ref[:, cs] + gt_ref[:, cs] * _dot(y1_ref[...], w_top[ci][...])
        y2 = jnp.concatenate(y2_parts, axis=1)
    ssq = jnp.zeros((bm, 1), F32)
    for ci, cs in enumerate(chunks):
        if conv_seq is None:
            acc = _dot(y1_ref[...], w_top[ci][...]) + _dot(y2_ref[...], w_bot[ci][...])
            xn = x_ref[:, cs] + gt_ref[:, cs] * acc
        else:
            xn = xo_ref[:, cs] + gt_ref[:, cs] * _dot(y2, w_bot[ci][...])
        xo_ref[:, cs] = xn
        if emit_h:
            ssq = ssq + jnp.sum(xn * xn, axis=-1, keepdims=True)
    if emit_h:
        r = lax.rsqrt(ssq * (1.0 / D_MODEL) + EPS)
        for cs in chunks:
            gain = gn_ref[:, cs] * (1.0 + sc_ref[:, cs])
            ho_ref[:, cs] = ((xo_ref[:, cs] * r) * gain + sh_ref[:, cs]).astype(BF16)


def _outproj(y1, y2, x, w_out, norm_g3, mod4, layer, row_fn, emit_h, conv=None):
    m_rows = x.shape[0]
    bm = PROJ_ROWS
    half_spec = pl.BlockSpec((bm, HALF_W), lambda m: (m, 0))
    in_specs = [half_spec]
    args = [y1]
    scratch = []
    if conv is None:
        in_specs.append(half_spec)
        args.append(y2)
    else:
        u, bgate, cgate, gate, conv_w, _ = conv
        halo_blocks = bm // HALO_ROWS
        last_halo = m_rows // HALO_ROWS - 1
        before = pl.BlockSpec((HALO_ROWS, HALF_W), lambda m: (jnp.maximum(m * halo_blocks - 1, 0), 0))
        after = pl.BlockSpec((HALO_ROWS, HALF_W), lambda m: (jnp.minimum((m + 1) * halo_blocks, last_halo), 0))
        in_specs += [half_spec] * 4 + [before, before, after, after,
                                       pl.BlockSpec((3, HALF_W), lambda m: (0, 0))]
        args += [u, bgate, cgate, gate, u, cgate, u, cgate, conv_w]
    w_specs, w_args = _weight_tile_specs(w_out)
    in_specs += [pl.BlockSpec((bm, D_MODEL), lambda m: (m, 0))] + w_specs + [_mod_spec(layer, 2, row_fn)]
    args += [x] + w_args + [mod4]
    out_specs = [pl.BlockSpec((bm, D_MODEL), lambda m: (m, 0))]
    out_shape = [jax.ShapeDtypeStruct((m_rows, D_MODEL), F32)]
    if emit_h:
        in_specs += [
            pl.BlockSpec((None, 1, D_MODEL), lambda m: (layer + 1, 0, 0)),
            _mod_spec(layer + 1, 0, row_fn),
            _mod_spec(layer + 1, 1, row_fn),
        ]
        args += [norm_g3, mod4, mod4]
        out_specs.append(pl.BlockSpec((bm, D_MODEL), lambda m: (m, 0)))
        out_shape.append(jax.ShapeDtypeStruct((m_rows, D_MODEL), BF16))
    outs = pl.pallas_call(
        functools.partial(_outproj_kernel, bm=bm, emit_h=emit_h,
                          conv_seq=None if conv is None else conv[5]),
        grid=(m_rows // bm,),
        in_specs=in_specs,
        out_specs=out_specs,
        out_shape=out_shape,
        scratch_shapes=scratch,
        compiler_params=_params(),
        name="outproj",
    )(*args)
    return (outs[0], outs[1]) if emit_h else (outs[0], None)


def _rope_tables(n_tokens):
    t = jnp.arange(n_tokens)
    row = t // GRID_W
    col = t % GRID_W
    n_freq = HEAD_DIM // 4
    inv = ROPE_THETA ** (-jnp.arange(n_freq, dtype=F32) / n_freq)
    ang_r = row[:, None] * inv
    ang_c = col[:, None] * inv
    cos = jnp.concatenate([jnp.cos(ang_r)] * 2 + [jnp.cos(ang_c)] * 2, axis=1)
    sin = jnp.concatenate([-jnp.sin(ang_r), jnp.sin(ang_r), -jnp.sin(ang_c), jnp.sin(ang_c)], axis=1)
    return cos, sin


def kernel(x_prompt, x_sample, cache_a_k, cache_a_v, cache_c_k, cache_c_v, cache_d_k, cache_d_v,
           c, c_ctx, norm_g, ada_w, ada_b, ev_w_in, ev_w_out, a_q_norm, a_k_norm, a_sink, b_conv,
           od_w_in, od_w_out, c_q_norm, c_k_norm, c_lambda, c_out_norm, d_q_norm, d_k_norm):
    n_prompt, seq_p, _ = x_prompt.shape
    n_sample, seq_s, _ = x_sample.shape
    past = cache_a_k.shape[-2]
    scale = HEAD_DIM ** -0.5 * LOG2E

    cond = jnp.concatenate(
        [c, c_ctx[None, :], jnp.zeros((COND_ROWS - n_sample - 1, D_MODEL), F32)], axis=0)
    mod4 = _modulation(cond, ada_w, ada_b).reshape(DEPTH, COND_ROWS, 1, 3 * D_MODEL)
    norm_g3 = norm_g.reshape(DEPTH, 1, D_MODEL)
    rope_tabs = _rope_tables(seq_s)

    prompt_row = lambda m: CTX_ROW
    sample_row = lambda m: (m * PROJ_ROWS) // seq_s

    xp = x_prompt.reshape(n_prompt * seq_p, D_MODEL)
    xs = x_sample.reshape(n_sample * seq_s, D_MODEL)
    hp, hs = xp, xs

    ev_in_tiles, ev_out_tiles = ev_w_in.astype(BF16), ev_w_out.astype(BF16)
    od_in_tiles, od_out_tiles = od_w_in.astype(BF16), od_w_out.astype(BF16)
    cache_c_k8 = cache_c_k.reshape(n_sample, DEPTH // 2, 2 * C_HEADS, past, HEAD_DIM)
    prompt_rows = PROMPT_SEQS_PER_STEP * seq_p

    new_a = None
    new_c = (None, None)
    for layer in range(DEPTH):
        i = layer // 2
        last = layer == DEPTH - 1
        if layer % 2 == 0:
            w_out = (ev_out_tiles, i, 0, D_MODEL // PROJ_CHUNK)
            gq = a_q_norm[i] * scale
            gk = a_k_norm[i]
            kv_cache = (N_KV, HEAD_DIM)

            def project(h, seq_len, tabs, cache, prev, row_fn):
                segs1 = [_Seg("norm", HALF_W), _Seg("norm", N_KV * HEAD_DIM, cache),
                         _Seg("plain", N_KV * HEAD_DIM, cache), _Seg("silu", HALF_W)]
                segs_all = segs1 + [_Seg("plain", HALF_W)] * 3 + [_Seg("silu", HALF_W)]
                outs, caches = _inproj(h, (ev_in_tiles, i, 0, ALL_TILES), segs_all, [gq, gk] + [None] * 6,
                                       tabs, seq_len, cache_slot=i, cache_prev=prev, bm=PROJ_ROWS,
                                       prenorm=(norm_g3, mod4, layer, row_fn) if layer == 0 else None)
                return (*outs, caches)

            q, k, v, ga, u, bg, cg, gb, new_a = project(hp, seq_p, None, kv_cache, new_a, prompt_row)
            y1p = _gqa(q, k, v, ga, seq_p, seq_p, prompt_rows, sink=a_sink[i])
            y2p, conv_p = None, (u, bg, cg, gb, b_conv[i], seq_p)

            q, k, v, ga, u, bg, cg, gb, _ = project(hs, seq_s, rope_tabs, None, None, sample_row)
            y1s = _banded(q, k, v, ga, a_sink[i], cache_a_k, cache_a_v, i, seq_s)
            y2s, conv_s = None, (u, bg, cg, gb, b_conv[i], seq_s)
        else:
            lam_init = 0.8 - 0.6 * math.exp(-0.3 * layer)
            w_out = (od_out_tiles, i, 0, D_MODEL // PROJ_CHUNK)
            gcq = c_q_norm[i] * scale
            gck = c_k_norm[i]
            gdq = d_q_norm[i] * scale
            gdk = d_k_norm[i]
            c_on = c_out_norm[i].reshape(1, C_V_DIM)

            def project(h, seq_len, tabs, with_cache, prev=(None, None)):
                ck_cache = (2 * C_HEADS, HEAD_DIM) if with_cache else None
                cv_cache = (C_HEADS, C_V_DIM) if with_cache else None
                kv_cache = (N_KV, HEAD_DIM) if with_cache else None
                segs_c = [_Seg("norm", HALF_W), _Seg("norm", HALF_W, ck_cache),
                          _Seg("plain", HALF_W, cv_cache), _Seg("silu", HALF_W)]
                segs_d = [_Seg("norm", HALF_W), _Seg("norm", N_KV * HEAD_DIM, kv_cache),
                          _Seg("plain", N_KV * HEAD_DIM, kv_cache), _Seg("silu", HALF_W)]
                gains_c, gains_d = [gcq, gck, None, None], [gdq, gdk, None, None]
                if not with_cache:
                    outs, _ = _inproj(h, (od_in_tiles, i, 0, ALL_TILES), segs_c + segs_d, gains_c + gains_d,
                                      tabs, seq_len, bm=PROJ_ROWS)
                    return (*outs, None)
                outs_c, cc = _inproj(h, (od_in_tiles, i, 0, C_TILES), segs_c, gains_c, tabs, seq_len,
                                     cache_slot=i, cache_prev=prev[0], bm=PROJ_ROWS)
                outs_d, cd = _inproj(h, (od_in_tiles, i, C_TILES, ALL_TILES - C_TILES), segs_d, gains_d,
                                     tabs, seq_len, cache_slot=i, cache_prev=prev[1])
                return (*outs_c, *outs_d, (cc, cd))

            cq, ck, cv, cgate, dq, dk, dv, dgate, new_c = project(hp, seq_p, None, True, new_c)
            y1p = _diff(cq, ck, cv, cgate, c_lambda[i], c_on, lam_init, seq_p, seq_p, prompt_rows)
            y2p = _gqa(dq, dk, dv, dgate, seq_p, seq_p, prompt_rows)

            cq, ck, cv, cgate, dq, dk, dv, dgate, _ = project(hs, seq_s, rope_tabs, False)
            y1s = _diff(cq, ck, cv, cgate, c_lambda[i], c_on, lam_init, seq_s, 2 * SAMPLE_QB, seq_s,
                        ctx=(cache_c_k8, cache_c_v), layer_idx=i)
            y2s = _gqa(dq, dk, dv, dgate, seq_s, 2 * SAMPLE_QB, seq_s,
                       ctx=(cache_d_k, cache_d_v), layer_idx=i)
            conv_p = conv_s = None

        xp, hp = _outproj(y1p, y2p, xp, w_out, norm_g3, mod4, layer, prompt_row, not last, conv_p)
        xs, hs = _outproj(y1s, y2s, xs, w_out, norm_g3, mod4, layer, sample_row, not last, conv_s)

    return (xp.reshape(n_prompt, seq_p, D_MODEL), xs.reshape(n_sample, seq_s, D_MODEL),
            new_a[0], new_a[1],
            new_c[0][0].reshape(n_prompt, CACHE_SLOTS, 2, C_HEADS, seq_p, HEAD_DIM), new_c[0][1],
            new_c[1][0], new_c[1][1])
```
